```python
import jax
import jax.numpy as jnp
from jax import lax
import numpy as np

D_MODEL = 2048
BATCH = 1
SEQ = 16384
DEPTH = 2

GRID_W = 64
CTX_LEN = 256
NORM_EPS = 1e-6
CHUNK = 32

MIX_HALF = D_MODEL // 2

HEAD_DIM = 128
ATT_HEADS = MIX_HALF // HEAD_DIM
ATT_KV_HEADS = 2
WINDOW = 128
ATT_BLOCK = 128
ROPE_THETA = 10000.0

HGRN_HEADS = 8
HGRN_DK = 128
HGRN_DV = MIX_HALF // HGRN_HEADS
N_HGRN_LAYERS = (DEPTH + 1) // 2

GLA_HEADS = 4
GLA_DK = MIX_HALF // 2 // GLA_HEADS
GLA_DV = MIX_HALF // GLA_HEADS
GLA_GATE_RANK = 16
GLA_GATE_NORMALIZER = 16.0

RWKV_N = 64
RWKV_HEADS = MIX_HALF // RWKV_N
RWKV_DECAY_RANK = 96
RWKV_A_RANK = 96
RWKV_GATE_RANK = 256
RWKV_LN_EPS = 64e-5

D_FF = 5632
N_EXPERTS = 8
TOP_K = 2
D_FF_EXPERT = 7168

ATT_Q = ATT_HEADS * HEAD_DIM
ATT_KV = ATT_KV_HEADS * HEAD_DIM
HGRN_K = HGRN_HEADS * HGRN_DK
HGRN_V = HGRN_HEADS * HGRN_DV
EVEN_COLS = (ATT_Q, ATT_KV, ATT_KV, HGRN_K, HGRN_V, HGRN_K, HGRN_K, HGRN_V)
EVEN_IN = sum(EVEN_COLS)
EVEN_OUT = ATT_Q + HGRN_V
GLA_K = GLA_HEADS * GLA_DK
GLA_V = GLA_HEADS * GLA_DV
GLA_COLS = (GLA_K, GLA_K, GLA_V, GLA_GATE_RANK, GLA_GATE_RANK, GLA_V)
GLA_IN = sum(GLA_COLS)
RWKV_C = RWKV_HEADS * RWKV_N
RWKV_COLS = (RWKV_C, RWKV_C, RWKV_C, RWKV_DECAY_RANK, RWKV_DECAY_RANK, RWKV_A_RANK, RWKV_GATE_RANK)
RWKV_IN = sum(RWKV_COLS)
ODD_IN = GLA_IN + RWKV_IN
ODD_OUT = GLA_V + RWKV_C

kernel_name = 'hybrid_diffusion_trunk'


def rmsnorm(x, gain):
    xf = x.astype(jnp.float32)
    y = xf * lax.rsqrt(jnp.mean(xf * xf, axis=-1, keepdims=True) + NORM_EPS)
    return (y * gain.astype(jnp.float32)).astype(x.dtype)


def split_last(t, sizes):
    return jnp.split(t, np.cumsum(sizes)[:-1].tolist(), axis=-1)


def to_heads(t, n_heads):
    return t.reshape(t.shape[:-1] + (n_heads, -1))


def adaln(cvec, w, b):
    return jnp.split(jax.nn.silu(cvec) @ w + b, 6, axis=-1)


def axial_rope_tables(rows):
    n_freq = HEAD_DIM // 4
    t = jnp.arange(rows * GRID_W)
    row = (t // GRID_W).astype(jnp.float32)
    col = (t % GRID_W).astype(jnp.float32)
    inv = ROPE_THETA ** (-jnp.arange(n_freq, dtype=jnp.float32) / n_freq)
    ang = jnp.stack([row[:, None] * inv, col[:, None] * inv], axis=1)
    return jnp.cos(ang), jnp.sin(ang)


def apply_axial_rope(x, cos, sin):
    n_freq = HEAD_DIM // 4
    xr = x.reshape(x.shape[:-1] + (2, 2, n_freq))
    x1, x2 = xr[..., 0, :], xr[..., 1, :]
    cb, sb = cos[None, :, None], sin[None, :, None]
    out = jnp.stack([x1 * cb - x2 * sb, x1 * sb + x2 * cb], axis=-2)
    return out.reshape(x.shape)


def sink_softmax(scores, sink):
    m = sink
    for s in scores:
        m = jnp.maximum(m, jnp.max(s, axis=-1, keepdims=True))
    probs = [jnp.exp(s - m) for s in scores]
    denom = jnp.exp(sink - m)
    for p in probs:
        denom = denom + jnp.sum(p, axis=-1, keepdims=True)
    return [p / denom for p in probs]


def window_sink_attention(q, k, v, kc, vc, sink):
    B, S, H, Dh = q.shape
    G = k.shape[2]
    R = H // G
    nb = S // ATT_BLOCK
    qb = q.reshape(B, nb, ATT_BLOCK, G, R, Dh) * (Dh ** -0.5)

    def band(t):
        pad = jnp.zeros((B, ATT_BLOCK) + t.shape[2:], t.dtype)
        tb = jnp.concatenate([pad, t, pad], axis=1).reshape((B, nb + 2, ATT_BLOCK) + t.shape[2:])
        return jnp.concatenate([tb[:, :-2], tb[:, 1:-1], tb[:, 2:]], axis=2)

    kw, vw = band(k), band(v)
    s_win = jnp.einsum('bnqgrd,bnkgd->bngrqk', qb, kw)
    s_ctx = jnp.einsum('bnqgrd,bcgd->bngrqc', qb, kc)
    qi = jnp.arange(ATT_BLOCK)[:, None]
    kj = jnp.arange(3 * ATT_BLOCK)[None, :]
    kpos = jnp.arange(nb)[:, None, None] * ATT_BLOCK + (kj - ATT_BLOCK)[None]
    valid = (jnp.abs(kj - ATT_BLOCK - qi) <= WINDOW)[None] & (kpos >= 0) & (kpos < S)
    s_win = jnp.where(valid[None, :, None, None], s_win, -jnp.inf)
    p_win, p_ctx = sink_softmax([s_win, s_ctx], sink.reshape(G, R)[None, None, :, :, None, None])
    o = jnp.einsum('bngrqk,bnkgd->bnqgrd', p_win, vw) + jnp.einsum('bngrqc,bcgd->bnqgrd', p_ctx, vc)
    return o.reshape(B, S, H * Dh)


def context_sink_attention(qc, kc, vc, sink):
    B, L, H, Dh = qc.shape
    G = kc.shape[2]
    R = H // G
    qg = qc.reshape(B, L, G, R, Dh) * (Dh ** -0.5)
    s = jnp.einsum('blgrd,bcgd->bgrlc', qg, kc)
    (p,) = sink_softmax([s], sink.reshape(G, R)[None, :, :, None, None])
    return jnp.einsum('bgrlc,bcgd->blgrd', p, vc).reshape(B, L, H * Dh)


def chunk_gated_recurrence(q, k, v, log_decay, state0, reverse):
    if reverse:
        q, k, v, log_decay = [jnp.flip(t, axis=1) for t in (q, k, v, log_decay)]
    B, T, H, K = q.shape
    V = v.shape[-1]
    n = T // CHUNK
    qc, kc, gc = [t.reshape(B, n, CHUNK, H, K) for t in (q, k, log_decay)]
    vc = v.reshape(B, n, CHUNK, H, V)
    G = jnp.cumsum(gc, axis=2)
    G_last = G[:, :, -1:]
    q_dec = qc * jnp.exp(G)
    k_inv = kc * jnp.exp(-G)
    k_tail = kc * jnp.exp(G_last - G)
    causal = jnp.tril(jnp.ones((CHUNK, CHUNK), dtype=bool))
    A = jnp.where(causal, jnp.einsum('bnihk,bnjhk->bnhij', q_dec, k_inv), 0.0)
    o = jnp.einsum('bnhij,bnjhv->bnihv', A, vc)
    kv = jnp.einsum('bnjhk,bnjhv->nbhkv', k_tail, vc)
    decay = jnp.moveaxis(jnp.exp(G_last[:, :, 0]), 1, 0)

    def step(S, inp):
        d, kv_n = inp
        return d[..., None] * S + kv_n, S

    S_final, S_start = lax.scan(step, state0, (decay, kv))
    o = o + jnp.einsum('bnihk,nbhkv->bnihv', q_dec, S_start)
    o = o.reshape(B, T, H, V)
    if reverse:
        o = jnp.flip(o, axis=1)
    return o, S_final


def rwkv7_scan(r, w, k, v, a, b, state0, reverse):
    def step(S, inp):
        r_t, w_t, k_t, v_t, a_t, b_t = inp
        sa = jnp.einsum('bhvk,bhk->bhv', S, a_t)
        S = S * w_t[:, :, None, :] + sa[..., None] * b_t[:, :, None, :] + v_t[..., None] * k_t[:, :, None, :]
        return S, jnp.einsum('bhvk,bhk->bhv', S, r_t)

    xs = tuple(jnp.moveaxis(t, 1, 0) for t in (r, w, k, v, a, b))
    S_final, o = lax.scan(step, state0, xs, reverse=reverse)
    return jnp.moveaxis(o, 0, 1), S_final


def bidirectional_prefix_scan(scan_fn, ctx_fwd, ctx_bwd, lat_fwd, lat_bwd, state0):
    oc_f, s_f = scan_fn(*ctx_fwd, state0, False)
    oc_b, s_b = scan_fn(*ctx_bwd, state0, True)
    ol_f, _ = scan_fn(*lat_fwd, s_f, False)
    ol_b, _ = scan_fn(*lat_bwd, s_b, True)
    return oc_f + oc_b, ol_f + ol_b


def centred_token_shift(p, mu_prev, mu_next):
    zero = jnp.zeros_like(p[:, :1])
    prev = jnp.concatenate([zero, p[:, :-1]], axis=1)
    nxt = jnp.concatenate([p[:, 1:], zero], axis=1)
    return p + mu_prev * (prev - p) + mu_next * (nxt - p)


def even_mixer(h, hc, w_in, w_out, attn_sink, hgrn_norm, hgrn_lb, rope_cos, rope_sin, need_ctx):
    B, S, _ = h.shape
    L = hc.shape[1]
    T = L + S
    ctx_sl, lat_sl = slice(0, L), slice(L, T)
    proj = (jnp.concatenate([hc, h], axis=1) @ w_in).astype(jnp.float32)
    qa, ka, va, qh, ih, f_fw, f_bw, gh = split_last(proj, EVEN_COLS)
    qa = to_heads(qa, ATT_HEADS)
    ka = to_heads(ka, ATT_KV_HEADS)
    va = to_heads(va, ATT_KV_HEADS)
    att = window_sink_attention(apply_axial_rope(qa[:, lat_sl], rope_cos, rope_sin),
                                apply_axial_rope(ka[:, lat_sl], rope_cos, rope_sin),
                                va[:, lat_sl], ka[:, ctx_sl], va[:, ctx_sl], attn_sink)
    qh = to_heads(jax.nn.silu(qh), HGRN_HEADS)
    ih = to_heads(ih, HGRN_HEADS)
    f_fw = to_heads(hgrn_lb + (1.0 - hgrn_lb) * jax.nn.sigmoid(f_fw), HGRN_HEADS)
    f_bw = to_heads(hgrn_lb + (1.0 - hgrn_lb) * jax.nn.sigmoid(f_bw), HGRN_HEADS)

    def hg_in(f, sl):
        return (qh[:, sl], 1.0 - f[:, sl], ih[:, sl], jnp.log(f[:, sl]))

    state0 = jnp.zeros((B, HGRN_HEADS, HGRN_DK, HGRN_DV), jnp.float32)
    o_ctx, o_lat = bidirectional_prefix_scan(chunk_gated_recurrence, hg_in(f_fw, ctx_sl), hg_in(f_bw, ctx_sl),
                                             hg_in(f_fw, lat_sl), hg_in(f_bw, lat_sl), state0)
    out_gate = jax.nn.silu(gh)

    def hg_out(o, sl):
        return rmsnorm(o, hgrn_norm).reshape(B, -1, HGRN_V) * out_gate[:, sl]

    y = jnp.concatenate([att, hg_out(o_lat, lat_sl)], axis=-1).astype(h.dtype) @ w_out
    if not need_ctx:
        return y, None
    att_c = context_sink_attention(qa[:, ctx_sl], ka[:, ctx_sl], va[:, ctx_sl], attn_sink)
    yc = jnp.concatenate([att_c, hg_out(o_ctx, ctx_sl)], axis=-1).astype(h.dtype) @ w_out
    return y, yc


def odd_mixer(h, hc, w_in, w_out, gla_gate_up_f, gla_gate_up_b, gla_gate_bias_f, gla_gate_bias_b, gla_norm,
              rwkv_mu_prev, rwkv_mu_next, rwkv_w0_f, rwkv_w0_b, rwkv_w2_f, rwkv_w2_b, rwkv_a0, rwkv_a2,
              rwkv_g2, rwkv_k_k, rwkv_k_a, rwkv_r_k, rwkv_ln_w, rwkv_ln_b, need_ctx):
    B, S, _ = h.shape
    L = hc.shape[1]
    T = L + S
    ctx_sl, lat_sl = slice(0, L), slice(L, T)
    proj = (jnp.concatenate([hc, h], axis=1) @ w_in).astype(jnp.float32)
    gla_p, rwkv_p = proj[..., :GLA_IN], proj[..., GLA_IN:]
    gq, gk, gv, gd_f, gd_b, gr = split_last(gla_p, GLA_COLS)
    gq = to_heads(gq, GLA_HEADS) * (GLA_DK ** -0.5)
    gk = to_heads(gk, GLA_HEADS)
    gv = to_heads(gv, GLA_HEADS)
    lg_f = to_heads(jax.nn.log_sigmoid(gd_f @ gla_gate_up_f + gla_gate_bias_f) / GLA_GATE_NORMALIZER, GLA_HEADS)
    lg_b = to_heads(jax.nn.log_sigmoid(gd_b @ gla_gate_up_b + gla_gate_bias_b) / GLA_GATE_NORMALIZER, GLA_HEADS)

    def gla_in(lg, sl):
        return (gq[:, sl], gk[:, sl], gv[:, sl], lg[:, sl])

    gstate0 = jnp.zeros((B, GLA_HEADS, GLA_DK, GLA_DV), jnp.float32)
    go_ctx, go_lat = bidirectional_prefix_scan(chunk_gated_recurrence, gla_in(lg_f, ctx_sl), gla_in(lg_b, ctx_sl),
                                               gla_in(lg_f, lat_sl), gla_in(lg_b, lat_sl), gstate0)
    gla_gate = jax.nn.silu(gr)

    def gla_out(o, sl):
        return rmsnorm(o, gla_norm).reshape(B, -1, GLA_V) * gla_gate[:, sl]

    rw = jnp.concatenate([centred_token_shift(rwkv_p[:, ctx_sl], rwkv_mu_prev, rwkv_mu_next),
                          centred_token_shift(rwkv_p[:, lat_sl], rwkv_mu_prev, rwkv_mu_next)], axis=1)
    rr, rk, rv, wd_f, wd_b, ad, gd = split_last(rw, RWKV_COLS)

    def decay(w0, wd, w2):
        wl = -jax.nn.softplus(-(w0 + jnp.tanh(wd) @ w2)) - 0.5
        return to_heads(jnp.exp(-jnp.exp(wl)), RWKV_HEADS)

    dec_f = decay(rwkv_w0_f, wd_f, rwkv_w2_f)
    dec_b = decay(rwkv_w0_b, wd_b, rwkv_w2_b)
    a = jax.nn.sigmoid(rwkv_a0 + ad @ rwkv_a2)
    g_out = jax.nn.sigmoid(gd) @ rwkv_g2
    kk = to_heads(rk * rwkv_k_k, RWKV_HEADS)
    kk = kk * lax.rsqrt(jnp.sum(kk * kk, axis=-1, keepdims=True) + 1e-12)
    k_mod = to_heads(rk * (1.0 + (a - 1.0) * rwkv_k_a), RWKV_HEADS)
    r = to_heads(rr, RWKV_HEADS)
    v = to_heads(rv, RWKV_HEADS)
    b_vec = kk * to_heads(a, RWKV_HEADS)

    def rw_in(dec, sl):
        return (r[:, sl], dec[:, sl], k_mod[:, sl], v[:, sl], -kk[:, sl], b_vec[:, sl])

    rstate0 = jnp.zeros((B, RWKV_HEADS, RWKV_N, RWKV_N), jnp.float32)
    ro_ctx, ro_lat = bidirectional_prefix_scan(rwkv7_scan, rw_in(dec_f, ctx_sl), rw_in(dec_b, ctx_sl),
                                               rw_in(dec_f, lat_sl), rw_in(dec_b, lat_sl), rstate0)

    def rwkv_out(o, sl):
        mu = jnp.mean(o, axis=-1, keepdims=True)
        var = jnp.mean(jnp.square(o - mu), axis=-1, keepdims=True)
        on = ((o - mu) * lax.rsqrt(var + RWKV_LN_EPS)).reshape(B, -1, RWKV_C) * rwkv_ln_w + rwkv_ln_b
        bonus = (jnp.sum(r[:, sl] * k_mod[:, sl] * rwkv_r_k, axis=-1, keepdims=True) * v[:, sl]).reshape(B, -1, RWKV_C)
        return (on + bonus) * g_out[:, sl]

    y = jnp.concatenate([gla_out(go_lat, lat_sl), rwkv_out(ro_lat, lat_sl)], axis=-1).astype(h.dtype) @ w_out
    if not need_ctx:
        return y, None
    yc = jnp.concatenate([gla_out(go_ctx, ctx_sl), rwkv_out(ro_ctx, ctx_sl)], axis=-1).astype(h.dtype) @ w_out
    return y, yc


def swiglu(h, w_gate, w_up, w_down):
    return (jax.nn.silu(h @ w_gate) * (h @ w_up)) @ w_down


def moe_swiglu(h, router, w_gate, w_up, w_down):
    logits = (h @ router).astype(jnp.float32)
    top_val, top_idx = lax.top_k(logits, TOP_K)
    weights = jax.nn.softmax(top_val, axis=-1)
    gates = jnp.sum(jax.nn.one_hot(top_idx, N_EXPERTS, dtype=jnp.float32) * weights[..., None], axis=-2)
    out = jnp.zeros(h.shape, jnp.float32)
    for e in range(N_EXPERTS):
        hidden = jax.nn.silu(h @ w_gate[e]) * (h @ w_up[e])
        out = out + gates[..., e:e + 1] * (hidden @ w_down[e])
    return out.astype(h.dtype)


def setup_inputs(seed: int = 0) -> dict:
    key = jax.random.key(seed)
    keys = jax.random.split(key, 80)
    counter = [0]

    def nk():
        k = keys[counter[0]]
        counter[0] += 1
        return k

    def nrm(shape, scale):
        return jax.random.normal(nk(), shape, jnp.float32) * scale

    def unif(shape, lo, hi):
        return jax.random.uniform(nk(), shape, jnp.float32, lo, hi)

    def gain(n):
        return 1.0 + nrm((n,), 0.05)

    def lin(fan_in, fan_out):
        return nrm((fan_in, fan_out), fan_in ** -0.5)

    D = D_MODEL
    return {
        'x': nrm((BATCH, SEQ, D), 1.0),
        'c': nrm((BATCH, D), 1.0),
        'ctx': nrm((BATCH, CTX_LEN, D), 1.0),
        'c_ctx': nrm((D,), 1.0),
        'hgrn_lb_logits': nrm((N_HGRN_LAYERS + 1, HGRN_K), 0.5),
        'l0_ada_w': nrm((D, 6 * D), 0.5 * D ** -0.5),
        'l0_ada_b': nrm((6 * D,), 0.02),
        'l0_norm_mix_pre': gain(D),
        'l0_norm_mix_post': gain(D),
        'l0_norm_ffn_pre': gain(D),
        'l0_norm_ffn_post': gain(D),
        'l0_w_in': lin(D, EVEN_IN),
        'l0_w_out': lin(EVEN_OUT, D),
        'l0_attn_sink': nrm((ATT_HEADS,), 1.0),
        'l0_hgrn_norm': gain(HGRN_DV),
        'l0_ffn_w_gate': lin(D, D_FF),
        'l0_ffn_w_up': lin(D, D_FF),
        'l0_ffn_w_down': lin(D_FF, D),
        'l1_ada_w': nrm((D, 6 * D), 0.5 * D ** -0.5),
        'l1_ada_b': nrm((6 * D,), 0.02),
        'l1_norm_mix_pre': gain(D),
        'l1_norm_mix_post': gain(D),
        'l1_norm_ffn_pre': gain(D),
        'l1_norm_ffn_post': gain(D),
        'l1_w_in': lin(D, ODD_IN),
        'l1_w_out': lin(ODD_OUT, D),
        'l1_gla_gate_up_f': lin(GLA_GATE_RANK, GLA_K),
        'l1_gla_gate_up_b': lin(GLA_GATE_RANK, GLA_K),
        'l1_gla_gate_bias_f': nrm((GLA_K,), 0.1),
        'l1_gla_gate_bias_b': nrm((GLA_K,), 0.1),
        'l1_gla_norm': gain(GLA_DV),
        'l1_rwkv_mu_prev': unif((RWKV_IN,), 0.0, 0.5),
        'l1_rwkv_mu_next': unif((RWKV_IN,), 0.0, 0.5),
        'l1_rwkv_w0_f': unif((RWKV_C,), -6.0, -1.0),
        'l1_rwkv_w0_b': unif((RWKV_C,), -6.0, -1.0),
        'l1_rwkv_w2_f': nrm((RWKV_DECAY_RANK, RWKV_C), 0.5 * RWKV_DECAY_RANK ** -0.5),
        'l1_rwkv_w2_b': nrm((RWKV_DECAY_RANK, RWKV_C), 0.5 * RWKV_DECAY_RANK ** -0.5),
        'l1_rwkv_a0': nrm((RWKV_C,), 0.5),
        'l1_rwkv_a2': lin(RWKV_A_RANK, RWKV_C),
        'l1_rwkv_g2': lin(RWKV_GATE_RANK, RWKV_C),
        'l1_rwkv_k_k': 0.85 + nrm((RWKV_C,), 0.05),
        'l1_rwkv_k_a': 1.0 + nrm((RWKV_C,), 0.05),
        'l1_rwkv_r_k': nrm((RWKV_HEADS, RWKV_N), 0.1),
        'l1_rwkv_ln_w': gain(RWKV_C),
        'l1_rwkv_ln_b': nrm((RWKV_C,), 0.02),
        'l1_moe_router': lin(D, N_EXPERTS),
        'l1_moe_w_gate': nrm((N_EXPERTS, D, D_FF_EXPERT), D ** -0.5),
        'l1_moe_w_up': nrm((N_EXPERTS, D, D_FF_EXPERT), D ** -0.5),
        'l1_moe_w_down': nrm((N_EXPERTS, D_FF_EXPERT, D), D_FF_EXPERT ** -0.5),
    }


def reference(x, c, ctx, c_ctx, hgrn_lb_logits,
              l0_ada_w, l0_ada_b, l0_norm_mix_pre, l0_norm_mix_post, l0_norm_ffn_pre, l0_norm_ffn_post,
              l0_w_in, l0_w_out, l0_attn_sink, l0_hgrn_norm, l0_ffn_w_gate, l0_ffn_w_up, l0_ffn_w_down,
              l1_ada_w, l1_ada_b, l1_norm_mix_pre, l1_norm_mix_post, l1_norm_ffn_pre, l1_norm_ffn_post,
              l1_w_in, l1_w_out, l1_gla_gate_up_f, l1_gla_gate_up_b, l1_gla_gate_bias_f, l1_gla_gate_bias_b,
              l1_gla_norm, l1_rwkv_mu_prev, l1_rwkv_mu_next, l1_rwkv_w0_f, l1_rwkv_w0_b, l1_rwkv_w2_f,
              l1_rwkv_w2_b, l1_rwkv_a0, l1_rwkv_a2, l1_rwkv_g2, l1_rwkv_k_k, l1_rwkv_k_a, l1_rwkv_r_k,
              l1_rwkv_ln_w, l1_rwkv_ln_b, l1_moe_router, l1_moe_w_gate, l1_moe_w_up, l1_moe_w_down):
    S = x.shape[1]
    L = ctx.shape[1]
    rows = S // GRID_W
    rope_cos, rope_sin = axial_rope_tables(rows)
    hgrn_lb = jnp.cumsum(jax.nn.softmax(hgrn_lb_logits.astype(jnp.float32), axis=0), axis=0)
    layers = [
        dict(ada_w=l0_ada_w, ada_b=l0_ada_b,
             norms=(l0_norm_mix_pre, l0_norm_mix_post, l0_norm_ffn_pre, l0_norm_ffn_post),
             mixer=dict(w_in=l0_w_in, w_out=l0_w_out, attn_sink=l0_attn_sink, hgrn_norm=l0_hgrn_norm),
             ffn=(l0_ffn_w_gate, l0_ffn_w_up, l0_ffn_w_down)),
        dict(ada_w=l1_ada_w, ada_b=l1_ada_b,
             norms=(l1_norm_mix_pre, l1_norm_mix_post, l1_norm_ffn_pre, l1_norm_ffn_post),
             mixer=dict(w_in=l1_w_in, w_out=l1_w_out, gla_gate_up_f=l1_gla_gate_up_f,
                        gla_gate_up_b=l1_gla_gate_up_b, gla_gate_bias_f=l1_gla_gate_bias_f,
                        gla_gate_bias_b=l1_gla_gate_bias_b, gla_norm=l1_gla_norm,
                        rwkv_mu_prev=l1_rwkv_mu_prev, rwkv_mu_next=l1_rwkv_mu_next,
                        rwkv_w0_f=l1_rwkv_w0_f, rwkv_w0_b=l1_rwkv_w0_b, rwkv_w2_f=l1_rwkv_w2_f,
                        rwkv_w2_b=l1_rwkv_w2_b, rwkv_a0=l1_rwkv_a0, rwkv_a2=l1_rwkv_a2, rwkv_g2=l1_rwkv_g2,
                        rwkv_k_k=l1_rwkv_k_k, rwkv_k_a=l1_rwkv_k_a, rwkv_r_k=l1_rwkv_r_k,
                        rwkv_ln_w=l1_rwkv_ln_w, rwkv_ln_b=l1_rwkv_ln_b),
             ffn=(l1_moe_router, l1_moe_w_gate, l1_moe_w_up, l1_moe_w_down)),
    ]
    xc = ctx
    for l in range(DEPTH):
        p = layers[l]
        need_ctx = l < DEPTH - 1
        n_mix_pre, n_mix_post, n_ffn_pre, n_ffn_post = p['norms']
        sh1, sc1, g1, sh2, sc2, g2 = [m[:, None, :] for m in adaln(c, p['ada_w'], p['ada_b'])]
        sh1c, sc1c, g1c, sh2c, sc2c, g2c = adaln(c_ctx, p['ada_w'], p['ada_b'])
        h = rmsnorm(x, n_mix_pre) * (1.0 + sc1) + sh1
        hc = rmsnorm(xc, n_mix_pre) * (1.0 + sc1c) + sh1c
        if l % 2 == 0:
            y, yc = even_mixer(h, hc, hgrn_lb=hgrn_lb[l // 2], rope_cos=rope_cos, rope_sin=rope_sin,
                               need_ctx=need_ctx, **p['mixer'])
            ffn = swiglu
        else:
            y, yc = odd_mixer(h, hc, need_ctx=need_ctx, **p['mixer'])
            ffn = moe_swiglu
        x = x + g1 * rmsnorm(y, n_mix_post)
        h = rmsnorm(x, n_ffn_pre) * (1.0 + sc2) + sh2
        if need_ctx:
            xc = xc + g1c * rmsnorm(yc, n_mix_post)
            hc = rmsnorm(xc, n_ffn_pre) * (1.0 + sc2c) + sh2c
            f = ffn(jnp.concatenate([hc, h], axis=1), *p['ffn'])
            xc = xc + g2c * rmsnorm(f[:, :L], n_ffn_post)
            x = x + g2 * rmsnorm(f[:, L:], n_ffn_post)
        else:
            x = x + g2 * rmsnorm(ffn(h, *p['ffn']), n_ffn_post)
    return x
```

```python
import functools

import jax
import jax.numpy as jnp
from jax import lax
from jax.experimental import pallas as pl
from jax.experimental.pallas import tpu as pltpu

F32 = jnp.float32
BF16 = jnp.bfloat16

NORM_EPS = 1e-6
CHUNK = 32
TOK_BLOCK = 256
ATT_BLOCK = 128
WINDOW = 128
HEAD_DIM = 128
ATT_HEADS = 8
ATT_KV_HEADS = 2
ROPE_THETA = 10000.0
GRID_W = 64
RWKV_N = 64
RWKV_HEADS = 16
RWKV_LN_EPS = 64e-5
GLA_GATE_NORMALIZER = 16.0
N_EXPERTS = 8
MOE_TILE = 1024
VMEM_LIMIT_BYTES = 56 * 1024 * 1024


def _params(*sem):
    return pltpu.CompilerParams(dimension_semantics=sem, vmem_limit_bytes=VMEM_LIMIT_BYTES)


def _mm_kernel(x_ref, w_ref, o_ref, wbf_ref):
    @pl.when(pl.program_id(1) == 0)
    def _():
        wbf_ref[...] = w_ref[...].astype(BF16)

    o_ref[...] = jnp.dot(x_ref[...].astype(BF16), wbf_ref[...],
                         preferred_element_type=F32).astype(o_ref.dtype)


def matmul(x, w, tm, tn, out_dtype=F32):
    M, K = x.shape
    N = w.shape[1]
    assert M % tm == 0 and N % tn == 0, (M, tm, N, tn)
    return pl.pallas_call(
        _mm_kernel,
        grid=(N // tn, M // tm),
        in_specs=[pl.BlockSpec((tm, K), lambda j, i: (i, 0)),
                  pl.BlockSpec((K, tn), lambda j, i: (0, j))],
        out_specs=pl.BlockSpec((tm, tn), lambda j, i: (i, j)),
        out_shape=jax.ShapeDtypeStruct((M, N), out_dtype),
        scratch_shapes=[pltpu.VMEM((K, tn), BF16)],
        compiler_params=_params("arbitrary", "arbitrary"),
        name="matmul",
    )(x, w)


def _rms(x, gain):
    ms = jnp.mean(x * x, axis=-1, keepdims=True)
    return x * lax.rsqrt(ms + NORM_EPS) * gain


def _normmod_kernel(x_ref, gain_ref, sc_ref, sh_ref, h_ref):
    y = _rms(x_ref[...], gain_ref[...])
    h_ref[...] = (y * (1.0 + sc_ref[0]) + sh_ref[0]).astype(h_ref.dtype)


def _mod_spec(D, slot, n_ctx_tiles):
    return pl.BlockSpec((1, 1, D), lambda i: (jnp.where(i < n_ctx_tiles, 6, 0) + slot, 0, 0))


def normmod(x, gain, mods, sh_slot, sc_slot, n_ctx_rows):
    R, D = x.shape
    tr = TOK_BLOCK
    assert R % tr == 0 and n_ctx_rows % tr == 0
    nct = n_ctx_rows // tr
    return pl.pallas_call(
        _normmod_kernel,
        grid=(R // tr,),
        in_specs=[pl.BlockSpec((tr, D), lambda i: (i, 0)),
                  pl.BlockSpec((1, D), lambda i: (0, 0)),
                  _mod_spec(D, sc_slot, nct), _mod_spec(D, sh_slot, nct)],
        out_specs=pl.BlockSpec((tr, D), lambda i: (i, 0)),
        out_shape=jax.ShapeDtypeStruct((R, D), BF16),
        compiler_params=_params("arbitrary"),
        name="normmod",
    )(x, gain.reshape(1, D), mods, mods)


def _resid_kernel(x_ref, y_ref, gpost_ref, g_ref, *rest, with_next):
    xn = x_ref[...] + g_ref[0] * _rms(y_ref[...], gpost_ref[...])
    if with_next:
        gpre_ref, sc_ref, sh_ref, xo_ref, h_ref = rest
        xo_ref[...] = xn
        h_ref[...] = (_rms(xn, gpre_ref[...]) * (1.0 + sc_ref[0]) + sh_ref[0]).astype(h_ref.dtype)
    else:
        (xo_ref,) = rest
        xo_ref[...] = xn


def resid_norm(x, y, gain_post, mods, g_slot, n_ctx_rows, nxt=None, h_dtype=BF16):
    R, D = x.shape
    tr = TOK_BLOCK
    assert R % tr == 0 and n_ctx_rows % tr == 0
    nct = n_ctx_rows // tr
    row = pl.BlockSpec((tr, D), lambda i: (i, 0))
    vec = pl.BlockSpec((1, D), lambda i: (0, 0))
    in_specs = [row, row, vec, _mod_spec(D, g_slot, nct)]
    args = [x, y, gain_post.reshape(1, D), mods]
    out_specs = [row]
    out_shape = [jax.ShapeDtypeStruct((R, D), F32)]
    if nxt is not None:
        gain_pre, mods_n, sh_slot, sc_slot = nxt
        in_specs += [vec, _mod_spec(D, sc_slot, nct), _mod_spec(D, sh_slot, nct)]
        args += [gain_pre.reshape(1, D), mods_n, mods_n]
        out_specs.append(row)
        out_shape.append(jax.ShapeDtypeStruct((R, D), h_dtype))
    out = pl.pallas_call(
        functools.partial(_resid_kernel, with_next=nxt is not None),
        grid=(R // tr,),
        in_specs=in_specs, out_specs=out_specs, out_shape=out_shape,
        compiler_params=_params("arbitrary"),
        name="resid_norm",
    )(*args)
    return out if nxt is not None else out[0]


def _ffn_kernel(te_ref, nu_ref, x_ref, wg_ref, wu_ref, wd_ref, *rest, scale_rows):
    if scale_rows:
        rw_ref, o_ref = rest
    else:
        (o_ref,) = rest
    i, j = pl.program_id(0), pl.program_id(1)

    @pl.when(i < nu_ref[0])
    def _():
        x = x_ref[...]
        g = jnp.dot(x, wg_ref[0].astype(BF16), preferred_element_type=F32)
        u = jnp.dot(x, wu_ref[0].astype(BF16), preferred_element_type=F32)
        hid = (g * jax.nn.sigmoid(g) * u).astype(BF16)
        part = jnp.dot(hid, wd_ref[0].astype(BF16), preferred_element_type=F32)
        if scale_rows:
            part = part * rw_ref[...]

        @pl.when(j == 0)
        def _():
            o_ref[...] = part

        @pl.when(j > 0)
        def _():
            o_ref[...] += part

    @pl.when(jnp.logical_and(i >= nu_ref[0], j == 0))
    def _():
        o_ref[...] = jnp.zeros_like(o_ref)


def swiglu_ffn(x, w_gate, w_up, w_down, tile_expert, n_used, tm, tf, row_weight=None):
    R, D = x.shape
    E, _, F = w_gate.shape
    assert R % tm == 0 and F % tf == 0
    nf = F // tf

    def fblk(i, j, te, nu):
        return jnp.where(i < nu[0], j, nf - 1)

    in_specs = [pl.BlockSpec((tm, D), lambda i, j, te, nu: (i, 0)),
                pl.BlockSpec((1, D, tf), lambda i, j, te, nu: (te[i], 0, fblk(i, j, te, nu))),
                pl.BlockSpec((1, D, tf), lambda i, j, te, nu: (te[i], 0, fblk(i, j, te, nu))),
                pl.BlockSpec((1, tf, D), lambda i, j, te, nu: (te[i], fblk(i, j, te, nu), 0))]
    args = [x, w_gate, w_up, w_down]
    if row_weight is not None:
        in_specs.append(pl.BlockSpec((tm, 1), lambda i, j, te, nu: (i, 0)))
        args.append(row_weight)
    return pl.pallas_call(
        functools.partial(_ffn_kernel, scale_rows=row_weight is not None),
        grid_spec=pltpu.PrefetchScalarGridSpec(
            num_scalar_prefetch=2,
            grid=(R // tm, nf),
            in_specs=in_specs,
            out_specs=pl.BlockSpec((tm, D), lambda i, j, te, nu: (i, 0)),
        ),
        out_shape=jax.ShapeDtypeStruct((R, D), F32),
        compiler_params=_params("arbitrary", "arbitrary"),
        name="swiglu_ffn",
    )(tile_expert, n_used, *args)


def _attn_kernel(sink_ref, q_ref, *rest, has_window, seq_len):
    if has_window:
        kp_ref, ko_ref, kn_ref, vp_ref, vo_ref, vn_ref, kc_ref, vc_ref, o_ref = rest
    else:
        kc_ref, vc_ref, o_ref = rest
    i = pl.program_id(0)
    BQ = q_ref.shape[0]
    L = kc_ref.shape[0]
    G = ATT_KV_HEADS
    R = ATT_HEADS // G
    n_win = 3 * BQ if has_window else 0
    nk = n_win + L
    if has_window:
        row = lax.broadcasted_iota(jnp.int32, (R * BQ, nk), 0)
        col = lax.broadcasted_iota(jnp.int32, (R * BQ, nk), 1)
        qi = jnp.bitwise_and(row, BQ - 1)
        kpos = i * BQ + col - BQ
        in_band = jnp.abs(col - BQ - qi) <= WINDOW
        in_seq = jnp.logical_and(kpos >= 0, kpos < seq_len)
        valid = jnp.logical_or(col >= n_win, jnp.logical_and(in_band, in_seq))
    for g in range(G):
        cs = slice(g * HEAD_DIM, (g + 1) * HEAD_DIM)
        if has_window:
            k_all = jnp.concatenate([kp_ref[:, cs], ko_ref[:, cs], kn_ref[:, cs], kc_ref[:, cs]], axis=0)
            v_all = jnp.concatenate([vp_ref[:, cs], vo_ref[:, cs], vn_ref[:, cs], vc_ref[:, cs]], axis=0)
        else:
            k_all, v_all = kc_ref[:, cs], vc_ref[:, cs]
        q_g = jnp.concatenate(
            [q_ref[:, (g * R + r) * HEAD_DIM:(g * R + r + 1) * HEAD_DIM] for r in range(R)], axis=0)
        q_g = (q_g * (HEAD_DIM ** -0.5)).astype(BF16)
        s = lax.dot_general(q_g, k_all.astype(BF16), (((1,), (1,)), ((), ())),
                            preferred_element_type=F32)
        if has_window:
            s = jnp.where(valid, s, -jnp.inf)
        sink = jnp.concatenate([jnp.full((BQ, 1), sink_ref[g * R + r], F32) for r in range(R)], axis=0)
        m = jnp.maximum(sink, jnp.max(s, axis=-1, keepdims=True))
        p = jnp.exp(s - m)
        denom = jnp.exp(sink - m) + jnp.sum(p, axis=-1, keepdims=True)
        o = jnp.dot(p.astype(BF16), v_all.astype(BF16), preferred_element_type=F32) / denom
        for r in range(R):
            h = g * R + r
            o_ref[:, h * HEAD_DIM:(h + 1) * HEAD_DIM] = o[r * BQ:(r + 1) * BQ].astype(o_ref.dtype)


def sink_attention(q, k, v, kc, vc, sink, has_window):
    S, QW = q.shape
    L, KW = kc.shape
    BQ = ATT_BLOCK if has_window else S
    nb = S // BQ
    assert S % BQ == 0
    qspec = pl.BlockSpec((BQ, QW), lambda i: (i, 0))
    cspec = pl.BlockSpec((L, KW), lambda i: (0, 0))
    in_specs = [pl.BlockSpec(memory_space=pltpu.SMEM), qspec]
    args = [sink, q]
    if has_window:
        prev = pl.BlockSpec((BQ, KW), lambda i: (jnp.maximum(i - 1, 0), 0))
        own = pl.BlockSpec((BQ, KW), lambda i: (i, 0))
        nxt = pl.BlockSpec((BQ, KW), lambda i: (jnp.minimum(i + 1, nb - 1), 0))
        in_specs += [prev, own, nxt, prev, own, nxt]
        args += [k, k, k, v, v, v]
    in_specs += [cspec, cspec]
    args += [kc, vc]
    return pl.pallas_call(
        functools.partial(_attn_kernel, has_window=has_window, seq_len=S),
        grid=(nb,),
        in_specs=in_specs,
        out_specs=qspec,
        out_shape=jax.ShapeDtypeStruct((S, QW), BF16),
        compiler_params=_params("arbitrary"),
        name="sink_attention",
    )(*args)


def _rope_tables(n_tokens):
    n_freq = HEAD_DIM // 4
    t = jnp.arange(n_tokens)
    row = (t // GRID_W).astype(F32)
    col = (t % GRID_W).astype(F32)
    inv = ROPE_THETA ** (-jnp.arange(n_freq, dtype=F32) / n_freq)
    ang = jnp.stack([row[:, None] * inv, col[:, None] * inv], axis=1)
    return jnp.cos(ang), jnp.sin(ang)


def _apply_rope(x, n_heads, cos, sin):
    S = x.shape[0]
    n_freq = HEAD_DIM // 4
    xr = x.reshape(S, n_heads, 2, 2, n_freq)
    x1, x2 = xr[..., 0, :], xr[..., 1, :]
    cb, sb = cos[:, None], sin[:, None]
    out = jnp.stack([x1 * cb - x2 * sb, x1 * sb + x2 * cb], axis=-2)
    return out.reshape(S, n_heads * HEAD_DIM)


CHUNK_SHIFT = CHUNK.bit_length() - 1
assert 1 << CHUNK_SHIFT == CHUNK


def _chunk_masks(tb, reverse, strict):
    r = lax.broadcasted_iota(jnp.int32, (tb, tb), 0)
    c = lax.broadcasted_iota(jnp.int32, (tb, tb), 1)
    same = jnp.right_shift(r, CHUNK_SHIFT) == jnp.right_shift(c, CHUNK_SHIFT)
    if reverse:
        tri = (c > r) if strict else (c >= r)
    else:
        tri = (c < r) if strict else (c <= r)
    return same, jnp.logical_and(same, tri)


def _dot_hi(a, b):
    return jnp.dot(a, b, preferred_element_type=F32, precision=lax.Precision.HIGHEST)


def _dot_tn(a, b):
    return lax.dot_general(a, b, (((0,), (0,)), ((), ())), preferred_element_type=F32)


def _dot_nt(a, b):
    return lax.dot_general(a, b, (((1,), (1,)), ((), ())), preferred_element_type=F32)


def _chunkrec_kernel(q_ref, k_ref, v_ref, ld_ref, o_ref, st_ref, *, reverse):
    @pl.when(pl.program_id(1) == 0)
    def _():
        st_ref[...] = jnp.zeros_like(st_ref)

    tb = q_ref.shape[0]
    q, k, v, ld = q_ref[...], k_ref[...], v_ref[...], ld_ref[...]
    same, tri = _chunk_masks(tb, reverse, strict=False)
    g_cum = _dot_hi(tri.astype(F32), ld)
    g_tot = _dot_hi(same.astype(F32), ld)
    q_dec = q * jnp.exp(g_cum)
    k_inv = k * jnp.exp(-g_cum)
    k_tail = k * jnp.exp(g_tot - g_cum)
    a = jnp.where(tri, _dot_nt(q_dec.astype(BF16), k_inv.astype(BF16)), 0.0)
    o_intra = jnp.dot(a.astype(BF16), v.astype(BF16), preferred_element_type=F32)
    n_chunks = tb // CHUNK
    order = range(n_chunks - 1, -1, -1) if reverse else range(n_chunks)
    st = st_ref[...]
    for c in order:
        sl = slice(c * CHUNK, (c + 1) * CHUNK)
        o_ref[sl, :] = (o_intra[sl] + _dot_nt(q_dec[sl].astype(BF16), st.astype(BF16))).astype(o_ref.dtype)
        d = jnp.exp(g_tot[c * CHUNK:c * CHUNK + 1, :])
        st = st * d + _dot_tn(v[sl], k_tail[sl])
    st_ref[...] = st


def _seq_block_index(nblk, nctx, reverse):
    if not reverse:
        return lambda t: t
    return lambda t: jnp.where(t < nctx, nctx - 1 - t, nblk - 1 - (t - nctx))


def chunk_recurrence(q, k, v, ld, n_heads, n_ctx_rows, reverse):
    T = q.shape[0]
    K = q.shape[1] // n_heads
    V = v.shape[1] // n_heads
    tb = TOK_BLOCK
    assert T % tb == 0 and n_ctx_rows % tb == 0
    nblk, nctx = T // tb, n_ctx_rows // tb
    blk = _seq_block_index(nblk, nctx, reverse)
    kspec = pl.BlockSpec((tb, K), lambda h, t: (blk(t), h))
    vspec = pl.BlockSpec((tb, V), lambda h, t: (blk(t), h))
    return pl.pallas_call(
        functools.partial(_chunkrec_kernel, reverse=reverse),
        grid=(n_heads, nblk),
        in_specs=[kspec, kspec, vspec, kspec],
        out_specs=vspec,
        out_shape=jax.ShapeDtypeStruct((T, n_heads * V), F32),
        scratch_shapes=[pltpu.VMEM((V, K), F32)],
        compiler_params=_params("arbitrary", "arbitrary"),
        name="chunk_recurrence",
    )(q, k, v, ld)


def _mm_bf(a, b):
    return jnp.dot(a.astype(BF16), b.astype(BF16), preferred_element_type=F32)


def _rwkv_pre_kernel(r_ref, k_ref, v_ref, a_ref, b_ref, lw_ref,
                     qp_ref, ol_ref, plt_ref, zt_ref, gt_ref, *, reverse):
    N = RWKV_N
    tb = r_ref.shape[0]
    r, k, v, a, b, lw = (ref[...] for ref in (r_ref, k_ref, v_ref, a_ref, b_ref, lw_ref))
    same, tri_incl = _chunk_masks(tb, reverse, strict=False)
    _, tri_strict = _chunk_masks(tb, reverse, strict=True)
    g_cum = _dot_hi(tri_incl.astype(F32), lw)
    g_tot = _dot_hi(same.astype(F32), lw)
    e_neg = jnp.exp(-g_cum)
    e_tail = jnp.exp(g_tot - g_cum)
    a_t = (a * jnp.exp(g_cum - lw)).astype(BF16)
    r_t = (r * jnp.exp(g_cum)).astype(BF16)
    b_t = (b * e_neg).astype(BF16)
    k_t = (k * e_neg).astype(BF16)
    b_h = b * e_tail
    k_h = k * e_tail
    a_ab = jnp.where(tri_strict, _dot_nt(a_t, b_t), 0.0)
    a_ak = jnp.where(tri_strict, _dot_nt(a_t, k_t), 0.0)
    a_rb = jnp.where(tri_incl, _dot_nt(r_t, b_t), 0.0)
    a_rk = jnp.where(tri_incl, _dot_nt(r_t, k_t), 0.0)
    x = jnp.concatenate([a_t.astype(F32), _mm_bf(a_ak, v)], axis=1)
    lp = a_ab
    for j in range(CHUNK_SHIFT):
        x = x + _mm_bf(lp, x)
        if j < CHUNK_SHIFT - 1:
            lp = _mm_bf(lp, lp)
    qo = jnp.concatenate([r_t.astype(F32), _mm_bf(a_rk, v)], axis=1) + _mm_bf(a_rb, x)
    qp_ref[...] = qo[:, :N]
    ol_ref[...] = qo[:, N:]
    gt_ref[...] = g_tot
    w, uloc = x[:, :N], x[:, N:]
    for c in range(tb // CHUNK):
        sl = slice(c * CHUNK, (c + 1) * CHUNK)
        plt_ref[c] = _dot_tn(w[sl], b_h[sl])
        zt_ref[c] = _dot_tn(uloc[sl], b_h[sl]) + _dot_tn(v[sl], k_h[sl])


def _rwkv_scan_kernel(qp_ref, ol_ref, plt_ref, zt_ref, gt_ref, o_ref, st_ref, *, reverse):
    @pl.when(pl.program_id(0) == 0)
    def _():
        st_ref[...] = jnp.zeros_like(st_ref)

    n_heads, tb, _ = qp_ref.shape
    n_chunks = tb // CHUNK
    order = range(n_chunks - 1, -1, -1) if reverse else range(n_chunks)
    for h in range(n_heads):
        st = st_ref[h]
        for c in order:
            sl = slice(c * CHUNK, (c + 1) * CHUNK)
            o_ref[h, sl, :] = _dot_nt(qp_ref[h, sl, :].astype(BF16), st.astype(BF16)) + ol_ref[h, sl, :]
            d = jnp.exp(gt_ref[h, c * CHUNK:c * CHUNK + 1, :])
            st = st * d + _mm_bf(st, plt_ref[h, c]) + zt_ref[h, c]
        st_ref[h] = st


def rwkv7_mix(r, k, v, a, b, lw, n_ctx_rows, reverse):
    H, T, N = r.shape
    tb = TOK_BLOCK
    assert T % tb == 0 and n_ctx_rows % tb == 0
    nblk, nctx = T // tb, n_ctx_rows // tb
    ncb = tb // CHUNK
    tok = pl.BlockSpec((None, tb, N), lambda h, t: (h, t, 0))
    mat = pl.BlockSpec((None, ncb, N, N), lambda h, t: (h, t, 0, 0))
    tok_shape = jax.ShapeDtypeStruct((H, T, N), F32)
    mat_shape = jax.ShapeDtypeStruct((H, T // CHUNK, N, N), F32)
    qp, ol, plt, zt, gt = pl.pallas_call(
        functools.partial(_rwkv_pre_kernel, reverse=reverse),
        grid=(H, nblk),
        in_specs=[tok] * 6,
        out_specs=[tok, tok, mat, mat, tok],
        out_shape=[tok_shape, tok_shape, mat_shape, mat_shape, tok_shape],
        compiler_params=_params("arbitrary", "arbitrary"),
        name="rwkv_pre",
    )(r, k, v, a, b, lw)
    blk = _seq_block_index(nblk, nctx, reverse)
    tok_all = pl.BlockSpec((H, tb, N), lambda t: (0, blk(t), 0))
    mat_all = pl.BlockSpec((H, ncb, N, N), lambda t: (0, blk(t), 0, 0))
    return pl.pallas_call(
        functools.partial(_rwkv_scan_kernel, reverse=reverse),
        grid=(nblk,),
        in_specs=[tok_all, tok_all, mat_all, mat_all, tok_all],
        out_specs=tok_all,
        out_shape=tok_shape,
        scratch_shapes=[pltpu.VMEM((H, N, N), F32)],
        compiler_params=_params("arbitrary"),
        name="rwkv_scan",
    )(qp, ol, plt, zt, gt)


def _row_copy(src_hbm, src_row, dst_ref, dst_row, sem):
    return pltpu.make_async_copy(src_hbm.at[pl.ds(src_row, 1)], dst_ref.at[pl.ds(dst_row, 1)], sem)


def _gather_kernel(src_ref, x_hbm, o_ref, buf, sem):
    n = o_ref.shape[0]

    def issue(r, carry):
        _row_copy(x_hbm, src_ref[0, 0, r], buf, r, sem).start()
        return carry

    def wait(r, carry):
        _row_copy(x_hbm, 0, buf, r, sem).wait()
        return carry

    lax.fori_loop(0, n, issue, 0)
    lax.fori_loop(0, n, wait, 0)
    o_ref[...] = buf[...].astype(o_ref.dtype)


def gather_rows(x, src, tg):
    S, D = x.shape
    P = src.shape[0]
    assert P % tg == 0 and x.dtype == F32
    return pl.pallas_call(
        _gather_kernel,
        grid=(P // tg,),
        in_specs=[pl.BlockSpec((1, 1, tg), lambda i: (i, 0, 0), memory_space=pltpu.SMEM),
                  pl.BlockSpec(memory_space=pl.ANY)],
        out_specs=pl.BlockSpec((tg, D), lambda i: (i, 0)),
        out_shape=jax.ShapeDtypeStruct((P, D), BF16),
        scratch_shapes=[pltpu.VMEM((tg, D), F32), pltpu.SemaphoreType.DMA(())],
        compiler_params=_params("arbitrary"),
        name="gather_rows",
    )(src.reshape(P // tg, 1, tg), x)


def _combine_kernel(pos_ref, x_ref, ys_hbm, gpost_ref, g_ref, o_ref, buf0, buf1, sems):
    n = x_ref.shape[0]

    def issue(r, carry):
        _row_copy(ys_hbm, pos_ref[0, 0, 2 * r], buf0, r, sems.at[0]).start()
        _row_copy(ys_hbm, pos_ref[0, 0, 2 * r + 1], buf1, r, sems.at[1]).start()
        return carry

    def wait(r, carry):
        _row_copy(ys_hbm, 0, buf0, r, sems.at[0]).wait()
        _row_copy(ys_hbm, 0, buf1, r, sems.at[1]).wait()
        return carry

    lax.fori_loop(0, n, issue, 0)
    lax.fori_loop(0, n, wait, 0)
    f = buf0[...] + buf1[...]
    o_ref[...] = x_ref[...] + g_ref[0] * _rms(f, gpost_ref[...])


def combine_resid(x, ys, pos, gain_post, mods, g_slot):
    S, D = x.shape
    tc = TOK_BLOCK
    assert S % tc == 0
    row = pl.BlockSpec((tc, D), lambda i: (i, 0))
    return pl.pallas_call(
        _combine_kernel,
        grid=(S // tc,),
        in_specs=[pl.BlockSpec((1, 1, 2 * tc), lambda i: (i, 0, 0), memory_space=pltpu.SMEM),
                  row, pl.BlockSpec(memory_space=pl.ANY),
                  pl.BlockSpec((1, D), lambda i: (0, 0)), _mod_spec(D, g_slot, 0)],
        out_specs=row,
        out_shape=jax.ShapeDtypeStruct((S, D), F32),
        scratch_shapes=[pltpu.VMEM((tc, D), F32), pltpu.VMEM((tc, D), F32),
                        pltpu.SemaphoreType.DMA((2,))],
        compiler_params=_params("arbitrary"),
        name="combine_resid",
    )(pos.reshape(S // tc, 1, 2 * tc), x, ys, gain_post.reshape(1, D), mods)


def _route(logits, tm):
    S = logits.shape[0]
    top_val, top_idx = lax.top_k(logits, 2)
    weights = jax.nn.softmax(top_val, axis=-1)
    e_flat = top_idx.reshape(-1)
    onehot = (e_flat[:, None] == jnp.arange(N_EXPERTS)[None, :]).astype(jnp.int32)
    rank = jnp.take_along_axis(jnp.cumsum(onehot, axis=0), e_flat[:, None], axis=1)[:, 0] - 1
    counts = jnp.sum(onehot, axis=0)
    tiles_per = (counts + tm - 1) // tm
    tile_end = jnp.cumsum(tiles_per)
    start = (tile_end - tiles_per) * tm
    dest = start[e_flat] + rank
    n_rows = 2 * S + N_EXPERTS * tm
    n_tiles = n_rows // tm
    token = jnp.arange(2 * S, dtype=jnp.int32) // 2
    src = jnp.zeros((n_rows,), jnp.int32).at[dest].set(token)
    row_w = jnp.zeros((n_rows,), F32).at[dest].set(weights.reshape(-1))
    n_used = tile_end[-1].astype(jnp.int32)
    tile_id = jnp.minimum(jnp.arange(n_tiles, dtype=jnp.int32), n_used - 1)
    tile_expert = jnp.sum((tile_end[None, :] <= tile_id[:, None]).astype(jnp.int32), axis=1)
    tile_expert = jnp.minimum(tile_expert, N_EXPERTS - 1)
    return src, row_w.reshape(n_rows, 1), dest.astype(jnp.int32), tile_expert, n_used.reshape(1)


def _adaln(c, c_ctx, w, b):
    D = c.shape[-1]
    rows = jnp.zeros((8, D), F32).at[0].set(c[0]).at[1].set(c_ctx)
    m = matmul(jax.nn.silu(rows), w, 8, 2048)[:2] + b[None, :]
    return m.reshape(12, 1, D)


def _head_rms(o, gain, n_heads):
    T = o.shape[0]
    oh = o.reshape(T, n_heads, -1)
    y = oh * lax.rsqrt(jnp.mean(oh * oh, axis=-1, keepdims=True) + NORM_EPS) * gain
    return y.reshape(T, -1)


def _even_mixer(h, L, w_in, w_out, attn_sink, hgrn_norm, hgrn_lb):
    T = h.shape[0]
    S = T - L
    proj = matmul(h, w_in, 1280, 512)
    qa, ka, va = proj[:, :1024], proj[:, 1024:1280], proj[:, 1280:1536]
    qh, ih = proj[:, 1536:2560], proj[:, 2560:3584]
    f_fw, f_bw, gh = proj[:, 3584:4608], proj[:, 4608:5632], proj[:, 5632:6656]
    cos, sin = _rope_tables(S)
    att = sink_attention(_apply_rope(qa[L:], ATT_HEADS, cos, sin), _apply_rope(ka[L:], ATT_KV_HEADS, cos, sin),
                         va[L:], ka[:L], va[:L], attn_sink, True)
    att_c = sink_attention(qa[:L], None, None, ka[:L], va[:L], attn_sink, False)
    q = jax.nn.silu(qh)
    o = None
    for f_raw, rev in ((f_fw, False), (f_bw, True)):
        f = hgrn_lb + (1.0 - hgrn_lb) * jax.nn.sigmoid(f_raw)
        od = chunk_recurrence(q, 1.0 - f, ih, jnp.log(f), 8, L, rev)
        o = od if o is None else o + od
    hg = (_head_rms(o, hgrn_norm, 8) * jax.nn.silu(gh)).astype(BF16)
    ycat = jnp.concatenate([jnp.concatenate([att_c, att], axis=0), hg], axis=-1)
    return matmul(ycat, w_out, 1280, 1024)


def _token_shift(p, L, mu_prev, mu_next):
    def one(s):
        zero = jnp.zeros_like(s[:1])
        prev = jnp.concatenate([zero, s[:-1]], axis=0)
        nxt = jnp.concatenate([s[1:], zero], axis=0)
        return s + mu_prev * (prev - s) + mu_next * (nxt - s)
    return jnp.concatenate([one(p[:L]), one(p[L:])], axis=0)


def _odd_mixer(h, L, w_in, w_out, gla_gate_up_f, gla_gate_up_b, gla_gate_bias_f, gla_gate_bias_b, gla_norm,
               mu_prev, mu_next, w0_f, w0_b, w2_f, w2_b, a0, a2, g2, k_k, k_a, r_k, ln_w, ln_b):
    T = h.shape[0]
    GO = 3104
    cols = lambda a, b: w_in[:, a:b]
    w_big = jnp.concatenate([cols(0, 2048), cols(2080, 3104), cols(GO, GO + 3072), cols(GO + 3360, GO + 3616)], axis=1)
    w_small = jnp.concatenate([cols(2048, 2080), cols(GO + 3072, GO + 3360),
                               jnp.zeros((w_in.shape[0], 64), F32)], axis=1)
    pb = matmul(h, w_big, 1280, 640)
    ps = matmul(h, w_small, 1280, 384)
    gq, gk, gv, gr = pb[:, :512], pb[:, 512:1024], pb[:, 1024:2048], pb[:, 2048:3072]
    gd_f, gd_b = ps[:, :16], ps[:, 16:32]
    o = None
    for gd, up, bias, rev in ((gd_f, gla_gate_up_f, gla_gate_bias_f, False), (gd_b, gla_gate_up_b, gla_gate_bias_b, True)):
        lg = jax.nn.log_sigmoid(matmul(gd, up, 1280, 512) + bias) / GLA_GATE_NORMALIZER
        od = chunk_recurrence(gq * (128 ** -0.5), gk, gv, lg, 4, L, rev)
        o = od if o is None else o + od
    gla = _head_rms(o, gla_norm, 4) * jax.nn.silu(gr)
    def shift(p, a, b):
        return _token_shift(p, L, mu_prev[a:b], mu_next[a:b])
    rrkv = shift(pb[:, 3072:6144], 0, 3072)
    rr, rk, rv = rrkv[:, :1024], rrkv[:, 1024:2048], rrkv[:, 2048:]
    low = shift(ps[:, 32:320], 3072, 3360)
    wd_f, wd_b, ad = low[:, :96], low[:, 96:192], low[:, 192:288]
    gd = shift(pb[:, 6144:6400], 3360, 3616)
    a_sig = jax.nn.sigmoid(a0 + matmul(ad, a2, 1280, 1024))
    g_out = matmul(jax.nn.sigmoid(gd), g2, 1280, 1024)
    heads = lambda t: t.reshape(T, RWKV_HEADS, RWKV_N)
    kk = heads(rk * k_k)
    kk = kk * lax.rsqrt(jnp.sum(kk * kk, axis=-1, keepdims=True) + 1e-12)
    k_mod = heads(rk * (1.0 + (a_sig - 1.0) * k_a))
    r, v = heads(rr), heads(rv)
    b_vec = kk * heads(a_sig)
    hm = lambda t: jnp.moveaxis(t, 1, 0)
    ro = None
    for w0, wd, w2, rev in ((w0_f, wd_f, w2_f, False), (w0_b, wd_b, w2_b, True)):
        log_w = -jnp.exp(-jax.nn.softplus(-(w0 + matmul(jnp.tanh(wd), w2, 1280, 1024))) - 0.5)
        od = rwkv7_mix(hm(r), hm(k_mod), hm(v), hm(-kk), hm(b_vec), hm(heads(log_w)), L, rev)
        ro = od if ro is None else ro + od
    ro = jnp.moveaxis(ro, 0, 1)[L:]
    mu = jnp.mean(ro, axis=-1, keepdims=True)
    var = jnp.mean(jnp.square(ro - mu), axis=-1, keepdims=True)
    on = ((ro - mu) * lax.rsqrt(var + RWKV_LN_EPS)).reshape(T - L, -1) * ln_w + ln_b
    bonus = (jnp.sum(r[L:] * k_mod[L:] * r_k, axis=-1, keepdims=True) * v[L:]).reshape(T - L, -1)
    rw = (on + bonus) * g_out[L:]
    ycat = jnp.concatenate([gla[L:], rw], axis=-1).astype(BF16)
    return matmul(ycat, w_out, 1024, 1024)


def kernel(x, c, ctx, c_ctx, hgrn_lb_logits, l0_ada_w, l0_ada_b, l0_norm_mix_pre, l0_norm_mix_post, l0_norm_ffn_pre, l0_norm_ffn_post, l0_w_in, l0_w_out, l0_attn_sink, l0_hgrn_norm, l0_ffn_w_gate, l0_ffn_w_up, l0_ffn_w_down, l1_ada_w, l1_ada_b, l1_norm_mix_pre, l1_norm_mix_post, l1_norm_ffn_pre, l1_norm_ffn_post, l1_w_in, l1_w_out, l1_gla_gate_up_f, l1_gla_gate_up_b, l1_gla_gate_bias_f, l1_gla_gate_bias_b, l1_gla_norm, l1_rwkv_mu_prev, l1_rwkv_mu_next, l1_rwkv_w0_f, l1_rwkv_w0_b, l1_rwkv_w2_f, l1_rwkv_w2_b, l1_rwkv_a0, l1_rwkv_a2, l1_rwkv_g2, l1_rwkv_k_k, l1_rwkv_k_a, l1_rwkv_r_k, l1_rwkv_ln_w, l1_rwkv_ln_b, l1_moe_router, l1_moe_w_gate, l1_moe_w_up, l1_moe_w_down):
    B, S, D = x.shape
    L = ctx.shape[1]
    assert B == 1
    T = L + S
    SH1, SC1, G1, SH2, SC2, G2 = range(6)
    xa = jnp.concatenate([ctx[0], x[0]], axis=0)
    hgrn_lb = jnp.cumsum(jax.nn.softmax(hgrn_lb_logits.astype(F32), axis=0), axis=0)
    m0 = _adaln(c, c_ctx, l0_ada_w, l0_ada_b)
    m1 = _adaln(c, c_ctx, l1_ada_w, l1_ada_b)

    h = normmod(xa, l0_norm_mix_pre, m0, SH1, SC1, L)
    y = _even_mixer(h, L, l0_w_in, l0_w_out, l0_attn_sink, l0_hgrn_norm, hgrn_lb[0])
    xa, h = resid_norm(xa, y, l0_norm_mix_post, m0, G1, L, nxt=(l0_norm_ffn_pre, m0, SH2, SC2))
    n_t = T // 1280
    f = swiglu_ffn(h, l0_ffn_w_gate[None], l0_ffn_w_up[None], l0_ffn_w_down[None],
                   jnp.zeros((n_t,), jnp.int32), jnp.full((1,), n_t, jnp.int32), 1280, 256)
    xa, h = resid_norm(xa, f, l0_norm_ffn_post, m0, G2, L, nxt=(l1_norm_mix_pre, m1, SH1, SC1))

    y = _odd_mixer(h, L, l1_w_in, l1_w_out, l1_gla_gate_up_f, l1_gla_gate_up_b, l1_gla_gate_bias_f,
                   l1_gla_gate_bias_b, l1_gla_norm, l1_rwkv_mu_prev, l1_rwkv_mu_next, l1_rwkv_w0_f, l1_rwkv_w0_b,
                   l1_rwkv_w2_f, l1_rwkv_w2_b, l1_rwkv_a0, l1_rwkv_a2, l1_rwkv_g2, l1_rwkv_k_k, l1_rwkv_k_a,
                   l1_rwkv_r_k, l1_rwkv_ln_w, l1_rwkv_ln_b)
    xl, h = resid_norm(xa[L:], y, l1_norm_mix_post, m1, G1, 0, nxt=(l1_norm_ffn_pre, m1, SH2, SC2), h_dtype=F32)
    router = jnp.concatenate([l1_moe_router, jnp.zeros((D, 128 - N_EXPERTS), F32)], axis=1)
    logits = matmul(h, router, 1024, 128)[:, :N_EXPERTS]
    src, row_w, dest, tile_expert, n_used = _route(logits, MOE_TILE)
    hs = gather_rows(h, src, TOK_BLOCK)
    ys = swiglu_ffn(hs, l1_moe_w_gate, l1_moe_w_up, l1_moe_w_down, tile_expert, n_used, MOE_TILE, 256,
                    row_weight=row_w)
    out = combine_resid(xl, ys, dest, l1_norm_ffn_post, m1, G2)
    return out[None]
```

```python
import functools

import jax
import jax.numpy as jnp
from jax import lax
from jax.experimental import pallas as pl
from jax.experimental.pallas import tpu as pltpu

F32 = jnp.float32
BF16 = jnp.bfloat16

NORM_EPS = 1e-6
CHUNK = 32
TOK_BLOCK = 256
ATT_BLOCK = 128
WINDOW = 128
HEAD_DIM = 128
ATT_HEADS = 8
ATT_KV_HEADS = 2
ROPE_THETA = 10000.0
GRID_W = 64
RWKV_N = 64
RWKV_HEADS = 16
RWKV_LN_EPS = 64e-5
RWKV_HEAD_BLOCK = 4
GLA_GATE_NORMALIZER = 16.0
N_EXPERTS = 8
MOE_TILE = 1024
VMEM_LIMIT_BYTES = 56 * 1024 * 1024


def _params(*sem):
    return pltpu.CompilerParams(dimension_semantics=sem, vmem_limit_bytes=VMEM_LIMIT_BYTES)


def _mm_kernel(x_ref, w_ref, o_ref, wbf_ref):
    @pl.when(pl.program_id(1) == 0)
    def _():
        wbf_ref[...] = w_ref[...].astype(BF16)

    o_ref[...] = jnp.dot(x_ref[...].astype(BF16), wbf_ref[...],
                         preferred_element_type=F32).astype(o_ref.dtype)


def matmul(x, w, tm, tn, out_dtype=F32):
    M, K = x.shape
    N = w.shape[1]
    assert M % tm == 0 and N % tn == 0, (M, tm, N, tn)
    return pl.pallas_call(
        _mm_kernel,
        grid=(N // tn, M // tm),
        in_specs=[pl.BlockSpec((tm, K), lambda j, i: (i, 0)),
                  pl.BlockSpec((K, tn), lambda j, i: (0, j))],
        out_specs=pl.BlockSpec((tm, tn), lambda j, i: (i, j)),
        out_shape=jax.ShapeDtypeStruct((M, N), out_dtype),
        scratch_shapes=[pltpu.VMEM((K, tn), BF16)],
        compiler_params=_params("arbitrary", "arbitrary"),
        name="matmul",
    )(x, w)


def _rms(x, gain):
    ms = jnp.mean(x * x, axis=-1, keepdims=True)
    return x * lax.rsqrt(ms + NORM_EPS) * gain


def _normmod_kernel(x_ref, gain_ref, sc_ref, sh_ref, h_ref):
    y = _rms(x_ref[...], gain_ref[...])
    h_ref[...] = (y * (1.0 + sc_ref[0]) + sh_ref[0]).astype(h_ref.dtype)


def _mod_spec(D, slot, n_ctx_tiles):
    return pl.BlockSpec((1, 1, D), lambda i: (jnp.where(i < n_ctx_tiles, 6, 0) + slot, 0, 0))


def normmod(x, gain, mods, sh_slot, sc_slot, n_ctx_rows):
    R, D = x.shape
    tr = TOK_BLOCK
    assert R % tr == 0 and n_ctx_rows % tr == 0
    nct = n_ctx_rows // tr
    return pl.pallas_call(
        _normmod_kernel,
        grid=(R // tr,),
        in_specs=[pl.BlockSpec((tr, D), lambda i: (i, 0)),
                  pl.BlockSpec((1, D), lambda i: (0, 0)),
                  _mod_spec(D, sc_slot, nct), _mod_spec(D, sh_slot, nct)],
        out_specs=pl.BlockSpec((tr, D), lambda i: (i, 0)),
        out_shape=jax.ShapeDtypeStruct((R, D), BF16),
        compiler_params=_params("arbitrary"),
        name="normmod",
    )(x, gain.reshape(1, D), mods, mods)


def _resid_kernel(x_ref, y_ref, gpost_ref, g_ref, *rest, with_next):
    xn = x_ref[...] + g_ref[0] * _rms(y_ref[...], gpost_ref[...])
    if with_next:
        gpre_ref, sc_ref, sh_ref, xo_ref, h_ref = rest
        xo_ref[...] = xn
        h_ref[...] = (_rms(xn, gpre_ref[...]) * (1.0 + sc_ref[0]) + sh_ref[0]).astype(h_ref.dtype)
    else:
        (xo_ref,) = rest
        xo_ref[...] = xn


def resid_norm(x, y, gain_post, mods, g_slot, n_ctx_rows, nxt=None, h_dtype=BF16):
    R, D = x.shape
    tr = TOK_BLOCK
    assert R % tr == 0 and n_ctx_rows % tr == 0
    nct = n_ctx_rows // tr
    row = pl.BlockSpec((tr, D), lambda i: (i, 0))
    vec = pl.BlockSpec((1, D), lambda i: (0, 0))
    in_specs = [row, row, vec, _mod_spec(D, g_slot, nct)]
    args = [x, y, gain_post.reshape(1, D), mods]
    out_specs = [row]
    out_shape = [jax.ShapeDtypeStruct((R, D), F32)]
    if nxt is not None:
        gain_pre, mods_n, sh_slot, sc_slot = nxt
        in_specs += [vec, _mod_spec(D, sc_slot, nct), _mod_spec(D, sh_slot, nct)]
        args += [gain_pre.reshape(1, D), mods_n, mods_n]
        out_specs.append(row)
        out_shape.append(jax.ShapeDtypeStruct((R, D), h_dtype))
    out = pl.pallas_call(
        functools.partial(_resid_kernel, with_next=nxt is not None),
        grid=(R // tr,),
        in_specs=in_specs, out_specs=out_specs, out_shape=out_shape,
        compiler_params=_params("arbitrary"),
        name="resid_norm",
    )(*args)
    return out if nxt is not None else out[0]


def _ffn_kernel(te_ref, nu_ref, x_ref, wg_ref, wu_ref, wd_ref, *rest, scale_rows):
    if scale_rows:
        rw_ref, o_ref = rest
    else:
        (o_ref,) = rest
    i, j = pl.program_id(0), pl.program_id(1)

    @pl.when(i < nu_ref[0])
    def _():
        x = x_ref[...]
        g = jnp.dot(x, wg_ref[0].astype(BF16), preferred_element_type=F32)
        u = jnp.dot(x, wu_ref[0].astype(BF16), preferred_element_type=F32)
        hid = (g * jax.nn.sigmoid(g) * u).astype(BF16)
        part = jnp.dot(hid, wd_ref[0].astype(BF16), preferred_element_type=F32)
        if scale_rows:
            part = part * rw_ref[...]

        @pl.when(j == 0)
        def _():
            o_ref[...] = part

        @pl.when(j > 0)
        def _():
            o_ref[...] += part

    @pl.when(jnp.logical_and(i >= nu_ref[0], j == 0))
    def _():
        o_ref[...] = jnp.zeros_like(o_ref)


def swiglu_ffn(x, w_gate, w_up, w_down, tile_expert, n_used, tm, tf, row_weight=None):
    R, D = x.shape
    E, _, F = w_gate.shape
    assert R % tm == 0 and F % tf == 0
    nf = F // tf

    def fblk(i, j, te, nu):
        return jnp.where(i < nu[0], j, nf - 1)

    in_specs = [pl.BlockSpec((tm, D), lambda i, j, te, nu: (i, 0)),
                pl.BlockSpec((1, D, tf), lambda i, j, te, nu: (te[i], 0, fblk(i, j, te, nu))),
                pl.BlockSpec((1, D, tf), lambda i, j, te, nu: (te[i], 0, fblk(i, j, te, nu))),
                pl.BlockSpec((1, tf, D), lambda i, j, te, nu: (te[i], fblk(i, j, te, nu), 0))]
    args = [x, w_gate, w_up, w_down]
    if row_weight is not None:
        in_specs.append(pl.BlockSpec((tm, 1), lambda i, j, te, nu: (i, 0)))
        args.append(row_weight)
    return pl.pallas_call(
        functools.partial(_ffn_kernel, scale_rows=row_weight is not None),
        grid_spec=pltpu.PrefetchScalarGridSpec(
            num_scalar_prefetch=2,
            grid=(R // tm, nf),
            in_specs=in_specs,
            out_specs=pl.BlockSpec((tm, D), lambda i, j, te, nu: (i, 0)),
        ),
        out_shape=jax.ShapeDtypeStruct((R, D), F32),
        compiler_params=_params("arbitrary", "arbitrary"),
        name="swiglu_ffn",
    )(tile_expert, n_used, *args)


def _attn_kernel(sink_ref, q_ref, *rest, has_window, seq_len):
    if has_window:
        kp_ref, ko_ref, kn_ref, vp_ref, vo_ref, vn_ref, kc_ref, vc_ref, o_ref = rest
    else:
        kc_ref, vc_ref, o_ref = rest
    i = pl.program_id(0)
    BQ = q_ref.shape[0]
    L = kc_ref.shape[0]
    G = ATT_KV_HEADS
    R = ATT_HEADS // G
    n_win = 3 * BQ if has_window else 0
    nk = n_win + L
    if has_window:
        row = lax.broadcasted_iota(jnp.int32, (R * BQ, nk), 0)
        col = lax.broadcasted_iota(jnp.int32, (R * BQ, nk), 1)
        qi = jnp.bitwise_and(row, BQ - 1)
        kpos = i * BQ + col - BQ
        in_band = jnp.abs(col - BQ - qi) <= WINDOW
        in_seq = jnp.logical_and(kpos >= 0, kpos < seq_len)
        valid = jnp.logical_or(col >= n_win, jnp.logical_and(in_band, in_seq))
    for g in range(G):
        cs = slice(g * HEAD_DIM, (g + 1) * HEAD_DIM)
        if has_window:
            k_all = jnp.concatenate([kp_ref[:, cs], ko_ref[:, cs], kn_ref[:, cs], kc_ref[:, cs]], axis=0)
            v_all = jnp.concatenate([vp_ref[:, cs], vo_ref[:, cs], vn_ref[:, cs], vc_ref[:, cs]], axis=0)
        else:
            k_all, v_all = kc_ref[:, cs], vc_ref[:, cs]
        q_g = jnp.concatenate(
            [q_ref[:, (g * R + r) * HEAD_DIM:(g * R + r + 1) * HEAD_DIM] for r in range(R)], axis=0)
        q_g = (q_g * (HEAD_DIM ** -0.5)).astype(BF16)
        s = lax.dot_general(q_g, k_all.astype(BF16), (((1,), (1,)), ((), ())),
                            preferred_element_type=F32)
        if has_window:
            s = jnp.where(valid, s, -jnp.inf)
        sink = jnp.concatenate([jnp.full((BQ, 1), sink_ref[g * R + r], F32) for r in range(R)], axis=0)
        m = jnp.maximum(sink, jnp.max(s, axis=-1, keepdims=True))
        p = jnp.exp(s - m)
        denom = jnp.exp(sink - m) + jnp.sum(p, axis=-1, keepdims=True)
        o = jnp.dot(p.astype(BF16), v_all.astype(BF16), preferred_element_type=F32) / denom
        for r in range(R):
            h = g * R + r
            o_ref[:, h * HEAD_DIM:(h + 1) * HEAD_DIM] = o[r * BQ:(r + 1) * BQ].astype(o_ref.dtype)


def sink_attention(q, k, v, kc, vc, sink, has_window):
    S, QW = q.shape
    L, KW = kc.shape
    BQ = ATT_BLOCK if has_window else S
    nb = S // BQ
    assert S % BQ == 0
    qspec = pl.BlockSpec((BQ, QW), lambda i: (i, 0))
    cspec = pl.BlockSpec((L, KW), lambda i: (0, 0))
    in_specs = [pl.BlockSpec(memory_space=pltpu.SMEM), qspec]
    args = [sink, q]
    if has_window:
        prev = pl.BlockSpec((BQ, KW), lambda i: (jnp.maximum(i - 1, 0), 0))
        own = pl.BlockSpec((BQ, KW), lambda i: (i, 0))
        nxt = pl.BlockSpec((BQ, KW), lambda i: (jnp.minimum(i + 1, nb - 1), 0))
        in_specs += [prev, own, nxt, prev, own, nxt]
        args += [k, k, k, v, v, v]
    in_specs += [cspec, cspec]
    args += [kc, vc]
    return pl.pallas_call(
        functools.partial(_attn_kernel, has_window=has_window, seq_len=S),
        grid=(nb,),
        in_specs=in_specs,
        out_specs=qspec,
        out_shape=jax.ShapeDtypeStruct((S, QW), BF16),
        compiler_params=_params("arbitrary"),
        name="sink_attention",
    )(*args)


def _rope_tables(n_tokens):
    n_freq = HEAD_DIM // 4
    t = jnp.arange(n_tokens)
    row = (t // GRID_W).astype(F32)
    col = (t % GRID_W).astype(F32)
    inv = ROPE_THETA ** (-jnp.arange(n_freq, dtype=F32) / n_freq)
    ang = jnp.stack([row[:, None] * inv, col[:, None] * inv], axis=1)
    return jnp.cos(ang), jnp.sin(ang)


def _apply_rope(x, n_heads, cos, sin):
    S = x.shape[0]
    n_freq = HEAD_DIM // 4
    xr = x.reshape(S, n_heads, 2, 2, n_freq)
    x1, x2 = xr[..., 0, :], xr[..., 1, :]
    cb, sb = cos[:, None], sin[:, None]
    out = jnp.stack([x1 * cb - x2 * sb, x1 * sb + x2 * cb], axis=-2)
    return out.reshape(S, n_heads * HEAD_DIM)


CHUNK_SHIFT = CHUNK.bit_length() - 1
assert 1 << CHUNK_SHIFT == CHUNK


def _chunk_masks(tb, reverse, strict):
    r = lax.broadcasted_iota(jnp.int32, (tb, tb), 0)
    c = lax.broadcasted_iota(jnp.int32, (tb, tb), 1)
    same = jnp.right_shift(r, CHUNK_SHIFT) == jnp.right_shift(c, CHUNK_SHIFT)
    if reverse:
        tri = (c > r) if strict else (c >= r)
    else:
        tri = (c < r) if strict else (c <= r)
    return same, jnp.logical_and(same, tri)


def _chunk_sums(ld, same, tri):
    tb = ld.shape[0]
    sel = jnp.concatenate([jnp.where(tri, 1.0, 0.0), jnp.where(same, 1.0, 0.0)], axis=0).astype(BF16)
    hi = ld.astype(BF16)
    rest = ld - hi.astype(F32)
    mid = rest.astype(BF16)
    lo = (rest - mid.astype(F32)).astype(BF16)
    dot = lambda p: jnp.dot(sel, p, preferred_element_type=F32)
    g = (dot(lo) + dot(mid)) + dot(hi)
    return g[:tb], g[tb:]


def _dot_tn(a, b):
    return lax.dot_general(a, b, (((0,), (0,)), ((), ())), preferred_element_type=F32)


def _dot_nt(a, b):
    return lax.dot_general(a, b, (((1,), (1,)), ((), ())), preferred_element_type=F32)


def _chunkrec_core(load_head, n_heads, tb, V, reverse, st_ref, dst_ref):
    @pl.when(pl.program_id(0) == 0)
    def _():
        st_ref[...] = jnp.zeros_like(st_ref)

    same, tri = _chunk_masks(tb, reverse, strict=False)
    per_head = []
    for h in range(n_heads):
        q, k, v, ld = load_head(h)
        g_cum, g_tot = _chunk_sums(ld, same, tri)
        q_dec = (q * jnp.exp(g_cum)).astype(BF16)
        k_inv = (k * jnp.exp(-g_cum)).astype(BF16)
        k_tail = k * jnp.exp(g_tot - g_cum)
        a = jnp.where(tri, _dot_nt(q_dec, k_inv), 0.0)
        o_intra = jnp.dot(a.astype(BF16), v.astype(BF16), preferred_element_type=F32)
        per_head.append((q_dec, k_tail, v, g_tot, o_intra))
    n_chunks = tb // CHUNK
    order = range(n_chunks - 1, -1, -1) if reverse else range(n_chunks)
    sts = [st_ref[h] for h in range(n_heads)]
    for c in order:
        sl = slice(c * CHUNK, (c + 1) * CHUNK)
        for h in range(n_heads):
            q_dec, k_tail, v, g_tot, o_intra = per_head[h]
            st = sts[h]
            dst_ref[sl, h * V:(h + 1) * V] = o_intra[sl] + _dot_nt(q_dec[sl], st.astype(BF16))
            d = jnp.exp(g_tot[c * CHUNK:c * CHUNK + 1, :])
            sts[h] = st * d + _dot_tn(v[sl], k_tail[sl])
    for h in range(n_heads):
        st_ref[h] = sts[h]


def _gated_head_norm(o, gain, gate_raw):
    y = o * lax.rsqrt(jnp.mean(o * o, axis=-1, keepdims=True) + NORM_EPS) * gain
    return y * (gate_raw * jax.nn.sigmoid(gate_raw))


def _chunk_mixer_kernel(*refs, reverse, n_heads, K, V, load_heads):
    if reverse:
        *in_refs, o_ref, st_ref = refs
        dst_ref = o_ref
    else:
        *in_refs, gate_ref, orev_ref, gain_ref, o_ref, st_ref, dst_ref = refs
    tb = o_ref.shape[0]
    _chunkrec_core(load_heads(*in_refs), n_heads, tb, V, reverse, st_ref, dst_ref)
    if not reverse:
        for h in range(n_heads):
            vs = slice(h * V, (h + 1) * V)
            o_ref[:, vs] = _gated_head_norm(dst_ref[:, vs] + orev_ref[:, vs], gain_ref[...],
                                            gate_ref[:, vs]).astype(o_ref.dtype)


def _hgrn_heads(qh_ref, ih_ref, fr_ref, lb_ref):
    def load(h):
        cs = slice(h * 128, (h + 1) * 128)
        lb = lb_ref[:, cs]
        f = lb + (1.0 - lb) * jax.nn.sigmoid(fr_ref[:, cs])
        qh = qh_ref[:, cs]
        return qh * jax.nn.sigmoid(qh), 1.0 - f, ih_ref[:, cs], jnp.log(f)
    return load


def _gla_heads(gq_ref, gk_ref, gv_ref, gd_ref, up_ref, bias_ref):
    x = jnp.dot(gd_ref[...].astype(BF16), up_ref[...].astype(BF16), preferred_element_type=F32) + bias_ref[...]
    lg = (jnp.minimum(x, 0.0) - jnp.log(1.0 + jnp.exp(-jnp.abs(x)))) * (1.0 / GLA_GATE_NORMALIZER)

    def load(h):
        ks, vs = slice(h * 128, (h + 1) * 128), slice(h * 256, (h + 1) * 256)
        return gq_ref[:, ks] * (128 ** -0.5), gk_ref[:, ks], gv_ref[:, vs], lg[:, ks]
    return load


def _seq_block_index(nblk, nctx, reverse):
    if not reverse:
        return lambda t: t
    return lambda t: jnp.where(t < nctx, nctx - 1 - t, nblk - 1 - (t - nctx))


def chunk_mixer(load_heads, inputs, n_heads, K, V, n_ctx_rows, reverse, final=None):
    T = inputs[0][0].shape[0]
    tb = TOK_BLOCK
    assert T % tb == 0 and n_ctx_rows % tb == 0
    nblk, nctx = T // tb, n_ctx_rows // tb
    blk = _seq_block_index(nblk, nctx, reverse)

    def spec(item):
        if len(item) == 1:
            return pl.BlockSpec(item[0].shape, lambda t: (0, 0))
        _, width, cb = item
        return pl.BlockSpec((tb, width), lambda t: (blk(t), cb))

    HV = n_heads * V
    ospec = pl.BlockSpec((tb, HV), lambda t: (blk(t), 0))
    scratch = [pltpu.VMEM((n_heads, V, K), F32)]
    if not reverse:
        gate, o_rev, gain = final
        inputs = list(inputs) + [gate, (o_rev, HV, 0), (gain.reshape(1, V),)]
        scratch.append(pltpu.VMEM((tb, HV), F32))
    return pl.pallas_call(
        functools.partial(_chunk_mixer_kernel, reverse=reverse, n_heads=n_heads, K=K, V=V, load_heads=load_heads),
        grid=(nblk,),
        in_specs=[spec(it) for it in inputs],
        out_specs=ospec,
        out_shape=jax.ShapeDtypeStruct((T, HV), F32 if reverse else BF16),
        scratch_shapes=scratch,
        compiler_params=_params("arbitrary"),
        name="chunk_mixer",
    )(*[it[0] for it in inputs])


def _mm_bf(a, b):
    return jnp.dot(a.astype(BF16), b.astype(BF16), preferred_element_type=F32)


def _rwkv_pre_kernel(r_ref, k_ref, v_ref, a_ref, b_ref, lw_ref,
                     qp_ref, ol_ref, plt_ref, zt_ref, gt_ref, *, reverse):
    N = RWKV_N
    hb, tb, _ = r_ref.shape
    same, tri_incl = _chunk_masks(tb, reverse, strict=False)
    _, tri_strict = _chunk_masks(tb, reverse, strict=True)
    xs, lps, rest = [], [], []
    for h in range(hb):
        r, k, v, a, b, lw = (ref[h] for ref in (r_ref, k_ref, v_ref, a_ref, b_ref, lw_ref))
        g_cum, g_tot = _chunk_sums(lw, same, tri_incl)
        gt_ref[h] = g_tot
        e_neg = jnp.exp(-g_cum)
        e_tail = jnp.exp(g_tot - g_cum)
        a_t = (a * jnp.exp(g_cum - lw)).astype(BF16)
        r_t = (r * jnp.exp(g_cum)).astype(BF16)
        b_t = (b * e_neg).astype(BF16)
        k_t = (k * e_neg).astype(BF16)
        a_ab = jnp.where(tri_strict, _dot_nt(a_t, b_t), 0.0)
        a_ak = jnp.where(tri_strict, _dot_nt(a_t, k_t), 0.0)
        a_rb = jnp.where(tri_incl, _dot_nt(r_t, b_t), 0.0)
        a_rk = jnp.where(tri_incl, _dot_nt(r_t, k_t), 0.0)
        xs.append(jnp.concatenate([a_t.astype(F32), _mm_bf(a_ak, v)], axis=1))
        lps.append(a_ab)
        rest.append((v, b * e_tail, k * e_tail, a_rb,
                     jnp.concatenate([r_t.astype(F32), _mm_bf(a_rk, v)], axis=1)))
    for j in range(CHUNK_SHIFT):
        xs = [x + _mm_bf(lp, x) for x, lp in zip(xs, lps)]
        if j < CHUNK_SHIFT - 1:
            lps = [_mm_bf(lp, lp) for lp in lps]
    for h in range(hb):
        v, b_h, k_h, a_rb, qo0 = rest[h]
        x = xs[h]
        qo = qo0 + _mm_bf(a_rb, x)
        qp_ref[h] = qo[:, :N]
        ol_ref[h] = qo[:, N:]
        w, uloc = x[:, :N], x[:, N:]
        for c in range(tb // CHUNK):
            sl = slice(c * CHUNK, (c + 1) * CHUNK)
            plt_ref[h, c] = _dot_tn(w[sl], b_h[sl])
            zt_ref[h, c] = _dot_tn(uloc[sl], b_h[sl]) + _dot_tn(v[sl], k_h[sl])


def _rwkv_scan_kernel(qp_ref, ol_ref, plt_ref, zt_ref, gt_ref, o_ref, st_ref, *, reverse):
    @pl.when(pl.program_id(0) == 0)
    def _():
        st_ref[...] = jnp.zeros_like(st_ref)

    n_heads, tb, _ = qp_ref.shape
    n_chunks = tb // CHUNK
    order = range(n_chunks - 1, -1, -1) if reverse else range(n_chunks)
    sts = [st_ref[h] for h in range(n_heads)]
    for c in order:
        sl = slice(c * CHUNK, (c + 1) * CHUNK)
        for h in range(n_heads):
            st = sts[h]
            o_ref[h, sl, :] = _dot_nt(qp_ref[h, sl, :].astype(BF16), st.astype(BF16)) + ol_ref[h, sl, :]
            d = jnp.exp(gt_ref[h, c * CHUNK:c * CHUNK + 1, :])
            sts[h] = st * d + _mm_bf(st, plt_ref[h, c]) + zt_ref[h, c]
    for h in range(n_heads):
        st_ref[h] = sts[h]


def rwkv7_mix(r, k, v, a, b, lw, n_ctx_rows, reverse):
    H, T, N = r.shape
    tb = TOK_BLOCK
    assert T % tb == 0 and n_ctx_rows % tb == 0
    nblk, nctx = T // tb, n_ctx_rows // tb
    ncb = tb // CHUNK
    hb = RWKV_HEAD_BLOCK
    assert H % hb == 0
    tok = pl.BlockSpec((hb, tb, N), lambda h, t: (h, t, 0))
    mat = pl.BlockSpec((hb, ncb, N, N), lambda h, t: (h, t, 0, 0))
    tok_shape = jax.ShapeDtypeStruct((H, T, N), F32)
    mat_shape = jax.ShapeDtypeStruct((H, T // CHUNK, N, N), F32)
    qp, ol, plt, zt, gt = pl.pallas_call(
        functools.partial(_rwkv_pre_kernel, reverse=reverse),
        grid=(H // hb, nblk),
        in_specs=[tok] * 6,
        out_specs=[tok, tok, mat, mat, tok],
        out_shape=[tok_shape, tok_shape, mat_shape, mat_shape, tok_shape],
        compiler_params=_params("arbitrary", "arbitrary"),
        name="rwkv_pre",
    )(r, k, v, a, b, lw)
    blk = _seq_block_index(nblk, nctx, reverse)
    tok_all = pl.BlockSpec((H, tb, N), lambda t: (0, blk(t), 0))
    mat_all = pl.BlockSpec((H, ncb, N, N), lambda t: (0, blk(t), 0, 0))
    return pl.pallas_call(
        functools.partial(_rwkv_scan_kernel, reverse=reverse),
        grid=(nblk,),
        in_specs=[tok_all, tok_all, mat_all, mat_all, tok_all],
        out_specs=tok_all,
        out_shape=tok_shape,
        scratch_shapes=[pltpu.VMEM((H, N, N), F32)],
        compiler_params=_params("arbitrary"),
        name="rwkv_scan",
    )(qp, ol, plt, zt, gt)


def _row_copy(src_hbm, src_row, dst_ref, dst_row, sem):
    return pltpu.make_async_copy(src_hbm.at[pl.ds(src_row, 1)], dst_ref.at[pl.ds(dst_row, 1)], sem)


def _gather_kernel(src_ref, x_hbm, o_ref, buf, sem):
    n = o_ref.shape[0]

    def issue(r, carry):
        _row_copy(x_hbm, src_ref[0, 0, r], buf, r, sem).start()
        return carry

    def wait(r, carry):
        _row_copy(x_hbm, 0, buf, r, sem).wait()
        return carry

    lax.fori_loop(0, n, issue, 0)
    lax.fori_loop(0, n, wait, 0)
    o_ref[...] = buf[...].astype(o_ref.dtype)


def gather_rows(x, src, tg):
    S, D = x.shape
    P = src.shape[0]
    assert P % tg == 0 and x.dtype == F32
    return pl.pallas_call(
        _gather_kernel,
        grid=(P // tg,),
        in_specs=[pl.BlockSpec((1, 1, tg), lambda i: (i, 0, 0), memory_space=pltpu.SMEM),
                  pl.BlockSpec(memory_space=pl.ANY)],
        out_specs=pl.BlockSpec((tg, D), lambda i: (i, 0)),
        out_shape=jax.ShapeDtypeStruct((P, D), BF16),
        scratch_shapes=[pltpu.VMEM((tg, D), F32), pltpu.SemaphoreType.DMA(())],
        compiler_params=_params("arbitrary"),
        name="gather_rows",
    )(src.reshape(P // tg, 1, tg), x)


def _combine_kernel(pos_ref, x_ref, ys_hbm, gpost_ref, g_ref, o_ref, buf0, buf1, sems):
    n = x_ref.shape[0]

    def issue(r, carry):
        _row_copy(ys_hbm, pos_ref[0, 0, 2 * r], buf0, r, sems.at[0]).start()
        _row_copy(ys_hbm, pos_ref[0, 0, 2 * r + 1], buf1, r, sems.at[1]).start()
        return carry

    def wait(r, carry):
        _row_copy(ys_hbm, 0, buf0, r, sems.at[0]).wait()
        _row_copy(ys_hbm, 0, buf1, r, sems.at[1]).wait()
        return carry

    lax.fori_loop(0, n, issue, 0)
    lax.fori_loop(0, n, wait, 0)
    f = buf0[...] + buf1[...]
    o_ref[...] = x_ref[...] + g_ref[0] * _rms(f, gpost_ref[...])


def combine_resid(x, ys, pos, gain_post, mods, g_slot):
    S, D = x.shape
    tc = TOK_BLOCK
    assert S % tc == 0
    row = pl.BlockSpec((tc, D), lambda i: (i, 0))
    return pl.pallas_call(
        _combine_kernel,
        grid=(S // tc,),
        in_specs=[pl.BlockSpec((1, 1, 2 * tc), lambda i: (i, 0, 0), memory_space=pltpu.SMEM),
                  row, pl.BlockSpec(memory_space=pl.ANY),
                  pl.BlockSpec((1, D), lambda i: (0, 0)), _mod_spec(D, g_slot, 0)],
        out_specs=row,
        out_shape=jax.ShapeDtypeStruct((S, D), F32),
        scratch_shapes=[pltpu.VMEM((tc, D), F32), pltpu.VMEM((tc, D), F32),
                        pltpu.SemaphoreType.DMA((2,))],
        compiler_params=_params("arbitrary"),
        name="combine_resid",
    )(pos.reshape(S // tc, 1, 2 * tc), x, ys, gain_post.reshape(1, D), mods)


def _route(logits, tm):
    S = logits.shape[0]
    top_val, top_idx = lax.top_k(logits, 2)
    weights = jax.nn.softmax(top_val, axis=-1)
    e_flat = top_idx.reshape(-1)
    onehot = (e_flat[:, None] == jnp.arange(N_EXPERTS)[None, :]).astype(jnp.int32)
    rank = jnp.take_along_axis(jnp.cumsum(onehot, axis=0), e_flat[:, None], axis=1)[:, 0] - 1
    counts = jnp.sum(onehot, axis=0)
    tiles_per = (counts + tm - 1) // tm
    tile_end = jnp.cumsum(tiles_per)
    start = (tile_end - tiles_per) * tm
    dest = start[e_flat] + rank
    n_rows = 2 * S + N_EXPERTS * tm
    n_tiles = n_rows // tm
    token = jnp.arange(2 * S, dtype=jnp.int32) // 2
    src = jnp.zeros((n_rows,), jnp.int32).at[dest].set(token)
    row_w = jnp.zeros((n_rows,), F32).at[dest].set(weights.reshape(-1))
    n_used = tile_end[-1].astype(jnp.int32)
    tile_id = jnp.minimum(jnp.arange(n_tiles, dtype=jnp.int32), n_used - 1)
    tile_expert = jnp.sum((tile_end[None, :] <= tile_id[:, None]).astype(jnp.int32), axis=1)
    tile_expert = jnp.minimum(tile_expert, N_EXPERTS - 1)
    return src, row_w.reshape(n_rows, 1), dest.astype(jnp.int32), tile_expert, n_used.reshape(1)


def _adaln(c, c_ctx, w, b):
    D = c.shape[-1]
    rows = jnp.zeros((8, D), F32).at[0].set(c[0]).at[1].set(c_ctx)
    m = matmul(jax.nn.silu(rows), w, 8, 2048)[:2] + b[None, :]
    return m.reshape(12, 1, D)


def _head_rms(o, gain, n_heads):
    T = o.shape[0]
    oh = o.reshape(T, n_heads, -1)
    y = oh * lax.rsqrt(jnp.mean(oh * oh, axis=-1, keepdims=True) + NORM_EPS) * gain
    return y.reshape(T, -1)


def _even_mixer(h, L, w_in, w_out, attn_sink, hgrn_norm, hgrn_lb):
    T = h.shape[0]
    S = T - L
    proj = matmul(h, jnp.concatenate([w_in[:, 1536:], w_in[:, :1536]], axis=1), 1280, 512)
    qa, ka, va = proj[:, 5120:6144], proj[:, 6144:6400], proj[:, 6400:6656]
    cos, sin = _rope_tables(S)
    att = sink_attention(_apply_rope(qa[L:], ATT_HEADS, cos, sin), _apply_rope(ka[L:], ATT_KV_HEADS, cos, sin),
                         va[L:], ka[:L], va[:L], attn_sink, True)
    att_c = sink_attention(qa[:L], None, None, ka[:L], va[:L], attn_sink, False)
    lb = (hgrn_lb.reshape(1, -1),)
    qh, ih = (proj, 1024, 0), (proj, 1024, 1)
    o_rev = chunk_mixer(_hgrn_heads, [qh, ih, (proj, 1024, 3), lb], 8, 128, 128, L, True)
    hg = chunk_mixer(_hgrn_heads, [qh, ih, (proj, 1024, 2), lb], 8, 128, 128, L, False,
                     final=((proj, 1024, 4), o_rev, hgrn_norm))
    ycat = jnp.concatenate([jnp.concatenate([att_c, att], axis=0), hg], axis=-1)
    return matmul(ycat, w_out, 1280, 1024)


def _token_shift(p, L, mu_prev, mu_next):
    def one(s):
        zero = jnp.zeros_like(s[:1])
        prev = jnp.concatenate([zero, s[:-1]], axis=0)
        nxt = jnp.concatenate([s[1:], zero], axis=0)
        return s + mu_prev * (prev - s) + mu_next * (nxt - s)
    return jnp.concatenate([one(p[:L]), one(p[L:])], axis=0)


def _odd_mixer(h, L, w_in, w_out, gla_gate_up_f, gla_gate_up_b, gla_gate_bias_f, gla_gate_bias_b, gla_norm,
               mu_prev, mu_next, w0_f, w0_b, w2_f, w2_b, a0, a2, g2, k_k, k_a, r_k, ln_w, ln_b):
    T = h.shape[0]
    GO = 3104
    cols = lambda a, b: w_in[:, a:b]
    w_big = jnp.concatenate([cols(0, 2048), cols(2080, 3104), cols(GO, GO + 3072), cols(GO + 3360, GO + 3616)], axis=1)
    w_small = jnp.concatenate([cols(2048, 2080), cols(GO + 3072, GO + 3360),
                               jnp.zeros((w_in.shape[0], 64), F32)], axis=1)
    pb = matmul(h, w_big, 1280, 640)
    ps = matmul(h, w_small, 1280, 384)
    def gla_inputs(up, row0, bias):
        up_rows = jnp.zeros((ps.shape[1], up.shape[1]), F32).at[row0:row0 + up.shape[0]].set(up)
        return [(pb, 512, 0), (pb, 512, 1), (pb, 1024, 1), (ps, ps.shape[1], 0), (up_rows,), (bias.reshape(1, -1),)]
    o_rev = chunk_mixer(_gla_heads, gla_inputs(gla_gate_up_b, 16, gla_gate_bias_b), 4, 128, 256, L, True)
    gla = chunk_mixer(_gla_heads, gla_inputs(gla_gate_up_f, 0, gla_gate_bias_f), 4, 128, 256, L, False,
                      final=((pb, 1024, 2), o_rev, gla_norm))
    def shift(p, a, b):
        return _token_shift(p, L, mu_prev[a:b], mu_next[a:b])
    rrkv = shift(pb[:, 3072:6144], 0, 3072)
    rr, rk, rv = rrkv[:, :1024], rrkv[:, 1024:2048], rrkv[:, 2048:]
    low = shift(ps[:, 32:320], 3072, 3360)
    wd_f, wd_b, ad = low[:, :96], low[:, 96:192], low[:, 192:288]
    gd = shift(pb[:, 6144:6400], 3360, 3616)
    a_sig = jax.nn.sigmoid(a0 + matmul(ad, a2, 1280, 1024))
    g_out = matmul(jax.nn.sigmoid(gd), g2, 1280, 1024)
    heads = lambda t: t.reshape(T, RWKV_HEADS, RWKV_N)
    kk = heads(rk * k_k)
    kk = kk * lax.rsqrt(jnp.sum(kk * kk, axis=-1, keepdims=True) + 1e-12)
    k_mod = heads(rk * (1.0 + (a_sig - 1.0) * k_a))
    r, v = heads(rr), heads(rv)
    b_vec = kk * heads(a_sig)
    hm = lambda t: jnp.moveaxis(t, 1, 0)
    ro = None
    for w0, wd, w2, rev in ((w0_f, wd_f, w2_f, False), (w0_b, wd_b, w2_b, True)):
        log_w = -jnp.exp(-jax.nn.softplus(-(w0 + matmul(jnp.tanh(wd), w2, 1280, 1024))) - 0.5)
        od = rwkv7_mix(hm(r), hm(k_mod), hm(v), hm(-kk), hm(b_vec), hm(heads(log_w)), L, rev)
        ro = od if ro is None else ro + od
    ro = jnp.moveaxis(ro, 0, 1)[L:]
    mu = jnp.mean(ro, axis=-1, keepdims=True)
    var = jnp.mean(jnp.square(ro - mu), axis=-1, keepdims=True)
    on = ((ro - mu) * lax.rsqrt(var + RWKV_LN_EPS)).reshape(T - L, -1) * ln_w + ln_b
    bonus = (jnp.sum(r[L:] * k_mod[L:] * r_k, axis=-1, keepdims=True) * v[L:]).reshape(T - L, -1)
    rw = (on + bonus) * g_out[L:]
    ycat = jnp.concatenate([gla[L:], rw.astype(BF16)], axis=-1)
    return matmul(ycat, w_out, 1024, 1024)


def kernel(x, c, ctx, c_ctx, hgrn_lb_logits, l0_ada_w, l0_ada_b, l0_norm_mix_pre, l0_norm_mix_post, l0_norm_ffn_pre, l0_norm_ffn_post, l0_w_in, l0_w_out, l0_attn_sink, l0_hgrn_norm, l0_ffn_w_gate, l0_ffn_w_up, l0_ffn_w_down, l1_ada_w, l1_ada_b, l1_norm_mix_pre, l1_norm_mix_post, l1_norm_ffn_pre, l1_norm_ffn_post, l1_w_in, l1_w_out, l1_gla_gate_up_f, l1_gla_gate_up_b, l1_gla_gate_bias_f, l1_gla_gate_bias_b, l1_gla_norm, l1_rwkv_mu_prev, l1_rwkv_mu_next, l1_rwkv_w0_f, l1_rwkv_w0_b, l1_rwkv_w2_f, l1_rwkv_w2_b, l1_rwkv_a0, l1_rwkv_a2, l1_rwkv_g2, l1_rwkv_k_k, l1_rwkv_k_a, l1_rwkv_r_k, l1_rwkv_ln_w, l1_rwkv_ln_b, l1_moe_router, l1_moe_w_gate, l1_moe_w_up, l1_moe_w_down):
    B, S, D = x.shape
    L = ctx.shape[1]
    assert B == 1
    T = L + S
    SH1, SC1, G1, SH2, SC2, G2 = range(6)
    xa = jnp.concatenate([ctx[0], x[0]], axis=0)
    hgrn_lb = jnp.cumsum(jax.nn.softmax(hgrn_lb_logits.astype(F32), axis=0), axis=0)
    m0 = _adaln(c, c_ctx, l0_ada_w, l0_ada_b)
    m1 = _adaln(c, c_ctx, l1_ada_w, l1_ada_b)

    h = normmod(xa, l0_norm_mix_pre, m0, SH1, SC1, L)
    y = _even_mixer(h, L, l0_w_in, l0_w_out, l0_attn_sink, l0_hgrn_norm, hgrn_lb[0])
    xa, h = resid_norm(xa, y, l0_norm_mix_post, m0, G1, L, nxt=(l0_norm_ffn_pre, m0, SH2, SC2))
    n_t = T // 1280
    f = swiglu_ffn(h, l0_ffn_w_gate[None], l0_ffn_w_up[None], l0_ffn_w_down[None],
                   jnp.zeros((n_t,), jnp.int32), jnp.full((1,), n_t, jnp.int32), 1280, 256)
    xa, h = resid_norm(xa, f, l0_norm_ffn_post, m0, G2, L, nxt=(l1_norm_mix_pre, m1, SH1, SC1))

    y = _odd_mixer(h, L, l1_w_in, l1_w_out, l1_gla_gate_up_f, l1_gla_gate_up_b, l1_gla_gate_bias_f,
                   l1_gla_gate_bias_b, l1_gla_norm, l1_rwkv_mu_prev, l1_rwkv_mu_next, l1_rwkv_w0_f, l1_rwkv_w0_b,
                   l1_rwkv_w2_f, l1_rwkv_w2_b, l1_rwkv_a0, l1_rwkv_a2, l1_rwkv_g2, l1_rwkv_k_k, l1_rwkv_k_a,
                   l1_rwkv_r_k, l1_rwkv_ln_w, l1_rwkv_ln_b)
    xl, h = resid_norm(xa[L:], y, l1_norm_mix_post, m1, G1, 0, nxt=(l1_norm_ffn_pre, m1, SH2, SC2), h_dtype=F32)
    router = jnp.concatenate([l1_moe_router, jnp.zeros((D, 128 - N_EXPERTS), F32)], axis=1)
    logits = matmul(h, router, 1024, 128)[:, :N_EXPERTS]
    src, row_w, dest, tile_expert, n_used = _route(logits, MOE_TILE)
    hs = gather_rows(h, src, TOK_BLOCK)
    ys = swiglu_ffn(hs, l1_moe_w_gate, l1_moe_w_up, l1_moe_w_down, tile_expert, n_used, MOE_TILE, 256,
                    row_weight=row_w)
    out = combine_resid(xl, ys, dest, l1_norm_ffn_post, m1, G2)
    return out[None]
```

```python
import functools

import jax
import jax.numpy as jnp
from jax import lax
from jax.experimental import pallas as pl
from jax.experimental.pallas import tpu as pltpu

F32 = jnp.float32
BF16 = jnp.bfloat16

NORM_EPS = 1e-6
CHUNK = 32
TOK_BLOCK = 256
ATT_BLOCK = 128
WINDOW = 128
HEAD_DIM = 128
ATT_HEADS = 8
ATT_KV_HEADS = 2
ROPE_THETA = 10000.0
GRID_W = 64
RWKV_N = 64
RWKV_HEADS = 16
RWKV_LN_EPS = 64e-5
RWKV_HEAD_BLOCK = 4
GLA_GATE_NORMALIZER = 16.0
N_EXPERTS = 8
MOE_TILE = 512
DENSE_TILE = 1280
VMEM_LIMIT_BYTES = 56 * 1024 * 1024


def _params(*sem):
    return pltpu.CompilerParams(dimension_semantics=sem, vmem_limit_bytes=VMEM_LIMIT_BYTES)


def _mm_kernel(x_ref, w_ref, o_ref, wbf_ref):
    @pl.when(pl.program_id(1) == 0)
    def _():
        wbf_ref[...] = w_ref[...].astype(BF16)

    o_ref[...] = jnp.dot(x_ref[...].astype(BF16), wbf_ref[...],
                         preferred_element_type=F32).astype(o_ref.dtype)


def matmul(x, w, tm, tn, out_dtype=F32):
    M, K = x.shape
    N = w.shape[1]
    assert M % tm == 0 and N % tn == 0, (M, tm, N, tn)
    return pl.pallas_call(
        _mm_kernel,
        grid=(N // tn, M // tm),
        in_specs=[pl.BlockSpec((tm, K), lambda j, i: (i, 0)),
                  pl.BlockSpec((K, tn), lambda j, i: (0, j))],
        out_specs=pl.BlockSpec((tm, tn), lambda j, i: (i, j)),
        out_shape=jax.ShapeDtypeStruct((M, N), out_dtype),
        scratch_shapes=[pltpu.VMEM((K, tn), BF16)],
        compiler_params=_params("arbitrary", "arbitrary"),
        name="matmul",
    )(x, w)


def _rms(x, gain):
    ms = jnp.mean(x * x, axis=-1, keepdims=True)
    return x * lax.rsqrt(ms + NORM_EPS) * gain


def _normmod_kernel(x_ref, gain_ref, sc_ref, sh_ref, h_ref):
    y = _rms(x_ref[...], gain_ref[...])
    h_ref[...] = (y * (1.0 + sc_ref[0]) + sh_ref[0]).astype(h_ref.dtype)


def _mod_spec(D, slot, n_ctx_tiles):
    return pl.BlockSpec((1, 1, D), lambda i: (jnp.where(i < n_ctx_tiles, 6, 0) + slot, 0, 0))


def normmod(x, gain, mods, sh_slot, sc_slot, n_ctx_rows):
    R, D = x.shape
    tr = TOK_BLOCK
    assert R % tr == 0 and n_ctx_rows % tr == 0
    nct = n_ctx_rows // tr
    return pl.pallas_call(
        _normmod_kernel,
        grid=(R // tr,),
        in_specs=[pl.BlockSpec((tr, D), lambda i: (i, 0)),
                  pl.BlockSpec((1, D), lambda i: (0, 0)),
                  _mod_spec(D, sc_slot, nct), _mod_spec(D, sh_slot, nct)],
        out_specs=pl.BlockSpec((tr, D), lambda i: (i, 0)),
        out_shape=jax.ShapeDtypeStruct((R, D), BF16),
        compiler_params=_params("arbitrary"),
        name="normmod",
    )(x, gain.reshape(1, D), mods, mods)


def _resid_kernel(x_ref, y_ref, gpost_ref, g_ref, *rest, with_next):
    xn = x_ref[...] + g_ref[0] * _rms(y_ref[...], gpost_ref[...])
    if with_next:
        gpre_ref, sc_ref, sh_ref, xo_ref, h_ref = rest
        xo_ref[...] = xn
        h_ref[...] = (_rms(xn, gpre_ref[...]) * (1.0 + sc_ref[0]) + sh_ref[0]).astype(h_ref.dtype)
    else:
        (xo_ref,) = rest
        xo_ref[...] = xn


def resid_norm(x, y, gain_post, mods, g_slot, n_ctx_rows, nxt=None, h_dtype=BF16):
    R, D = x.shape
    tr = TOK_BLOCK
    assert R % tr == 0 and n_ctx_rows % tr == 0
    nct = n_ctx_rows // tr
    row = pl.BlockSpec((tr, D), lambda i: (i, 0))
    vec = pl.BlockSpec((1, D), lambda i: (0, 0))
    in_specs = [row, row, vec, _mod_spec(D, g_slot, nct)]
    args = [x, y, gain_post.reshape(1, D), mods]
    out_specs = [row]
    out_shape = [jax.ShapeDtypeStruct((R, D), F32)]
    if nxt is not None:
        gain_pre, mods_n, sh_slot, sc_slot = nxt
        in_specs += [vec, _mod_spec(D, sc_slot, nct), _mod_spec(D, sh_slot, nct)]
        args += [gain_pre.reshape(1, D), mods_n, mods_n]
        out_specs.append(row)
        out_shape.append(jax.ShapeDtypeStruct((R, D), h_dtype))
    out = pl.pallas_call(
        functools.partial(_resid_kernel, with_next=nxt is not None),
        grid=(R // tr,),
        in_specs=in_specs, out_specs=out_specs, out_shape=out_shape,
        compiler_params=_params("arbitrary"),
        name="resid_norm",
    )(*args)
    return out if nxt is not None else out[0]


def _ffn_kernel(te_ref, nu_ref, x_ref, wg_ref, wu_ref, wd_ref, o_ref):
    i, j = pl.program_id(0), pl.program_id(1)

    @pl.when(i < nu_ref[0])
    def _():
        x = x_ref[...]
        g = jnp.dot(x, wg_ref[0], preferred_element_type=F32)
        u = jnp.dot(x, wu_ref[0], preferred_element_type=F32)
        hid = (g * jax.nn.sigmoid(g) * u).astype(BF16)
        part = jnp.dot(hid, wd_ref[0], preferred_element_type=F32)

        @pl.when(j == 0)
        def _():
            o_ref[...] = part

        @pl.when(j > 0)
        def _():
            o_ref[...] += part

    @pl.when(jnp.logical_and(i >= nu_ref[0], j == 0))
    def _():
        o_ref[...] = jnp.zeros_like(o_ref)


def swiglu_ffn(x, w_gate, w_up, w_down, tile_expert, n_used, tm, tf):
    R, D = x.shape
    E, _, F = w_gate.shape
    assert R % tm == 0 and F % tf == 0 and w_gate.dtype == BF16 and x.dtype == BF16
    nf = F // tf

    def fblk(i, j, te, nu):
        return jnp.where(i < nu[0], j, nf - 1)

    in_specs = [pl.BlockSpec((tm, D), lambda i, j, te, nu: (i, 0)),
                pl.BlockSpec((1, D, tf), lambda i, j, te, nu: (te[i], 0, fblk(i, j, te, nu))),
                pl.BlockSpec((1, D, tf), lambda i, j, te, nu: (te[i], 0, fblk(i, j, te, nu))),
                pl.BlockSpec((1, tf, D), lambda i, j, te, nu: (te[i], fblk(i, j, te, nu), 0))]
    args = [x, w_gate, w_up, w_down]
    return pl.pallas_call(
        _ffn_kernel,
        grid_spec=pltpu.PrefetchScalarGridSpec(
            num_scalar_prefetch=2,
            grid=(R // tm, nf),
            in_specs=in_specs,
            out_specs=pl.BlockSpec((tm, D), lambda i, j, te, nu: (i, 0)),
        ),
        out_shape=jax.ShapeDtypeStruct((R, D), F32),
        compiler_params=_params("arbitrary", "arbitrary"),
        name="swiglu_ffn",
    )(tile_expert, n_used, *args)


def _rope(x, c, sg):
    lane = lax.broadcasted_iota(jnp.int32, x.shape, 1)
    first_half = jnp.bitwise_and(lane, HEAD_DIM // 2 - 1) < HEAD_DIM // 4
    partner = jnp.where(first_half, pltpu.roll(x, HEAD_DIM - HEAD_DIM // 4, 1), pltpu.roll(x, HEAD_DIM // 4, 1))
    return x * c + partner * sg


def _attn_kernel(sink_ref, q_ref, *rest, has_window, seq_len):
    if has_window:
        (kp_ref, ko_ref, kn_ref, vp_ref, vo_ref, vn_ref, cp_ref, co_ref, cn_ref, sp_ref, so_ref, sn_ref,
         kc_ref, vc_ref, o_ref) = rest
    else:
        kc_ref, vc_ref, o_ref = rest
    i = pl.program_id(0)
    BQ = q_ref.shape[0]
    L = kc_ref.shape[0]
    G = ATT_KV_HEADS
    R = ATT_HEADS // G
    n_win = 3 * BQ if has_window else 0
    nk = n_win + L
    if has_window:
        row = lax.broadcasted_iota(jnp.int32, (R * BQ, nk), 0)
        col = lax.broadcasted_iota(jnp.int32, (R * BQ, nk), 1)
        qi = jnp.bitwise_and(row, BQ - 1)
        kpos = i * BQ + col - BQ
        in_band = jnp.abs(col - BQ - qi) <= WINDOW
        in_seq = jnp.logical_and(kpos >= 0, kpos < seq_len)
        valid = jnp.logical_or(col >= n_win, jnp.logical_and(in_band, in_seq))
    for g in range(G):
        cs = slice(g * HEAD_DIM, (g + 1) * HEAD_DIM)
        q_heads = [q_ref[:, (g * R + r) * HEAD_DIM:(g * R + r + 1) * HEAD_DIM] for r in range(R)]
        if has_window:
            q_heads = [_rope(q, co_ref[...], so_ref[...]) for q in q_heads]
            k_all = jnp.concatenate([_rope(kp_ref[:, cs], cp_ref[...], sp_ref[...]),
                                     _rope(ko_ref[:, cs], co_ref[...], so_ref[...]),
                                     _rope(kn_ref[:, cs], cn_ref[...], sn_ref[...]), kc_ref[:, cs]], axis=0)
            v_all = jnp.concatenate([vp_ref[:, cs], vo_ref[:, cs], vn_ref[:, cs], vc_ref[:, cs]], axis=0)
        else:
            k_all, v_all = kc_ref[:, cs], vc_ref[:, cs]
        q_g = (jnp.concatenate(q_heads, axis=0) * (HEAD_DIM ** -0.5)).astype(BF16)
        s = lax.dot_general(q_g, k_all.astype(BF16), (((1,), (1,)), ((), ())),
                            preferred_element_type=F32)
        if has_window:
            s = jnp.where(valid, s, -jnp.inf)
        sink = jnp.concatenate([jnp.full((BQ, 1), sink_ref[g * R + r], F32) for r in range(R)], axis=0)
        m = jnp.maximum(sink, jnp.max(s, axis=-1, keepdims=True))
        p = jnp.exp(s - m)
        denom = jnp.exp(sink - m) + jnp.sum(p, axis=-1, keepdims=True)
        o = jnp.dot(p.astype(BF16), v_all.astype(BF16), preferred_element_type=F32) / denom
        for r in range(R):
            h = g * R + r
            o_ref[:, h * HEAD_DIM:(h + 1) * HEAD_DIM] = o[r * BQ:(r + 1) * BQ].astype(o_ref.dtype)


def sink_attention(proj, q_col, k_col, v_col, L, sink, rope):
    QW, KW = ATT_HEADS * HEAD_DIM, ATT_KV_HEADS * HEAD_DIM
    has_window = rope is not None
    S = proj.shape[0] - L if has_window else L
    BQ = ATT_BLOCK if has_window else L
    nb = S // BQ
    assert S % BQ == 0 and L % BQ == 0
    r0 = L // BQ if has_window else 0
    in_specs = [pl.BlockSpec(memory_space=pltpu.SMEM), pl.BlockSpec((BQ, QW), lambda i: (r0 + i, q_col))]
    args = [sink, proj]
    if has_window:
        prev = lambda i: jnp.maximum(i - 1, 0)
        nxt = lambda i: jnp.minimum(i + 1, nb - 1)
        for col in (k_col, v_col):
            in_specs += [pl.BlockSpec((BQ, KW), lambda i, col=col: (r0 + prev(i), col)),
                         pl.BlockSpec((BQ, KW), lambda i, col=col: (r0 + i, col)),
                         pl.BlockSpec((BQ, KW), lambda i, col=col: (r0 + nxt(i), col))]
            args += [proj] * 3
        for tab in rope:
            in_specs += [pl.BlockSpec((BQ, HEAD_DIM), lambda i: (prev(i), 0)),
                         pl.BlockSpec((BQ, HEAD_DIM), lambda i: (i, 0)),
                         pl.BlockSpec((BQ, HEAD_DIM), lambda i: (nxt(i), 0))]
            args += [tab] * 3
    in_specs += [pl.BlockSpec((L, KW), lambda i: (0, k_col)), pl.BlockSpec((L, KW), lambda i: (0, v_col))]
    args += [proj, proj]
    return pl.pallas_call(
        functools.partial(_attn_kernel, has_window=has_window, seq_len=S),
        grid=(nb,),
        in_specs=in_specs,
        out_specs=pl.BlockSpec((BQ, QW), lambda i: (i, 0)),
        out_shape=jax.ShapeDtypeStruct((S, QW), BF16),
        compiler_params=_params("arbitrary"),
        name="sink_attention",
    )(*args)


def _rope_tables(n_tokens):
    n_freq = HEAD_DIM // 4
    t = jnp.arange(n_tokens)
    row = (t // GRID_W).astype(F32)
    col = (t % GRID_W).astype(F32)
    inv = ROPE_THETA ** (-jnp.arange(n_freq, dtype=F32) / n_freq)
    ar, ac = row[:, None] * inv, col[:, None] * inv
    c = jnp.concatenate([jnp.cos(ar), jnp.cos(ar), jnp.cos(ac), jnp.cos(ac)], axis=1)
    sg = jnp.concatenate([-jnp.sin(ar), jnp.sin(ar), -jnp.sin(ac), jnp.sin(ac)], axis=1)
    return c, sg


CHUNK_SHIFT = CHUNK.bit_length() - 1
assert 1 << CHUNK_SHIFT == CHUNK


def _chunk_masks(tb, reverse, strict):
    r = lax.broadcasted_iota(jnp.int32, (tb, tb), 0)
    c = lax.broadcasted_iota(jnp.int32, (tb, tb), 1)
    same = jnp.right_shift(r, CHUNK_SHIFT) == jnp.right_shift(c, CHUNK_SHIFT)
    if reverse:
        tri = (c > r) if strict else (c >= r)
    else:
        tri = (c < r) if strict else (c <= r)
    return same, jnp.logical_and(same, tri)


def _chunk_sums(ld, same, tri):
    tb = ld.shape[0]
    sel = jnp.concatenate([jnp.where(tri, 1.0, 0.0), jnp.where(same, 1.0, 0.0)], axis=0).astype(BF16)
    hi = ld.astype(BF16)
    rest = ld - hi.astype(F32)
    mid = rest.astype(BF16)
    lo = (rest - mid.astype(F32)).astype(BF16)
    dot = lambda p: jnp.dot(sel, p, preferred_element_type=F32)
    g = (dot(lo) + dot(mid)) + dot(hi)
    return g[:tb], g[tb:]


def _dot_tn(a, b):
    return lax.dot_general(a, b, (((0,), (0,)), ((), ())), preferred_element_type=F32)


def _dot_nt(a, b):
    return lax.dot_general(a, b, (((1,), (1,)), ((), ())), preferred_element_type=F32)


def _chunkrec_core(load_head, n_heads, tb, V, reverse, st_ref, dst_ref):
    @pl.when(pl.program_id(0) == 0)
    def _():
        st_ref[...] = jnp.zeros_like(st_ref)

    same, tri = _chunk_masks(tb, reverse, strict=False)
    per_head = []
    for h in range(n_heads):
        q, k, v, ld = load_head(h)
        g_cum, g_tot = _chunk_sums(ld, same, tri)
        q_dec = (q * jnp.exp(g_cum)).astype(BF16)
        k_inv = (k * jnp.exp(-g_cum)).astype(BF16)
        k_tail = k * jnp.exp(g_tot - g_cum)
        a = jnp.where(tri, _dot_nt(q_dec, k_inv), 0.0)
        o_intra = jnp.dot(a.astype(BF16), v.astype(BF16), preferred_element_type=F32)
        per_head.append((q_dec, k_tail, v, g_tot, o_intra))
    n_chunks = tb // CHUNK
    order = range(n_chunks - 1, -1, -1) if reverse else range(n_chunks)
    sts = [st_ref[h] for h in range(n_heads)]
    for c in order:
        sl = slice(c * CHUNK, (c + 1) * CHUNK)
        for h in range(n_heads):
            q_dec, k_tail, v, g_tot, o_intra = per_head[h]
            st = sts[h]
            dst_ref[sl, h * V:(h + 1) * V] = o_intra[sl] + _dot_nt(q_dec[sl], st.astype(BF16))
            d = jnp.exp(g_tot[c * CHUNK:c * CHUNK + 1, :])
            sts[h] = st * d + _dot_tn(v[sl], k_tail[sl])
    for h in range(n_heads):
        st_ref[h] = sts[h]


def _gated_head_norm(o, gain, gate_raw):
    y = o * lax.rsqrt(jnp.mean(o * o, axis=-1, keepdims=True) + NORM_EPS) * gain
    return y * (gate_raw * jax.nn.sigmoid(gate_raw))


def _chunk_mixer_kernel(*refs, reverse, n_heads, K, V, load_heads):
    if reverse:
        *in_refs, o_ref, st_ref = refs
        dst_ref = o_ref
    else:
        *in_refs, gate_ref, orev_ref, gain_ref, o_ref, st_ref, dst_ref = refs
    tb = o_ref.shape[0]
    _chunkrec_core(load_heads(*in_refs), n_heads, tb, V, reverse, st_ref, dst_ref)
    if not reverse:
        for h in range(n_heads):
            vs = slice(h * V, (h + 1) * V)
            o_ref[:, vs] = _gated_head_norm(dst_ref[:, vs] + orev_ref[:, vs], gain_ref[...],
                                            gate_ref[:, vs]).astype(o_ref.dtype)


def _hgrn_heads(qh_ref, ih_ref, fr_ref, lb_ref):
    def load(h):
        cs = slice(h * 128, (h + 1) * 128)
        lb = lb_ref[:, cs]
        f = lb + (1.0 - lb) * jax.nn.sigmoid(fr_ref[:, cs])
        qh = qh_ref[:, cs]
        return qh * jax.nn.sigmoid(qh), 1.0 - f, ih_ref[:, cs], jnp.log(f)
    return load


def _gla_heads(gq_ref, gk_ref, gv_ref, gd_ref, up_ref, bias_ref):
    x = jnp.dot(gd_ref[...].astype(BF16), up_ref[...].astype(BF16), preferred_element_type=F32) + bias_ref[...]
    lg = (jnp.minimum(x, 0.0) - jnp.log(1.0 + jnp.exp(-jnp.abs(x)))) * (1.0 / GLA_GATE_NORMALIZER)

    def load(h):
        ks, vs = slice(h * 128, (h + 1) * 128), slice(h * 256, (h + 1) * 256)
        return gq_ref[:, ks] * (128 ** -0.5), gk_ref[:, ks], gv_ref[:, vs], lg[:, ks]
    return load


def _seq_block_index(nblk, nctx, reverse):
    if not reverse:
        return lambda t: t
    return lambda t: jnp.where(t < nctx, nctx - 1 - t, nblk - 1 - (t - nctx))


def chunk_mixer(load_heads, inputs, n_heads, K, V, n_ctx_rows, reverse, final=None):
    T = inputs[0][0].shape[0]
    tb = TOK_BLOCK
    assert T % tb == 0 and n_ctx_rows % tb == 0
    nblk, nctx = T // tb, n_ctx_rows // tb
    blk = _seq_block_index(nblk, nctx, reverse)

    def spec(item):
        if len(item) == 1:
            return pl.BlockSpec(item[0].shape, lambda t: (0, 0))
        _, width, cb = item
        return pl.BlockSpec((tb, width), lambda t: (blk(t), cb))

    HV = n_heads * V
    ospec = pl.BlockSpec((tb, HV), lambda t: (blk(t), 0))
    scratch = [pltpu.VMEM((n_heads, V, K), F32)]
    if not reverse:
        gate, o_rev, gain = final
        inputs = list(inputs) + [gate, (o_rev, HV, 0), (gain.reshape(1, V),)]
        scratch.append(pltpu.VMEM((tb, HV), F32))
    return pl.pallas_call(
        functools.partial(_chunk_mixer_kernel, reverse=reverse, n_heads=n_heads, K=K, V=V, load_heads=load_heads),
        grid=(nblk,),
        in_specs=[spec(it) for it in inputs],
        out_specs=ospec,
        out_shape=jax.ShapeDtypeStruct((T, HV), F32 if reverse else BF16),
        scratch_shapes=scratch,
        compiler_params=_params("arbitrary"),
        name="chunk_mixer",
    )(*[it[0] for it in inputs])


def _mm_bf(a, b):
    return jnp.dot(a.astype(BF16), b.astype(BF16), preferred_element_type=F32)


def _rwkv_pre_kernel(r_ref, k_ref, v_ref, a_ref, b_ref, lw_ref,
                     qp_ref, ol_ref, plt_ref, zt_ref, gt_ref, *, reverse):
    N = RWKV_N
    hb, tb, _ = r_ref.shape
    same, tri_incl = _chunk_masks(tb, reverse, strict=False)
    _, tri_strict = _chunk_masks(tb, reverse, strict=True)
    xs, lps, rest = [], [], []
    for h in range(hb):
        r, k, v, a, b, lw = (ref[h] for ref in (r_ref, k_ref, v_ref, a_ref, b_ref, lw_ref))
        g_cum, g_tot = _chunk_sums(lw, same, tri_incl)
        gt_ref[h] = g_tot
        e_neg = jnp.exp(-g_cum)
        e_tail = jnp.exp(g_tot - g_cum)
        a_t = (a * jnp.exp(g_cum - lw)).astype(BF16)
        r_t = (r * jnp.exp(g_cum)).astype(BF16)
        b_t = (b * e_neg).astype(BF16)
        k_t = (k * e_neg).astype(BF16)
        a_ab = jnp.where(tri_strict, _dot_nt(a_t, b_t), 0.0)
        a_ak = jnp.where(tri_strict, _dot_nt(a_t, k_t), 0.0)
        a_rb = jnp.where(tri_incl, _dot_nt(r_t, b_t), 0.0)
        a_rk = jnp.where(tri_incl, _dot_nt(r_t, k_t), 0.0)
        xs.append(jnp.concatenate([a_t.astype(F32), _mm_bf(a_ak, v)], axis=1))
        lps.append(a_ab)
        rest.append((v, b * e_tail, k * e_tail, a_rb,
                     jnp.concatenate([r_t.astype(F32), _mm_bf(a_rk, v)], axis=1)))
    for j in range(CHUNK_SHIFT):
        xs = [x + _mm_bf(lp, x) for x, lp in zip(xs, lps)]
        if j < CHUNK_SHIFT - 1:
            lps = [_mm_bf(lp, lp) for lp in lps]
    for h in range(hb):
        v, b_h, k_h, a_rb, qo0 = rest[h]
        x = xs[h]
        qo = qo0 + _mm_bf(a_rb, x)
        qp_ref[h] = qo[:, :N]
        ol_ref[h] = qo[:, N:]
        w, uloc = x[:, :N], x[:, N:]
        for c in range(tb // CHUNK):
            sl = slice(c * CHUNK, (c + 1) * CHUNK)
            plt_ref[h, c] = _dot_tn(w[sl], b_h[sl])
            zt_ref[h, c] = _dot_tn(uloc[sl], b_h[sl]) + _dot_tn(v[sl], k_h[sl])


def _rwkv_scan_kernel(qp_ref, ol_ref, plt_ref, zt_ref, gt_ref, o_ref, st_ref, *, reverse):
    @pl.when(pl.program_id(0) == 0)
    def _():
        st_ref[...] = jnp.zeros_like(st_ref)

    n_heads, tb, _ = qp_ref.shape
    n_chunks = tb // CHUNK
    order = range(n_chunks - 1, -1, -1) if reverse else range(n_chunks)
    sts = [st_ref[h] for h in range(n_heads)]
    for c in order:
        sl = slice(c * CHUNK, (c + 1) * CHUNK)
        for h in range(n_heads):
            st = sts[h]
            o_ref[h, sl, :] = _dot_nt(qp_ref[h, sl, :].astype(BF16), st.astype(BF16)) + ol_ref[h, sl, :]
            d = jnp.exp(gt_ref[h, c * CHUNK:c * CHUNK + 1, :])
            sts[h] = st * d + _mm_bf(st, plt_ref[h, c]) + zt_ref[h, c]
    for h in range(n_heads):
        st_ref[h] = sts[h]


def rwkv7_mix(r, k, v, a, b, lw, n_ctx_rows, reverse):
    H, T, N = r.shape
    tb = TOK_BLOCK
    assert T % tb == 0 and n_ctx_rows % tb == 0
    nblk, nctx = T // tb, n_ctx_rows // tb
    ncb = tb // CHUNK
    hb = RWKV_HEAD_BLOCK
    assert H % hb == 0
    tok = pl.BlockSpec((hb, tb, N), lambda h, t: (h, t, 0))
    mat = pl.BlockSpec((hb, ncb, N, N), lambda h, t: (h, t, 0, 0))
    tok_shape = jax.ShapeDtypeStruct((H, T, N), F32)
    mat_shape = jax.ShapeDtypeStruct((H, T // CHUNK, N, N), F32)
    qp, ol, plt, zt, gt = pl.pallas_call(
        functools.partial(_rwkv_pre_kernel, reverse=reverse),
        grid=(H // hb, nblk),
        in_specs=[tok] * 6,
        out_specs=[tok, tok, mat, mat, tok],
        out_shape=[tok_shape, tok_shape, mat_shape, mat_shape, tok_shape],
        compiler_params=_params("arbitrary", "arbitrary"),
        name="rwkv_pre",
    )(r, k, v, a, b, lw)
    blk = _seq_block_index(nblk, nctx, reverse)
    tok_all = pl.BlockSpec((H, tb, N), lambda t: (0, blk(t), 0))
    mat_all = pl.BlockSpec((H, ncb, N, N), lambda t: (0, blk(t), 0, 0))
    return pl.pallas_call(
        functools.partial(_rwkv_scan_kernel, reverse=reverse),
        grid=(nblk,),
        in_specs=[tok_all, tok_all, mat_all, mat_all, tok_all],
        out_specs=tok_all,
        out_shape=tok_shape,
        scratch_shapes=[pltpu.VMEM((H, N, N), F32)],
        compiler_params=_params("arbitrary"),
        name="rwkv_scan",
    )(qp, ol, plt, zt, gt)


def _row_copy(src_hbm, src_row, dst_ref, dst_row, sem):
    return pltpu.make_async_copy(src_hbm.at[pl.ds(src_row, 1)], dst_ref.at[pl.ds(dst_row, 1)], sem)


def _gather_kernel(src_ref, nxt_ref, x_hbm, o_ref, buf, sems):
    i, n_steps = pl.program_id(0), pl.num_programs(0)
    n = o_ref.shape[0]

    def issue(idx_ref, slot):
        def body(r, carry):
            _row_copy(x_hbm, idx_ref[0, 0, r], buf.at[slot], r, sems.at[slot]).start()
            return carry
        lax.fori_loop(0, n, body, 0)

    @pl.when(i == 0)
    def _():
        issue(src_ref, 0)

    @pl.when(i + 1 < n_steps)
    def _():
        issue(nxt_ref, (i + 1) % 2)

    slot = i % 2

    def wait(r, carry):
        _row_copy(x_hbm, 0, buf.at[slot], r, sems.at[slot]).wait()
        return carry

    lax.fori_loop(0, n, wait, 0)
    o_ref[...] = buf[slot].astype(o_ref.dtype)


def gather_rows(x, src, tg):
    S, D = x.shape
    P = src.shape[0]
    assert P % tg == 0 and x.dtype == F32
    n_steps = P // tg
    src3 = src.reshape(n_steps, 1, tg)
    return pl.pallas_call(
        _gather_kernel,
        grid=(n_steps,),
        in_specs=[pl.BlockSpec((1, 1, tg), lambda i: (i, 0, 0), memory_space=pltpu.SMEM),
                  pl.BlockSpec((1, 1, tg), lambda i: (jnp.minimum(i + 1, n_steps - 1), 0, 0),
                               memory_space=pltpu.SMEM),
                  pl.BlockSpec(memory_space=pl.ANY)],
        out_specs=pl.BlockSpec((tg, D), lambda i: (i, 0)),
        out_shape=jax.ShapeDtypeStruct((P, D), BF16),
        scratch_shapes=[pltpu.VMEM((2, tg, D), F32), pltpu.SemaphoreType.DMA((2,))],
        compiler_params=_params("arbitrary"),
        name="gather_rows",
    )(src3, src3, x)


def _combine_kernel(pos_ref, nxt_ref, x_ref, w_ref, ys_hbm, gpost_ref, g_ref, o_ref, buf, sems):
    i, n_steps = pl.program_id(0), pl.num_programs(0)
    n = x_ref.shape[0]

    def issue(idx_ref, slot):
        def body(r, carry):
            for k in range(2):
                _row_copy(ys_hbm, idx_ref[0, 0, 2 * r + k], buf.at[slot, k], r, sems.at[slot, k]).start()
            return carry
        lax.fori_loop(0, n, body, 0)

    @pl.when(i == 0)
    def _():
        issue(pos_ref, 0)

    @pl.when(i + 1 < n_steps)
    def _():
        issue(nxt_ref, (i + 1) % 2)

    slot = i % 2

    def wait(r, carry):
        for k in range(2):
            _row_copy(ys_hbm, 0, buf.at[slot, k], r, sems.at[slot, k]).wait()
        return carry

    lax.fori_loop(0, n, wait, 0)
    w = w_ref[...]
    f = w[:, 0:1] * buf[slot, 0] + w[:, 1:2] * buf[slot, 1]
    o_ref[...] = x_ref[...] + g_ref[0] * _rms(f, gpost_ref[...])


def combine_resid(x, ys, pos, weights, gain_post, mods, g_slot):
    S, D = x.shape
    tc = TOK_BLOCK
    assert S % tc == 0
    n_steps = S // tc
    row = pl.BlockSpec((tc, D), lambda i: (i, 0))
    pos3 = pos.reshape(n_steps, 1, 2 * tc)
    return pl.pallas_call(
        _combine_kernel,
        grid=(n_steps,),
        in_specs=[pl.BlockSpec((1, 1, 2 * tc), lambda i: (i, 0, 0), memory_space=pltpu.SMEM),
                  pl.BlockSpec((1, 1, 2 * tc), lambda i: (jnp.minimum(i + 1, n_steps - 1), 0, 0),
                               memory_space=pltpu.SMEM),
                  row, pl.BlockSpec((tc, 2), lambda i: (i, 0)), pl.BlockSpec(memory_space=pl.ANY),
                  pl.BlockSpec((1, D), lambda i: (0, 0)), _mod_spec(D, g_slot, 0)],
        out_specs=row,
        out_shape=jax.ShapeDtypeStruct((S, D), F32),
        scratch_shapes=[pltpu.VMEM((2, 2, tc, D), F32), pltpu.SemaphoreType.DMA((2, 2))],
        compiler_params=_params("arbitrary"),
        name="combine_resid",
    )(pos3, pos3, x, weights, ys, gain_post.reshape(1, D), mods)


def _route(logits, tm):
    S = logits.shape[0]
    top_val, top_idx = lax.top_k(logits, 2)
    weights = jax.nn.softmax(top_val, axis=-1)
    e_flat = top_idx.reshape(-1)
    onehot = (e_flat[:, None] == jnp.arange(N_EXPERTS)[None, :]).astype(jnp.int32)
    rank = jnp.take_along_axis(jnp.cumsum(onehot, axis=0), e_flat[:, None], axis=1)[:, 0] - 1
    counts = jnp.sum(onehot, axis=0)
    tiles_per = (counts + tm - 1) // tm
    tile_end = jnp.cumsum(tiles_per)
    start = (tile_end - tiles_per) * tm
    dest = start[e_flat] + rank
    n_rows = 2 * S + N_EXPERTS * tm
    n_tiles = n_rows // tm
    token = jnp.arange(2 * S, dtype=jnp.int32) // 2
    src = jnp.zeros((n_rows,), jnp.int32).at[dest].set(token)
    n_used = tile_end[-1].astype(jnp.int32)
    tile_id = jnp.minimum(jnp.arange(n_tiles, dtype=jnp.int32), n_used - 1)
    tile_expert = jnp.sum((tile_end[None, :] <= tile_id[:, None]).astype(jnp.int32), axis=1)
    tile_expert = jnp.minimum(tile_expert, N_EXPERTS - 1)
    return src, weights, dest.astype(jnp.int32), tile_expert, n_used.reshape(1)


def _adaln(c, c_ctx, w, b):
    D = c.shape[-1]
    rows = jnp.zeros((8, D), F32).at[0].set(c[0]).at[1].set(c_ctx)
    m = matmul(jax.nn.silu(rows), w, 8, 2048)[:2] + b[None, :]
    return m.reshape(12, 1, D)


def _head_rms(o, gain, n_heads):
    T = o.shape[0]
    oh = o.reshape(T, n_heads, -1)
    y = oh * lax.rsqrt(jnp.mean(oh * oh, axis=-1, keepdims=True) + NORM_EPS) * gain
    return y.reshape(T, -1)


def _even_mixer(h, L, w_in, w_out, attn_sink, hgrn_norm, hgrn_lb):
    T = h.shape[0]
    S = T - L
    proj = matmul(h, jnp.concatenate([w_in[:, 1536:], w_in[:, :1536]], axis=1), 1280, 512)
    att = sink_attention(proj, 5, 24, 25, L, attn_sink, _rope_tables(S))
    att_c = sink_attention(proj, 5, 24, 25, L, attn_sink, None)
    lb = (hgrn_lb.reshape(1, -1),)
    qh, ih = (proj, 1024, 0), (proj, 1024, 1)
    o_rev = chunk_mixer(_hgrn_heads, [qh, ih, (proj, 1024, 3), lb], 8, 128, 128, L, True)
    hg = chunk_mixer(_hgrn_heads, [qh, ih, (proj, 1024, 2), lb], 8, 128, 128, L, False,
                     final=((proj, 1024, 4), o_rev, hgrn_norm))
    ycat = jnp.concatenate([jnp.concatenate([att_c, att], axis=0), hg], axis=-1)
    return matmul(ycat, w_out, 1280, 1024)


def _token_shift(p, L, mu_prev, mu_next):
    def one(s):
        zero = jnp.zeros_like(s[:1])
        prev = jnp.concatenate([zero, s[:-1]], axis=0)
        nxt = jnp.concatenate([s[1:], zero], axis=0)
        return s + mu_prev * (prev - s) + mu_next * (nxt - s)
    return jnp.concatenate([one(p[:L]), one(p[L:])], axis=0)


def _odd_mixer(h, L, w_in, w_out, gla_gate_up_f, gla_gate_up_b, gla_gate_bias_f, gla_gate_bias_b, gla_norm,
               mu_prev, mu_next, w0_f, w0_b, w2_f, w2_b, a0, a2, g2, k_k, k_a, r_k, ln_w, ln_b):
    T = h.shape[0]
    GO = 3104
    cols = lambda a, b: w_in[:, a:b]
    w_big = jnp.concatenate([cols(0, 2048), cols(2080, 3104), cols(GO, GO + 3072), cols(GO + 3360, GO + 3616)], axis=1)
    w_small = jnp.concatenate([cols(2048, 2080), cols(GO + 3072, GO + 3360),
                               jnp.zeros((w_in.shape[0], 64), F32)], axis=1)
    pb = matmul(h, w_big, 1280, 640)
    ps = matmul(h, w_small, 1280, 384)
    def gla_inputs(up, row0, bias):
        up_rows = jnp.zeros((ps.shape[1], up.shape[1]), F32).at[row0:row0 + up.shape[0]].set(up)
        return [(pb, 512, 0), (pb, 512, 1), (pb, 1024, 1), (ps, ps.shape[1], 0), (up_rows,), (bias.reshape(1, -1),)]
    o_rev = chunk_mixer(_gla_heads, gla_inputs(gla_gate_up_b, 16, gla_gate_bias_b), 4, 128, 256, L, True)
    gla = chunk_mixer(_gla_heads, gla_inputs(gla_gate_up_f, 0, gla_gate_bias_f), 4, 128, 256, L, False,
                      final=((pb, 1024, 2), o_rev, gla_norm))
    def shift(p, a, b):
        return _token_shift(p, L, mu_prev[a:b], mu_next[a:b])
    rrkv = shift(pb[:, 3072:6144], 0, 3072)
    rr, rk, rv = rrkv[:, :1024], rrkv[:, 1024:2048], rrkv[:, 2048:]
    low = shift(ps[:, 32:320], 3072, 3360)
    wd_f, wd_b, ad = low[:, :96], low[:, 96:192], low[:, 192:288]
    gd = shift(pb[:, 6144:6400], 3360, 3616)
    a_sig = jax.nn.sigmoid(a0 + matmul(ad, a2, 1280, 1024))
    g_out = matmul(jax.nn.sigmoid(gd), g2, 1280, 1024)
    heads = lambda t: t.reshape(T, RWKV_HEADS, RWKV_N)
    kk = heads(rk * k_k)
    kk = kk * lax.rsqrt(jnp.sum(kk * kk, axis=-1, keepdims=True) + 1e-12)
    k_mod = heads(rk * (1.0 + (a_sig - 1.0) * k_a))
    r, v = heads(rr), heads(rv)
    b_vec = kk * heads(a_sig)
    hm = lambda t: jnp.moveaxis(t, 1, 0)
    ro = None
    for w0, wd, w2, rev in ((w0_f, wd_f, w2_f, False), (w0_b, wd_b, w2_b, True)):
        log_w = -jnp.exp(-jax.nn.softplus(-(w0 + matmul(jnp.tanh(wd), w2, 1280, 1024))) - 0.5)
        od = rwkv7_mix(hm(r), hm(k_mod), hm(v), hm(-kk), hm(b_vec), hm(heads(log_w)), L, rev)
        ro = od if ro is None else ro + od
    ro = jnp.moveaxis(ro, 0, 1)[L:]
    mu = jnp.mean(ro, axis=-1, keepdims=True)
    var = jnp.mean(jnp.square(ro - mu), axis=-1, keepdims=True)
    on = ((ro - mu) * lax.rsqrt(var + RWKV_LN_EPS)).reshape(T - L, -1) * ln_w + ln_b
    bonus = (jnp.sum(r[L:] * k_mod[L:] * r_k, axis=-1, keepdims=True) * v[L:]).reshape(T - L, -1)
    rw = (on + bonus) * g_out[L:]
    ycat = jnp.concatenate([gla[L:], rw.astype(BF16)], axis=-1)
    return matmul(ycat, w_out, 1024, 1024)


def kernel(x, c, ctx, c_ctx, hgrn_lb_logits, l0_ada_w, l0_ada_b, l0_norm_mix_pre, l0_norm_mix_post, l0_norm_ffn_pre, l0_norm_ffn_post, l0_w_in, l0_w_out, l0_attn_sink, l0_hgrn_norm, l0_ffn_w_gate, l0_ffn_w_up, l0_ffn_w_down, l1_ada_w, l1_ada_b, l1_norm_mix_pre, l1_norm_mix_post, l1_norm_ffn_pre, l1_norm_ffn_post, l1_w_in, l1_w_out, l1_gla_gate_up_f, l1_gla_gate_up_b, l1_gla_gate_bias_f, l1_gla_gate_bias_b, l1_gla_norm, l1_rwkv_mu_prev, l1_rwkv_mu_next, l1_rwkv_w0_f, l1_rwkv_w0_b, l1_rwkv_w2_f, l1_rwkv_w2_b, l1_rwkv_a0, l1_rwkv_a2, l1_rwkv_g2, l1_rwkv_k_k, l1_rwkv_k_a, l1_rwkv_r_k, l1_rwkv_ln_w, l1_rwkv_ln_b, l1_moe_router, l1_moe_w_gate, l1_moe_w_up, l1_moe_w_down):
    B, S, D = x.shape
    L = ctx.shape[1]
    assert B == 1
    T = L + S
    SH1, SC1, G1, SH2, SC2, G2 = range(6)
    xa = jnp.concatenate([ctx[0], x[0]], axis=0)
    hgrn_lb = jnp.cumsum(jax.nn.softmax(hgrn_lb_logits.astype(F32), axis=0), axis=0)
    m0 = _adaln(c, c_ctx, l0_ada_w, l0_ada_b)
    m1 = _adaln(c, c_ctx, l1_ada_w, l1_ada_b)

    h = normmod(xa, l0_norm_mix_pre, m0, SH1, SC1, L)
    y = _even_mixer(h, L, l0_w_in, l0_w_out, l0_attn_sink, l0_hgrn_norm, hgrn_lb[0])
    xa, h = resid_norm(xa, y, l0_norm_mix_post, m0, G1, L, nxt=(l0_norm_ffn_pre, m0, SH2, SC2))
    n_t = T // DENSE_TILE
    f = swiglu_ffn(h, l0_ffn_w_gate[None].astype(BF16), l0_ffn_w_up[None].astype(BF16),
                   l0_ffn_w_down[None].astype(BF16),
                   jnp.zeros((n_t,), jnp.int32), jnp.full((1,), n_t, jnp.int32), DENSE_TILE, 512)
    xa, h = resid_norm(xa, f, l0_norm_ffn_post, m0, G2, L, nxt=(l1_norm_mix_pre, m1, SH1, SC1))

    y = _odd_mixer(h, L, l1_w_in, l1_w_out, l1_gla_gate_up_f, l1_gla_gate_up_b, l1_gla_gate_bias_f,
                   l1_gla_gate_bias_b, l1_gla_norm, l1_rwkv_mu_prev, l1_rwkv_mu_next, l1_rwkv_w0_f, l1_rwkv_w0_b,
                   l1_rwkv_w2_f, l1_rwkv_w2_b, l1_rwkv_a0, l1_rwkv_a2, l1_rwkv_g2, l1_rwkv_k_k, l1_rwkv_k_a,
                   l1_rwkv_r_k, l1_rwkv_ln_w, l1_rwkv_ln_b)
    xl, h = resid_norm(xa[L:], y, l1_norm_mix_post, m1, G1, 0, nxt=(l1_norm_ffn_pre, m1, SH2, SC2), h_dtype=F32)
    router = jnp.concatenate([l1_moe_router, jnp.zeros((D, 128 - N_EXPERTS), F32)], axis=1)
    logits = matmul(h, router, 1024, 128)[:, :N_EXPERTS]
    src, gate_w, dest, tile_expert, n_used = _route(logits, MOE_TILE)
    hs = gather_rows(h, src, TOK_BLOCK)
    ys = swiglu_ffn(hs, l1_moe_w_gate.astype(BF16), l1_moe_w_up.astype(BF16), l1_moe_w_down.astype(BF16),
                    tile_expert, n_used, MOE_TILE, 1024)
    out = combine_resid(xl, ys, dest, gate_w, l1_norm_ffn_post, m1, G2)
    return out[None]
```

```python
import functools

import jax
import jax.numpy as jnp
from jax import lax
from jax.experimental import pallas as pl
from jax.experimental.pallas import tpu as pltpu

F32 = jnp.float32
BF16 = jnp.bfloat16

NORM_EPS = 1e-6
CHUNK = 32
TOK_BLOCK = 256
ATT_BLOCK = 128
WINDOW = 128
HEAD_DIM = 128
ATT_HEADS = 8
ATT_KV_HEADS = 2
ROPE_THETA = 10000.0
GRID_W = 64
RWKV_N = 64
RWKV_HEADS = 16
RWKV_LN_EPS = 64e-5
RWKV_HEAD_BLOCK = 4
GLA_GATE_NORMALIZER = 16.0
N_EXPERTS = 8
MOE_TILE = 512
DENSE_TILE = 1280
VMEM_LIMIT_BYTES = 56 * 1024 * 1024
LANES = 128
BF16_SUBLANES = 16


def _params(*sem):
    return pltpu.CompilerParams(dimension_semantics=sem, vmem_limit_bytes=VMEM_LIMIT_BYTES)


def _mm_kernel(x_ref, w_ref, o_ref, wbf_ref):
    @pl.when(pl.program_id(1) == 0)
    def _():
        wbf_ref[...] = w_ref[...].astype(BF16)

    o_ref[...] = jnp.dot(x_ref[...].astype(BF16), wbf_ref[...],
                         preferred_element_type=F32).astype(o_ref.dtype)


def matmul(x, w, tm, tn, out_dtype=F32):
    M, K = x.shape
    N = w.shape[1]
    assert M % tm == 0 and N % tn == 0, (M, tm, N, tn)
    return pl.pallas_call(
        _mm_kernel,
        grid=(N // tn, M // tm),
        in_specs=[pl.BlockSpec((tm, K), lambda j, i: (i, 0)),
                  pl.BlockSpec((K, tn), lambda j, i: (0, j))],
        out_specs=pl.BlockSpec((tm, tn), lambda j, i: (i, j)),
        out_shape=jax.ShapeDtypeStruct((M, N), out_dtype),
        scratch_shapes=[pltpu.VMEM((K, tn), BF16)],
        compiler_params=_params("arbitrary", "arbitrary"),
        name="matmul",
    )(x, w)


def _rms(x, gain):
    ms = jnp.mean(x * x, axis=-1, keepdims=True)
    return x * lax.rsqrt(ms + NORM_EPS) * gain


def _normmod_kernel(x_ref, gain_ref, sc_ref, sh_ref, h_ref):
    y = _rms(x_ref[...], gain_ref[...])
    h_ref[...] = (y * (1.0 + sc_ref[0]) + sh_ref[0]).astype(h_ref.dtype)


def _mod_spec(D, slot, n_ctx_tiles):
    return pl.BlockSpec((1, 1, D), lambda i: (jnp.where(i < n_ctx_tiles, 6, 0) + slot, 0, 0))


def normmod(x, gain, mods, sh_slot, sc_slot, n_ctx_rows):
    R, D = x.shape
    tr = TOK_BLOCK
    assert R % tr == 0 and n_ctx_rows % tr == 0
    nct = n_ctx_rows // tr
    return pl.pallas_call(
        _normmod_kernel,
        grid=(R // tr,),
        in_specs=[pl.BlockSpec((tr, D), lambda i: (i, 0)),
                  pl.BlockSpec((1, D), lambda i: (0, 0)),
                  _mod_spec(D, sc_slot, nct), _mod_spec(D, sh_slot, nct)],
        out_specs=pl.BlockSpec((tr, D), lambda i: (i, 0)),
        out_shape=jax.ShapeDtypeStruct((R, D), BF16),
        compiler_params=_params("arbitrary"),
        name="normmod",
    )(x, gain.reshape(1, D), mods, mods)


def _resid_kernel(x_ref, y_ref, gpost_ref, g_ref, *rest, with_next):
    xn = x_ref[...] + g_ref[0] * _rms(y_ref[...], gpost_ref[...])
    if with_next:
        gpre_ref, sc_ref, sh_ref, xo_ref, h_ref = rest
        xo_ref[...] = xn
        h_ref[...] = (_rms(xn, gpre_ref[...]) * (1.0 + sc_ref[0]) + sh_ref[0]).astype(h_ref.dtype)
    else:
        (xo_ref,) = rest
        xo_ref[...] = xn


def resid_norm(x, y, gain_post, mods, g_slot, n_ctx_rows, nxt=None, h_dtype=BF16):
    R, D = x.shape
    tr = TOK_BLOCK
    assert R % tr == 0 and n_ctx_rows % tr == 0
    nct = n_ctx_rows // tr
    row = pl.BlockSpec((tr, D), lambda i: (i, 0))
    vec = pl.BlockSpec((1, D), lambda i: (0, 0))
    in_specs = [row, row, vec, _mod_spec(D, g_slot, nct)]
    args = [x, y, gain_post.reshape(1, D), mods]
    out_specs = [row]
    out_shape = [jax.ShapeDtypeStruct((R, D), F32)]
    if nxt is not None:
        gain_pre, mods_n, sh_slot, sc_slot = nxt
        in_specs += [vec, _mod_spec(D, sc_slot, nct), _mod_spec(D, sh_slot, nct)]
        args += [gain_pre.reshape(1, D), mods_n, mods_n]
        out_specs.append(row)
        out_shape.append(jax.ShapeDtypeStruct((R, D), h_dtype))
    out = pl.pallas_call(
        functools.partial(_resid_kernel, with_next=nxt is not None),
        grid=(R // tr,),
        in_specs=in_specs, out_specs=out_specs, out_shape=out_shape,
        compiler_params=_params("arbitrary"),
        name="resid_norm",
    )(*args)
    return out if nxt is not None else out[0]


def _ffn_kernel(te_ref, nu_ref, x_ref, wg_ref, wu_ref, wd_ref, o_ref):
    i, j = pl.program_id(0), pl.program_id(1)

    @pl.when(i < nu_ref[0])
    def _():
        x = x_ref[...]
        g = jnp.dot(x, wg_ref[0], preferred_element_type=F32)
        u = jnp.dot(x, wu_ref[0], preferred_element_type=F32)
        hid = (g * jax.nn.sigmoid(g) * u).astype(BF16)
        part = jnp.dot(hid, wd_ref[0], preferred_element_type=F32)

        @pl.when(j == 0)
        def _():
            o_ref[...] = part

        @pl.when(j > 0)
        def _():
            o_ref[...] += part

    @pl.when(jnp.logical_and(i >= nu_ref[0], j == 0))
    def _():
        o_ref[...] = jnp.zeros_like(o_ref)


def swiglu_ffn(x, w_gate, w_up, w_down, tile_expert, n_used, tm, tf):
    R, D = x.shape
    E, _, F = w_gate.shape
    assert R % tm == 0 and F % tf == 0 and w_gate.dtype == BF16 and x.dtype == BF16
    nf = F // tf

    def fblk(i, j, te, nu):
        return jnp.where(i < nu[0], j, nf - 1)

    in_specs = [pl.BlockSpec((tm, D), lambda i, j, te, nu: (i, 0)),
                pl.BlockSpec((1, D, tf), lambda i, j, te, nu: (te[i], 0, fblk(i, j, te, nu))),
                pl.BlockSpec((1, D, tf), lambda i, j, te, nu: (te[i], 0, fblk(i, j, te, nu))),
                pl.BlockSpec((1, tf, D), lambda i, j, te, nu: (te[i], fblk(i, j, te, nu), 0))]
    args = [x, w_gate, w_up, w_down]
    return pl.pallas_call(
        _ffn_kernel,
        grid_spec=pltpu.PrefetchScalarGridSpec(
            num_scalar_prefetch=2,
            grid=(R // tm, nf),
            in_specs=in_specs,
            out_specs=pl.BlockSpec((tm, D), lambda i, j, te, nu: (i, 0)),
        ),
        out_shape=jax.ShapeDtypeStruct((R, D), F32),
        compiler_params=_params("arbitrary", "arbitrary"),
        name="swiglu_ffn",
    )(tile_expert, n_used, *args)


def _rope(x, c, sg):
    lane = lax.broadcasted_iota(jnp.int32, x.shape, 1)
    first_half = jnp.bitwise_and(lane, HEAD_DIM // 2 - 1) < HEAD_DIM // 4
    partner = jnp.where(first_half, pltpu.roll(x, HEAD_DIM - HEAD_DIM // 4, 1), pltpu.roll(x, HEAD_DIM // 4, 1))
    return x * c + partner * sg


def _attn_kernel(sink_ref, q_ref, *rest, has_window, seq_len):
    if has_window:
        (kp_ref, ko_ref, kn_ref, vp_ref, vo_ref, vn_ref, cp_ref, co_ref, cn_ref, sp_ref, so_ref, sn_ref,
         kc_ref, vc_ref, o_ref) = rest
    else:
        kc_ref, vc_ref, o_ref = rest
    i = pl.program_id(0)
    BQ = q_ref.shape[0]
    L = kc_ref.shape[0]
    G = ATT_KV_HEADS
    R = ATT_HEADS // G
    n_win = 3 * BQ if has_window else 0
    nk = n_win + L
    if has_window:
        row = lax.broadcasted_iota(jnp.int32, (R * BQ, nk), 0)
        col = lax.broadcasted_iota(jnp.int32, (R * BQ, nk), 1)
        qi = jnp.bitwise_and(row, BQ - 1)
        kpos = i * BQ + col - BQ
        in_band = jnp.abs(col - BQ - qi) <= WINDOW
        in_seq = jnp.logical_and(kpos >= 0, kpos < seq_len)
        valid = jnp.logical_or(col >= n_win, jnp.logical_and(in_band, in_seq))
    for g in range(G):
        cs = slice(g * HEAD_DIM, (g + 1) * HEAD_DIM)
        q_heads = [q_ref[:, (g * R + r) * HEAD_DIM:(g * R + r + 1) * HEAD_DIM] for r in range(R)]
        if has_window:
            q_heads = [_rope(q, co_ref[...], so_ref[...]) for q in q_heads]
            k_all = jnp.concatenate([_rope(kp_ref[:, cs], cp_ref[...], sp_ref[...]),
                                     _rope(ko_ref[:, cs], co_ref[...], so_ref[...]),
                                     _rope(kn_ref[:, cs], cn_ref[...], sn_ref[...]), kc_ref[:, cs]], axis=0)
            v_all = jnp.concatenate([vp_ref[:, cs], vo_ref[:, cs], vn_ref[:, cs], vc_ref[:, cs]], axis=0)
        else:
            k_all, v_all = kc_ref[:, cs], vc_ref[:, cs]
        q_g = (jnp.concatenate(q_heads, axis=0) * (HEAD_DIM ** -0.5)).astype(BF16)
        s = lax.dot_general(q_g, k_all.astype(BF16), (((1,), (1,)), ((), ())),
                            preferred_element_type=F32)
        if has_window:
            s = jnp.where(valid, s, -jnp.inf)
        sink = jnp.concatenate([jnp.full((BQ, 1), sink_ref[g * R + r], F32) for r in range(R)], axis=0)
        m = jnp.maximum(sink, jnp.max(s, axis=-1, keepdims=True))
        p = jnp.exp(s - m)
        denom = jnp.exp(sink - m) + jnp.sum(p, axis=-1, keepdims=True)
        o = jnp.dot(p.astype(BF16), v_all.astype(BF16), preferred_element_type=F32) / denom
        for r in range(R):
            h = g * R + r
            o_ref[:, h * HEAD_DIM:(h + 1) * HEAD_DIM] = o[r * BQ:(r + 1) * BQ].astype(o_ref.dtype)


def sink_attention(proj, q_col, k_col, v_col, L, sink, rope):
    QW, KW = ATT_HEADS * HEAD_DIM, ATT_KV_HEADS * HEAD_DIM
    has_window = rope is not None
    S = proj.shape[0] - L if has_window else L
    BQ = ATT_BLOCK if has_window else L
    nb = S // BQ
    assert S % BQ == 0 and L % BQ == 0
    r0 = L // BQ if has_window else 0
    in_specs = [pl.BlockSpec(memory_space=pltpu.SMEM), pl.BlockSpec((BQ, QW), lambda i: (r0 + i, q_col))]
    args = [sink, proj]
    if has_window:
        prev = lambda i: jnp.maximum(i - 1, 0)
        nxt = lambda i: jnp.minimum(i + 1, nb - 1)
        for col in (k_col, v_col):
            in_specs += [pl.BlockSpec((BQ, KW), lambda i, col=col: (r0 + prev(i), col)),
                         pl.BlockSpec((BQ, KW), lambda i, col=col: (r0 + i, col)),
                         pl.BlockSpec((BQ, KW), lambda i, col=col: (r0 + nxt(i), col))]
            args += [proj] * 3
        for tab in rope:
            in_specs += [pl.BlockSpec((BQ, HEAD_DIM), lambda i: (prev(i), 0)),
                         pl.BlockSpec((BQ, HEAD_DIM), lambda i: (i, 0)),
                         pl.BlockSpec((BQ, HEAD_DIM), lambda i: (nxt(i), 0))]
            args += [tab] * 3
    in_specs += [pl.BlockSpec((L, KW), lambda i: (0, k_col)), pl.BlockSpec((L, KW), lambda i: (0, v_col))]
    args += [proj, proj]
    return pl.pallas_call(
        functools.partial(_attn_kernel, has_window=has_window, seq_len=S),
        grid=(nb,),
        in_specs=in_specs,
        out_specs=pl.BlockSpec((BQ, QW), lambda i: (i, 0)),
        out_shape=jax.ShapeDtypeStruct((S, QW), BF16),
        compiler_params=_params("arbitrary"),
        name="sink_attention",
    )(*args)


def _rope_tables(n_tokens):
    n_freq = HEAD_DIM // 4
    t = jnp.arange(n_tokens)
    row = (t // GRID_W).astype(F32)
    col = (t % GRID_W).astype(F32)
    inv = ROPE_THETA ** (-jnp.arange(n_freq, dtype=F32) / n_freq)
    ar, ac = row[:, None] * inv, col[:, None] * inv
    c = jnp.concatenate([jnp.cos(ar), jnp.cos(ar), jnp.cos(ac), jnp.cos(ac)], axis=1)
    sg = jnp.concatenate([-jnp.sin(ar), jnp.sin(ar), -jnp.sin(ac), jnp.sin(ac)], axis=1)
    return c, sg


CHUNK_SHIFT = CHUNK.bit_length() - 1
assert 1 << CHUNK_SHIFT == CHUNK


def _chunk_masks(tb, reverse, strict):
    r = lax.broadcasted_iota(jnp.int32, (tb, tb), 0)
    c = lax.broadcasted_iota(jnp.int32, (tb, tb), 1)
    same = jnp.right_shift(r, CHUNK_SHIFT) == jnp.right_shift(c, CHUNK_SHIFT)
    if reverse:
        tri = (c > r) if strict else (c >= r)
    else:
        tri = (c < r) if strict else (c <= r)
    return same, jnp.logical_and(same, tri)


def _chunk_sums(ld, same, tri):
    tb = ld.shape[0]
    sel = jnp.concatenate([jnp.where(tri, 1.0, 0.0), jnp.where(same, 1.0, 0.0)], axis=0).astype(BF16)
    hi = ld.astype(BF16)
    rest = ld - hi.astype(F32)
    mid = rest.astype(BF16)
    lo = (rest - mid.astype(F32)).astype(BF16)
    dot = lambda p: jnp.dot(sel, p, preferred_element_type=F32)
    g = (dot(lo) + dot(mid)) + dot(hi)
    return g[:tb], g[tb:]


def _dot_tn(a, b):
    return lax.dot_general(a, b, (((0,), (0,)), ((), ())), preferred_element_type=F32)


def _dot_nt(a, b):
    return lax.dot_general(a, b, (((1,), (1,)), ((), ())), preferred_element_type=F32)


def _chunkrec_core(load_head, n_heads, tb, V, reverse, st_ref, dst_ref):
    @pl.when(pl.program_id(0) == 0)
    def _():
        st_ref[...] = jnp.zeros_like(st_ref)

    same, tri = _chunk_masks(tb, reverse, strict=False)
    per_head = []
    for h in range(n_heads):
        q, k, v, ld = load_head(h)
        g_cum, g_tot = _chunk_sums(ld, same, tri)
        q_dec = (q * jnp.exp(g_cum)).astype(BF16)
        k_inv = (k * jnp.exp(-g_cum)).astype(BF16)
        k_tail = k * jnp.exp(g_tot - g_cum)
        a = jnp.where(tri, _dot_nt(q_dec, k_inv), 0.0)
        o_intra = jnp.dot(a.astype(BF16), v.astype(BF16), preferred_element_type=F32)
        per_head.append((q_dec, k_tail, v, g_tot, o_intra))
    n_chunks = tb // CHUNK
    order = range(n_chunks - 1, -1, -1) if reverse else range(n_chunks)
    sts = [st_ref[h] for h in range(n_heads)]
    for c in order:
        sl = slice(c * CHUNK, (c + 1) * CHUNK)
        for h in range(n_heads):
            q_dec, k_tail, v, g_tot, o_intra = per_head[h]
            st = sts[h]
            dst_ref[sl, h * V:(h + 1) * V] = o_intra[sl] + _dot_nt(q_dec[sl], st.astype(BF16))
            d = jnp.exp(g_tot[c * CHUNK:c * CHUNK + 1, :])
            sts[h] = st * d + _dot_tn(v[sl], k_tail[sl])
    for h in range(n_heads):
        st_ref[h] = sts[h]


def _gated_head_norm(o, gain, gate_raw):
    y = o * lax.rsqrt(jnp.mean(o * o, axis=-1, keepdims=True) + NORM_EPS) * gain
    return y * (gate_raw * jax.nn.sigmoid(gate_raw))


def _chunk_mixer_kernel(*refs, reverse, n_heads, K, V, load_heads):
    if reverse:
        *in_refs, o_ref, st_ref = refs
        dst_ref = o_ref
    else:
        *in_refs, gate_ref, orev_ref, gain_ref, o_ref, st_ref, dst_ref = refs
    tb = o_ref.shape[0]
    _chunkrec_core(load_heads(*in_refs), n_heads, tb, V, reverse, st_ref, dst_ref)
    if not reverse:
        for h in range(n_heads):
            vs = slice(h * V, (h + 1) * V)
            o_ref[:, vs] = _gated_head_norm(dst_ref[:, vs] + orev_ref[:, vs], gain_ref[...],
                                            gate_ref[:, vs]).astype(o_ref.dtype)


def _hgrn_heads(qh_ref, ih_ref, fr_ref, lb_ref):
    def load(h):
        cs = slice(h * 128, (h + 1) * 128)
        lb = lb_ref[:, cs]
        f = lb + (1.0 - lb) * jax.nn.sigmoid(fr_ref[:, cs])
        qh = qh_ref[:, cs]
        return qh * jax.nn.sigmoid(qh), 1.0 - f, ih_ref[:, cs], jnp.log(f)
    return load


def _gla_heads(gq_ref, gk_ref, gv_ref, gd_ref, up_ref, bias_ref):
    x = jnp.dot(gd_ref[...].astype(BF16), up_ref[...].astype(BF16), preferred_element_type=F32) + bias_ref[...]
    lg = (jnp.minimum(x, 0.0) - jnp.log(1.0 + jnp.exp(-jnp.abs(x)))) * (1.0 / GLA_GATE_NORMALIZER)

    def load(h):
        ks, vs = slice(h * 128, (h + 1) * 128), slice(h * 256, (h + 1) * 256)
        return gq_ref[:, ks] * (128 ** -0.5), gk_ref[:, ks], gv_ref[:, vs], lg[:, ks]
    return load


def _seq_block_index(nblk, nctx, reverse):
    if not reverse:
        return lambda t: t
    return lambda t: jnp.where(t < nctx, nctx - 1 - t, nblk - 1 - (t - nctx))


def chunk_mixer(load_heads, inputs, n_heads, K, V, n_ctx_rows, reverse, final=None):
    T = inputs[0][0].shape[0]
    tb = TOK_BLOCK
    assert T % tb == 0 and n_ctx_rows % tb == 0
    nblk, nctx = T // tb, n_ctx_rows // tb
    blk = _seq_block_index(nblk, nctx, reverse)

    def spec(item):
        if len(item) == 1:
            return pl.BlockSpec(item[0].shape, lambda t: (0, 0))
        _, width, cb = item
        return pl.BlockSpec((tb, width), lambda t: (blk(t), cb))

    HV = n_heads * V
    ospec = pl.BlockSpec((tb, HV), lambda t: (blk(t), 0))
    scratch = [pltpu.VMEM((n_heads, V, K), F32)]
    if not reverse:
        gate, o_rev, gain = final
        inputs = list(inputs) + [gate, (o_rev, HV, 0), (gain.reshape(1, V),)]
        scratch.append(pltpu.VMEM((tb, HV), F32))
    return pl.pallas_call(
        functools.partial(_chunk_mixer_kernel, reverse=reverse, n_heads=n_heads, K=K, V=V, load_heads=load_heads),
        grid=(nblk,),
        in_specs=[spec(it) for it in inputs],
        out_specs=ospec,
        out_shape=jax.ShapeDtypeStruct((T, HV), F32 if reverse else BF16),
        scratch_shapes=scratch,
        compiler_params=_params("arbitrary"),
        name="chunk_mixer",
    )(*[it[0] for it in inputs])


def _mm_bf(a, b):
    return jnp.dot(a.astype(BF16), b.astype(BF16), preferred_element_type=F32)


def _softplus(x):
    return jnp.maximum(x, 0.0) + jnp.log(1.0 + jnp.exp(-jnp.abs(x)))


def _head_sums(x, bd_ref):
    hi = x.astype(BF16)
    lo = (x - hi.astype(F32)).astype(BF16)
    bd = bd_ref[...]
    return jnp.dot(lo, bd, preferred_element_type=F32) + jnp.dot(hi, bd, preferred_element_type=F32)


def _rwkv_prep_kernel(p_ref, pp_ref, pn_ref, g_ref, gp_ref, gn_ref, s_ref, sp_ref, sn_ref,
                      mup_ref, mun_ref, mugp_ref, mugn_ref, musp_ref, musn_ref,
                      w2f_ref, w2b_ref, a2_ref, g2_ref, w0f_ref, w0b_ref, a0_ref, kk_ref, ka_ref, rk_ref, bd_ref,
                      r_out, k_out, v_out, kk_out, b_out, lwf_out, lwb_out, go_out, bon_out, *, n_ctx_blocks):
    t, nblk = pl.program_id(0), pl.num_programs(0)
    tb = p_ref.shape[0]
    C = RWKV_HEADS * RWKV_N
    keep_prev = jnp.where(jnp.logical_or(t == 0, t == n_ctx_blocks), 0.0, 1.0)
    keep_next = jnp.where(jnp.logical_or(t == n_ctx_blocks - 1, t == nblk - 1), 0.0, 1.0)

    def shifted(x_ref, xp_ref, xn_ref, mu_p_ref, mu_n_ref):
        x = x_ref[...]
        rows = lax.broadcasted_iota(jnp.int32, x.shape, 0)
        prev = jnp.where(rows == 0, xp_ref[7:8, :] * keep_prev, pltpu.roll(x, 1, 0))
        nxt = jnp.where(rows == tb - 1, xn_ref[0:1, :] * keep_next, pltpu.roll(x, tb - 1, 0))
        return x + mu_p_ref[...] * (prev - x) + mu_n_ref[...] * (nxt - x)

    rkv = shifted(p_ref, pp_ref, pn_ref, mup_ref, mun_ref)
    rr, rk, rv = rkv[:, :C], rkv[:, C:2 * C], rkv[:, 2 * C:]
    low = shifted(s_ref, sp_ref, sn_ref, musp_ref, musn_ref)
    gd = shifted(g_ref, gp_ref, gn_ref, mugp_ref, mugn_ref)
    tl = jnp.tanh(low).astype(BF16)
    lwf_out[...] = -jnp.exp(-_softplus(-(w0f_ref[...] + jnp.dot(tl, w2f_ref[...], preferred_element_type=F32))) - 0.5)
    lwb_out[...] = -jnp.exp(-_softplus(-(w0b_ref[...] + jnp.dot(tl, w2b_ref[...], preferred_element_type=F32))) - 0.5)
    a_sig = jax.nn.sigmoid(a0_ref[...] + jnp.dot(low.astype(BF16), a2_ref[...], preferred_element_type=F32))
    go_out[...] = jnp.dot(jax.nn.sigmoid(gd).astype(BF16), g2_ref[...], preferred_element_type=F32)
    kk = rk * kk_ref[...]
    kk = kk * lax.rsqrt(_head_sums(kk * kk, bd_ref) + 1e-12)
    k_mod = rk * (1.0 + (a_sig - 1.0) * ka_ref[...])
    r_out[...] = rr
    k_out[...] = k_mod
    v_out[...] = rv
    kk_out[...] = kk
    b_out[...] = kk * a_sig
    bon_out[...] = _head_sums(rr * k_mod * rk_ref[...], bd_ref) * rv


def _rwkv_pre_kernel(r_ref, k_ref, v_ref, kk_ref, b_ref, lw_ref,
                     qp_ref, ol_ref, plt_ref, zt_ref, gt_ref, *, reverse):
    N = RWKV_N
    tb = r_ref.shape[0]
    hb = r_ref.shape[1] // N
    same, tri_incl = _chunk_masks(tb, reverse, strict=False)
    _, tri_strict = _chunk_masks(tb, reverse, strict=True)
    xs, lps, rest = [], [], []
    for h in range(hb):
        hs = slice(h * N, (h + 1) * N)
        r, k, v, b, lw = (ref[:, hs] for ref in (r_ref, k_ref, v_ref, b_ref, lw_ref))
        a = -kk_ref[:, hs]
        g_cum, g_tot = _chunk_sums(lw, same, tri_incl)
        gt_ref[:, hs] = g_tot
        e_neg = jnp.exp(-g_cum)
        e_tail = jnp.exp(g_tot - g_cum)
        a_t = (a * jnp.exp(g_cum - lw)).astype(BF16)
        r_t = (r * jnp.exp(g_cum)).astype(BF16)
        b_t = (b * e_neg).astype(BF16)
        k_t = (k * e_neg).astype(BF16)
        a_ab = jnp.where(tri_strict, _dot_nt(a_t, b_t), 0.0)
        a_ak = jnp.where(tri_strict, _dot_nt(a_t, k_t), 0.0)
        a_rb = jnp.where(tri_incl, _dot_nt(r_t, b_t), 0.0)
        a_rk = jnp.where(tri_incl, _dot_nt(r_t, k_t), 0.0)
        xs.append(jnp.concatenate([a_t.astype(F32), _mm_bf(a_ak, v)], axis=1))
        lps.append(a_ab)
        rest.append((v, b * e_tail, k * e_tail, a_rb,
                     jnp.concatenate([r_t.astype(F32), _mm_bf(a_rk, v)], axis=1)))
    for j in range(CHUNK_SHIFT):
        xs = [x + _mm_bf(lp, x) for x, lp in zip(xs, lps)]
        if j < CHUNK_SHIFT - 1:
            lps = [_mm_bf(lp, lp) for lp in lps]
    for h in range(hb):
        v, b_h, k_h, a_rb, qo0 = rest[h]
        x = xs[h]
        qo = qo0 + _mm_bf(a_rb, x)
        qp_ref[:, h * N:(h + 1) * N] = qo[:, :N]
        ol_ref[:, h * N:(h + 1) * N] = qo[:, N:]
        w, uloc = x[:, :N], x[:, N:]
        for c in range(tb // CHUNK):
            sl = slice(c * CHUNK, (c + 1) * CHUNK)
            plt_ref[h, c] = _dot_tn(w[sl], b_h[sl])
            zt_ref[h, c] = _dot_tn(uloc[sl], b_h[sl]) + _dot_tn(v[sl], k_h[sl])


def _rwkv_scan_kernel(qp_ref, ol_ref, plt_ref, zt_ref, gt_ref, *rest, reverse):
    if reverse:
        o_ref, st_ref = rest
        dst_ref = o_ref
    else:
        orev_ref, go_ref, bon_ref, lnw_ref, lnb_ref, o_ref, st_ref, dst_ref = rest

    @pl.when(pl.program_id(0) == 0)
    def _():
        st_ref[...] = jnp.zeros_like(st_ref)

    N = RWKV_N
    tb = qp_ref.shape[0]
    n_heads = qp_ref.shape[1] // N
    n_chunks = tb // CHUNK
    order = range(n_chunks - 1, -1, -1) if reverse else range(n_chunks)
    sts = [st_ref[h] for h in range(n_heads)]
    for c in order:
        sl = slice(c * CHUNK, (c + 1) * CHUNK)
        for h in range(n_heads):
            hs = slice(h * N, (h + 1) * N)
            st = sts[h]
            dst_ref[sl, hs] = _dot_nt(qp_ref[sl, hs].astype(BF16), st.astype(BF16)) + ol_ref[sl, hs]
            d = jnp.exp(gt_ref[c * CHUNK:c * CHUNK + 1, hs])
            sts[h] = st * d + _mm_bf(st, plt_ref[h, c]) + zt_ref[h, c]
    for h in range(n_heads):
        st_ref[h] = sts[h]
    if not reverse:
        for h in range(n_heads):
            hs = slice(h * N, (h + 1) * N)
            o = dst_ref[:, hs] + orev_ref[:, hs]
            mu = jnp.mean(o, axis=-1, keepdims=True)
            var = jnp.mean(jnp.square(o - mu), axis=-1, keepdims=True)
            on = (o - mu) * lax.rsqrt(var + RWKV_LN_EPS) * lnw_ref[:, hs] + lnb_ref[:, hs]
            o_ref[:, hs] = ((on + bon_ref[:, hs]) * go_ref[:, hs]).astype(o_ref.dtype)


def rwkv7_branch(pb, ps, rkv_col, gd_col, n_ctx_rows, mu_prev, mu_next, low_cols, w0_f, w0_b, w2_f, w2_b,
                 a0, a2, g2, k_k, k_a, r_k, ln_w, ln_b):
    T = pb.shape[0]
    H, N = RWKV_HEADS, RWKV_N
    C = H * N
    tb = TOK_BLOCK
    assert T % tb == 0 and n_ctx_rows % tb == 0
    nblk, nctx = T // tb, n_ctx_rows // tb
    ncb = tb // CHUNK
    GW, SW, R = g2.shape[0], ps.shape[1], w2_f.shape[0]
    wf_col, wb_col, ad_col = low_cols
    s0 = 3 * C
    row = lambda v: v.reshape(1, -1)

    def padded_rows(w, r0):
        return jnp.zeros((SW, C), F32).at[r0:r0 + w.shape[0]].set(w).astype(BF16)

    def mu_low(mu):
        return jnp.zeros((1, SW), F32).at[0, wf_col:wf_col + 3 * R].set(mu[s0:s0 + 3 * R])

    bd = (jnp.arange(C)[:, None] // N == jnp.arange(C)[None, :] // N).astype(BF16)
    r8 = tb // 8

    def halo(width, col):
        return (pl.BlockSpec((tb, width), lambda t: (t, col)),
                pl.BlockSpec((8, width), lambda t: (jnp.maximum(t * r8 - 1, 0), col)),
                pl.BlockSpec((8, width), lambda t: (jnp.minimum((t + 1) * r8, T // 8 - 1), col)))

    whole = lambda a: pl.BlockSpec(a.shape, lambda t: (0, 0))
    consts = [row(mu_prev[:s0]), row(mu_next[:s0]), row(mu_prev[s0 + 3 * R:]), row(mu_next[s0 + 3 * R:]),
              mu_low(mu_prev), mu_low(mu_next),
              padded_rows(w2_f, wf_col), padded_rows(w2_b, wb_col), padded_rows(a2, ad_col), g2.astype(BF16),
              row(w0_f), row(w0_b), row(a0), row(k_k), row(k_a), row(r_k), bd]
    tokC = pl.BlockSpec((tb, C), lambda t: (t, 0))
    shapeC = jax.ShapeDtypeStruct((T, C), F32)
    r, k, v, kk, b, lw_f, lw_b, g_out, bonus = pl.pallas_call(
        functools.partial(_rwkv_prep_kernel, n_ctx_blocks=nctx),
        grid=(nblk,),
        in_specs=[*halo(3 * C, rkv_col), *halo(GW, gd_col), *halo(SW, 0), *[whole(a) for a in consts]],
        out_specs=[tokC] * 9,
        out_shape=[shapeC] * 9,
        compiler_params=_params("arbitrary"),
        name="rwkv_prep",
    )(pb, pb, pb, pb, pb, pb, ps, ps, ps, *consts)

    hb = RWKV_HEAD_BLOCK
    tok = pl.BlockSpec((tb, hb * N), lambda h, t: (t, h))
    mat = pl.BlockSpec((hb, ncb, N, N), lambda h, t: (h, t, 0, 0))
    mat_shape = jax.ShapeDtypeStruct((H, T // CHUNK, N, N), F32)
    o_rev = None
    for lw, reverse in ((lw_b, True), (lw_f, False)):
        qp, ol, plt, zt, gt = pl.pallas_call(
            functools.partial(_rwkv_pre_kernel, reverse=reverse),
            grid=(H // hb, nblk),
            in_specs=[tok] * 6,
            out_specs=[tok, tok, mat, mat, tok],
            out_shape=[shapeC, shapeC, mat_shape, mat_shape, shapeC],
            compiler_params=_params("arbitrary", "arbitrary"),
            name="rwkv_pre",
        )(r, k, v, kk, b, lw)
        blk = _seq_block_index(nblk, nctx, reverse)
        tok_all = pl.BlockSpec((tb, C), lambda t: (blk(t), 0))
        mat_all = pl.BlockSpec((H, ncb, N, N), lambda t: (0, blk(t), 0, 0))
        in_specs = [tok_all, tok_all, mat_all, mat_all, tok_all]
        args = [qp, ol, plt, zt, gt]
        scratch = [pltpu.VMEM((H, N, N), F32)]
        if not reverse:
            in_specs += [tok_all, tok_all, tok_all, whole(row(ln_w)), whole(row(ln_b))]
            args += [o_rev, g_out, bonus, row(ln_w), row(ln_b)]
            scratch.append(pltpu.VMEM((tb, C), F32))
        out = pl.pallas_call(
            functools.partial(_rwkv_scan_kernel, reverse=reverse),
            grid=(nblk,),
            in_specs=in_specs,
            out_specs=tok_all,
            out_shape=jax.ShapeDtypeStruct((T, C), F32 if reverse else BF16),
            scratch_shapes=scratch,
            compiler_params=_params("arbitrary"),
            name="rwkv_scan",
        )(*args)
        if reverse:
            o_rev = out
    return out


def _row_copy(src_hbm, src_row, dst_ref, dst_row, sem):
    return pltpu.make_async_copy(src_hbm.at[pl.ds(src_row, 1)], dst_ref.at[pl.ds(dst_row, 1)], sem)


def _gather_kernel(src_ref, nxt_ref, x_hbm, o_ref, buf, sems):
    i, n_steps = pl.program_id(0), pl.num_programs(0)
    n = o_ref.shape[0]

    def issue(idx_ref, slot):
        def body(r, carry):
            _row_copy(x_hbm, idx_ref[0, 0, r], buf.at[slot], r, sems.at[slot]).start()
            return carry
        lax.fori_loop(0, n, body, 0)

    @pl.when(i == 0)
    def _():
        issue(src_ref, 0)

    @pl.when(i + 1 < n_steps)
    def _():
        issue(nxt_ref, (i + 1) % 2)

    slot = i % 2

    def wait(r, carry):
        _row_copy(x_hbm, 0, buf.at[slot], r, sems.at[slot]).wait()
        return carry

    lax.fori_loop(0, n, wait, 0)
    o_ref[...] = buf[slot].astype(o_ref.dtype)


def gather_rows(x, src, tg):
    S, D = x.shape
    P = src.shape[0]
    sub = D // LANES
    assert P % tg == 0 and x.dtype == BF16 and sub == BF16_SUBLANES
    n_steps = P // tg
    src3 = src.reshape(n_steps, 1, tg)
    out = pl.pallas_call(
        _gather_kernel,
        grid=(n_steps,),
        in_specs=[pl.BlockSpec((1, 1, tg), lambda i: (i, 0, 0), memory_space=pltpu.SMEM),
                  pl.BlockSpec((1, 1, tg), lambda i: (jnp.minimum(i + 1, n_steps - 1), 0, 0),
                               memory_space=pltpu.SMEM),
                  pl.BlockSpec(memory_space=pl.ANY)],
        out_specs=pl.BlockSpec((tg, sub, LANES), lambda i: (i, 0, 0)),
        out_shape=jax.ShapeDtypeStruct((P, sub, LANES), BF16),
        scratch_shapes=[pltpu.VMEM((2, tg, sub, LANES), BF16), pltpu.SemaphoreType.DMA((2,))],
        compiler_params=_params("arbitrary"),
        name="gather_rows",
    )(src3, src3, x.reshape(S, sub, LANES))
    return out.reshape(P, D)


def _combine_kernel(pos_ref, nxt_ref, x_ref, w_ref, ys_hbm, gpost_ref, g_ref, o_ref, buf, sems):
    i, n_steps = pl.program_id(0), pl.num_programs(0)
    n = x_ref.shape[0]

    def issue(idx_ref, slot):
        def body(r, carry):
            for k in range(2):
                _row_copy(ys_hbm, idx_ref[0, 0, 2 * r + k], buf.at[slot, k], r, sems.at[slot, k]).start()
            return carry
        lax.fori_loop(0, n, body, 0)

    @pl.when(i == 0)
    def _():
        issue(pos_ref, 0)

    @pl.when(i + 1 < n_steps)
    def _():
        issue(nxt_ref, (i + 1) % 2)

    slot = i % 2

    def wait(r, carry):
        for k in range(2):
            _row_copy(ys_hbm, 0, buf.at[slot, k], r, sems.at[slot, k]).wait()
        return carry

    lax.fori_loop(0, n, wait, 0)
    w = w_ref[...]
    f = w[:, 0:1] * buf[slot, 0] + w[:, 1:2] * buf[slot, 1]
    o_ref[...] = x_ref[...] + g_ref[0] * _rms(f, gpost_ref[...])


def combine_resid(x, ys, pos, weights, gain_post, mods, g_slot):
    S, D = x.shape
    tc = TOK_BLOCK
    assert S % tc == 0
    n_steps = S // tc
    row = pl.BlockSpec((tc, D), lambda i: (i, 0))
    pos3 = pos.reshape(n_steps, 1, 2 * tc)
    return pl.pallas_call(
        _combine_kernel,
        grid=(n_steps,),
        in_specs=[pl.BlockSpec((1, 1, 2 * tc), lambda i: (i, 0, 0), memory_space=pltpu.SMEM),
                  pl.BlockSpec((1, 1, 2 * tc), lambda i: (jnp.minimum(i + 1, n_steps - 1), 0, 0),
                               memory_space=pltpu.SMEM),
                  row, pl.BlockSpec((tc, 2), lambda i: (i, 0)), pl.BlockSpec(memory_space=pl.ANY),
                  pl.BlockSpec((1, D), lambda i: (0, 0)), _mod_spec(D, g_slot, 0)],
        out_specs=row,
        out_shape=jax.ShapeDtypeStruct((S, D), F32),
        scratch_shapes=[pltpu.VMEM((2, 2, tc, D), F32), pltpu.SemaphoreType.DMA((2, 2))],
        compiler_params=_params("arbitrary"),
        name="combine_resid",
    )(pos3, pos3, x, weights, ys, gain_post.reshape(1, D), mods)


def _route(logits, tm):
    S = logits.shape[0]
    top_val, top_idx = lax.top_k(logits, 2)
    weights = jax.nn.softmax(top_val, axis=-1)
    e_flat = top_idx.reshape(-1)
    onehot = (e_flat[:, None] == jnp.arange(N_EXPERTS)[None, :]).astype(jnp.int32)
    rank = jnp.take_along_axis(jnp.cumsum(onehot, axis=0), e_flat[:, None], axis=1)[:, 0] - 1
    counts = jnp.sum(onehot, axis=0)
    tiles_per = (counts + tm - 1) // tm
    tile_end = jnp.cumsum(tiles_per)
    start = (tile_end - tiles_per) * tm
    dest = start[e_flat] + rank
    n_rows = 2 * S + N_EXPERTS * tm
    n_tiles = n_rows // tm
    token = jnp.arange(2 * S, dtype=jnp.int32) // 2
    src = jnp.zeros((n_rows,), jnp.int32).at[dest].set(token)
    n_used = tile_end[-1].astype(jnp.int32)
    tile_id = jnp.minimum(jnp.arange(n_tiles, dtype=jnp.int32), n_used - 1)
    tile_expert = jnp.sum((tile_end[None, :] <= tile_id[:, None]).astype(jnp.int32), axis=1)
    tile_expert = jnp.minimum(tile_expert, N_EXPERTS - 1)
    return src, weights, dest.astype(jnp.int32), tile_expert, n_used.reshape(1)


def _adaln(c, c_ctx, w, b):
    D = c.shape[-1]
    rows = jnp.zeros((8, D), F32).at[0].set(c[0]).at[1].set(c_ctx)
    m = matmul(jax.nn.silu(rows), w, 8, 2048)[:2] + b[None, :]
    return m.reshape(12, 1, D)


def _head_rms(o, gain, n_heads):
    T = o.shape[0]
    oh = o.reshape(T, n_heads, -1)
    y = oh * lax.rsqrt(jnp.mean(oh * oh, axis=-1, keepdims=True) + NORM_EPS) * gain
    return y.reshape(T, -1)


def _even_mixer(h, L, w_in, w_out, attn_sink, hgrn_norm, hgrn_lb):
    T = h.shape[0]
    S = T - L
    proj = matmul(h, jnp.concatenate([w_in[:, 1536:], w_in[:, :1536]], axis=1), 1280, 512)
    att = sink_attention(proj, 5, 24, 25, L, attn_sink, _rope_tables(S))
    att_c = sink_attention(proj, 5, 24, 25, L, attn_sink, None)
    lb = (hgrn_lb.reshape(1, -1),)
    qh, ih = (proj, 1024, 0), (proj, 1024, 1)
    o_rev = chunk_mixer(_hgrn_heads, [qh, ih, (proj, 1024, 3), lb], 8, 128, 128, L, True)
    hg = chunk_mixer(_hgrn_heads, [qh, ih, (proj, 1024, 2), lb], 8, 128, 128, L, False,
                     final=((proj, 1024, 4), o_rev, hgrn_norm))
    ycat = jnp.concatenate([jnp.concatenate([att_c, att], axis=0), hg], axis=-1)
    return matmul(ycat, w_out, 1280, 1024)


def _token_shift(p, L, mu_prev, mu_next):
    def one(s):
        zero = jnp.zeros_like(s[:1])
        prev = jnp.concatenate([zero, s[:-1]], axis=0)
        nxt = jnp.concatenate([s[1:], zero], axis=0)
        return s + mu_prev * (prev - s) + mu_next * (nxt - s)
    return jnp.concatenate([one(p[:L]), one(p[L:])], axis=0)


def _odd_mixer(h, L, w_in, w_out, gla_gate_up_f, gla_gate_up_b, gla_gate_bias_f, gla_gate_bias_b, gla_norm,
               mu_prev, mu_next, w0_f, w0_b, w2_f, w2_b, a0, a2, g2, k_k, k_a, r_k, ln_w, ln_b):
    T = h.shape[0]
    GO = 3104
    cols = lambda a, b: w_in[:, a:b]
    w_big = jnp.concatenate([cols(0, 2048), cols(2080, 3104), cols(GO, GO + 3072), cols(GO + 3360, GO + 3616)], axis=1)
    w_small = jnp.concatenate([cols(2048, 2080), cols(GO + 3072, GO + 3360),
                               jnp.zeros((w_in.shape[0], 64), F32)], axis=1)
    pb = matmul(h, w_big, 1280, 640)
    ps = matmul(h, w_small, 1280, 384)
    def gla_inputs(up, row0, bias):
        up_rows = jnp.zeros((ps.shape[1], up.shape[1]), F32).at[row0:row0 + up.shape[0]].set(up)
        return [(pb, 512, 0), (pb, 512, 1), (pb, 1024, 1), (ps, ps.shape[1], 0), (up_rows,), (bias.reshape(1, -1),)]
    o_rev = chunk_mixer(_gla_heads, gla_inputs(gla_gate_up_b, 16, gla_gate_bias_b), 4, 128, 256, L, True)
    gla = chunk_mixer(_gla_heads, gla_inputs(gla_gate_up_f, 0, gla_gate_bias_f), 4, 128, 256, L, False,
                      final=((pb, 1024, 2), o_rev, gla_norm))
    rw = rwkv7_branch(pb, ps, 1, 24, L, mu_prev, mu_next, (32, 128, 224), w0_f, w0_b, w2_f, w2_b,
                      a0, a2, g2, k_k, k_a, r_k, ln_w, ln_b)
    ycat = jnp.concatenate([gla[L:], rw[L:]], axis=-1)
    return matmul(ycat, w_out, 1024, 1024)


def kernel(x, c, ctx, c_ctx, hgrn_lb_logits, l0_ada_w, l0_ada_b, l0_norm_mix_pre, l0_norm_mix_post, l0_norm_ffn_pre, l0_norm_ffn_post, l0_w_in, l0_w_out, l0_attn_sink, l0_hgrn_norm, l0_ffn_w_gate, l0_ffn_w_up, l0_ffn_w_down, l1_ada_w, l1_ada_b, l1_norm_mix_pre, l1_norm_mix_post, l1_norm_ffn_pre, l1_norm_ffn_post, l1_w_in, l1_w_out, l1_gla_gate_up_f, l1_gla_gate_up_b, l1_gla_gate_bias_f, l1_gla_gate_bias_b, l1_gla_norm, l1_rwkv_mu_prev, l1_rwkv_mu_next, l1_rwkv_w0_f, l1_rwkv_w0_b, l1_rwkv_w2_f, l1_rwkv_w2_b, l1_rwkv_a0, l1_rwkv_a2, l1_rwkv_g2, l1_rwkv_k_k, l1_rwkv_k_a, l1_rwkv_r_k, l1_rwkv_ln_w, l1_rwkv_ln_b, l1_moe_router, l1_moe_w_gate, l1_moe_w_up, l1_moe_w_down):
    B, S, D = x.shape
    L = ctx.shape[1]
    assert B == 1
    T = L + S
    SH1, SC1, G1, SH2, SC2, G2 = range(6)
    xa = jnp.concatenate([ctx[0], x[0]], axis=0)
    hgrn_lb = jnp.cumsum(jax.nn.softmax(hgrn_lb_logits.astype(F32), axis=0), axis=0)
    m0 = _adaln(c, c_ctx, l0_ada_w, l0_ada_b)
    m1 = _adaln(c, c_ctx, l1_ada_w, l1_ada_b)

    h = normmod(xa, l0_norm_mix_pre, m0, SH1, SC1, L)
    y = _even_mixer(h, L, l0_w_in, l0_w_out, l0_attn_sink, l0_hgrn_norm, hgrn_lb[0])
    xa, h = resid_norm(xa, y, l0_norm_mix_post, m0, G1, L, nxt=(l0_norm_ffn_pre, m0, SH2, SC2))
    n_t = T // DENSE_TILE
    f = swiglu_ffn(h, l0_ffn_w_gate[None].astype(BF16), l0_ffn_w_up[None].astype(BF16),
                   l0_ffn_w_down[None].astype(BF16),
                   jnp.zeros((n_t,), jnp.int32), jnp.full((1,), n_t, jnp.int32), DENSE_TILE, 512)
    xa, h = resid_norm(xa, f, l0_norm_ffn_post, m0, G2, L, nxt=(l1_norm_mix_pre, m1, SH1, SC1))

    y = _odd_mixer(h, L, l1_w_in, l1_w_out, l1_gla_gate_up_f, l1_gla_gate_up_b, l1_gla_gate_bias_f,
                   l1_gla_gate_bias_b, l1_gla_norm, l1_rwkv_mu_prev, l1_rwkv_mu_next, l1_rwkv_w0_f, l1_rwkv_w0_b,
                   l1_rwkv_w2_f, l1_rwkv_w2_b, l1_rwkv_a0, l1_rwkv_a2, l1_rwkv_g2, l1_rwkv_k_k, l1_rwkv_k_a,
                   l1_rwkv_r_k, l1_rwkv_ln_w, l1_rwkv_ln_b)
    xl, h = resid_norm(xa[L:], y, l1_norm_mix_post, m1, G1, 0, nxt=(l1_norm_ffn_pre, m1, SH2, SC2))
    router = jnp.concatenate([l1_moe_router, jnp.zeros((D, 128 - N_EXPERTS), F32)], axis=1)
    logits = matmul(h, router, 1024, 128)[:, :N_EXPERTS]
    src, gate_w, dest, tile_expert, n_used = _route(logits, MOE_TILE)
    hs = gather_rows(h, src, TOK_BLOCK)
    ys = swiglu_ffn(hs, l1_moe_w_gate.astype(BF16), l1_moe_w_up.astype(BF16), l1_moe_w_down.astype(BF16),
                    tile_expert, n_used, MOE_TILE, 1024)
    out = combine_resid(xl, ys, dest, gate_w, l1_norm_ffn_post, m1, G2)
    return out[None]
```

```python
import functools

import jax
import jax.numpy as jnp
from jax import lax
from jax.experimental import pallas as pl
from jax.experimental.pallas import tpu as pltpu

F32 = jnp.float32
BF16 = jnp.bfloat16

NORM_EPS = 1e-6
CHUNK = 32
TOK_BLOCK = 256
ATT_BLOCK = 128
WINDOW = 128
HEAD_DIM = 128
ATT_HEADS = 8
ATT_KV_HEADS = 2
ROPE_THETA = 10000.0
GRID_W = 64
RWKV_N = 64
RWKV_HEADS = 16
RWKV_LN_EPS = 64e-5
RWKV_HEAD_BLOCK = 4
GLA_GATE_NORMALIZER = 16.0
N_EXPERTS = 8
MOE_TILE = 512
DENSE_TILE = 1280
VMEM_LIMIT_BYTES = 56 * 1024 * 1024
LANES = 128
BF16_SUBLANES = 16
ROW_COPY_UNROLL = 8


def _params(*sem):
    return pltpu.CompilerParams(dimension_semantics=sem, vmem_limit_bytes=VMEM_LIMIT_BYTES)


def _mm_kernel(x_ref, w_ref, o_ref, wbf_ref):
    @pl.when(pl.program_id(1) == 0)
    def _():
        wbf_ref[...] = w_ref[...].astype(BF16)

    o_ref[...] = jnp.dot(x_ref[...].astype(BF16), wbf_ref[...],
                         preferred_element_type=F32).astype(o_ref.dtype)


def matmul(x, w, tm, tn, out_dtype=F32):
    M, K = x.shape
    N = w.shape[1]
    assert M % tm == 0 and N % tn == 0, (M, tm, N, tn)
    return pl.pallas_call(
        _mm_kernel,
        grid=(N // tn, M // tm),
        in_specs=[pl.BlockSpec((tm, K), lambda j, i: (i, 0)),
                  pl.BlockSpec((K, tn), lambda j, i: (0, j))],
        out_specs=pl.BlockSpec((tm, tn), lambda j, i: (i, j)),
        out_shape=jax.ShapeDtypeStruct((M, N), out_dtype),
        scratch_shapes=[pltpu.VMEM((K, tn), BF16)],
        compiler_params=_params("arbitrary", "arbitrary"),
        name="matmul",
    )(x, w)


def _rms(x, gain):
    ms = jnp.mean(x * x, axis=-1, keepdims=True)
    return x * lax.rsqrt(ms + NORM_EPS) * gain


def _normmod_kernel(x_ref, gain_ref, sc_ref, sh_ref, h_ref):
    y = _rms(x_ref[...], gain_ref[...])
    h_ref[...] = (y * (1.0 + sc_ref[0]) + sh_ref[0]).astype(h_ref.dtype)


def _mod_spec(D, slot, n_ctx_tiles):
    return pl.BlockSpec((1, 1, D), lambda i: (jnp.where(i < n_ctx_tiles, 6, 0) + slot, 0, 0))


def normmod(x, gain, mods, sh_slot, sc_slot, n_ctx_rows):
    R, D = x.shape
    tr = TOK_BLOCK
    assert R % tr == 0 and n_ctx_rows % tr == 0
    nct = n_ctx_rows // tr
    return pl.pallas_call(
        _normmod_kernel,
        grid=(R // tr,),
        in_specs=[pl.BlockSpec((tr, D), lambda i: (i, 0)),
                  pl.BlockSpec((1, D), lambda i: (0, 0)),
                  _mod_spec(D, sc_slot, nct), _mod_spec(D, sh_slot, nct)],
        out_specs=pl.BlockSpec((tr, D), lambda i: (i, 0)),
        out_shape=jax.ShapeDtypeStruct((R, D), BF16),
        compiler_params=_params("arbitrary"),
        name="normmod",
    )(x, gain.reshape(1, D), mods, mods)


def _resid_kernel(x_ref, y_ref, gpost_ref, g_ref, *rest, with_next):
    xn = x_ref[...] + g_ref[0] * _rms(y_ref[...], gpost_ref[...])
    if with_next:
        gpre_ref, sc_ref, sh_ref, xo_ref, h_ref = rest
        xo_ref[...] = xn
        h_ref[...] = (_rms(xn, gpre_ref[...]) * (1.0 + sc_ref[0]) + sh_ref[0]).astype(h_ref.dtype)
    else:
        (xo_ref,) = rest
        xo_ref[...] = xn


def resid_norm(x, y, gain_post, mods, g_slot, n_ctx_rows, nxt=None, h_dtype=BF16):
    R, D = x.shape
    tr = TOK_BLOCK
    assert R % tr == 0 and n_ctx_rows % tr == 0
    nct = n_ctx_rows // tr
    row = pl.BlockSpec((tr, D), lambda i: (i, 0))
    vec = pl.BlockSpec((1, D), lambda i: (0, 0))
    in_specs = [row, row, vec, _mod_spec(D, g_slot, nct)]
    args = [x, y, gain_post.reshape(1, D), mods]
    out_specs = [row]
    out_shape = [jax.ShapeDtypeStruct((R, D), F32)]
    if nxt is not None:
        gain_pre, mods_n, sh_slot, sc_slot = nxt
        in_specs += [vec, _mod_spec(D, sc_slot, nct), _mod_spec(D, sh_slot, nct)]
        args += [gain_pre.reshape(1, D), mods_n, mods_n]
        out_specs.append(row)
        out_shape.append(jax.ShapeDtypeStruct((R, D), h_dtype))
    out = pl.pallas_call(
        functools.partial(_resid_kernel, with_next=nxt is not None),
        grid=(R // tr,),
        in_specs=in_specs, out_specs=out_specs, out_shape=out_shape,
        compiler_params=_params("arbitrary"),
        name="resid_norm",
    )(*args)
    return out if nxt is not None else out[0]


def _ffn_kernel(te_ref, nu_ref, x_ref, wg_ref, wu_ref, wd_ref, o_ref):
    i, j = pl.program_id(0), pl.program_id(1)

    @pl.when(i < nu_ref[0])
    def _():
        x = x_ref[...]
        g = jnp.dot(x, wg_ref[0], preferred_element_type=F32)
        u = jnp.dot(x, wu_ref[0], preferred_element_type=F32)
        hid = (g * jax.nn.sigmoid(g) * u).astype(BF16)
        part = jnp.dot(hid, wd_ref[0], preferred_element_type=F32)

        @pl.when(j == 0)
        def _():
            o_ref[...] = part

        @pl.when(j > 0)
        def _():
            o_ref[...] += part

    @pl.when(jnp.logical_and(i >= nu_ref[0], j == 0))
    def _():
        o_ref[...] = jnp.zeros_like(o_ref)


def swiglu_ffn(x, w_gate, w_up, w_down, tile_expert, n_used, tm, tf):
    R, D = x.shape
    E, _, F = w_gate.shape
    assert R % tm == 0 and F % tf == 0 and w_gate.dtype == BF16 and x.dtype == BF16
    nf = F // tf

    def fblk(i, j, te, nu):
        return jnp.where(i < nu[0], j, nf - 1)

    in_specs = [pl.BlockSpec((tm, D), lambda i, j, te, nu: (i, 0)),
                pl.BlockSpec((1, D, tf), lambda i, j, te, nu: (te[i], 0, fblk(i, j, te, nu))),
                pl.BlockSpec((1, D, tf), lambda i, j, te, nu: (te[i], 0, fblk(i, j, te, nu))),
                pl.BlockSpec((1, tf, D), lambda i, j, te, nu: (te[i], fblk(i, j, te, nu), 0))]
    args = [x, w_gate, w_up, w_down]
    return pl.pallas_call(
        _ffn_kernel,
        grid_spec=pltpu.PrefetchScalarGridSpec(
            num_scalar_prefetch=2,
            grid=(R // tm, nf),
            in_specs=in_specs,
            out_specs=pl.BlockSpec((tm, D), lambda i, j, te, nu: (i, 0)),
        ),
        out_shape=jax.ShapeDtypeStruct((R, D), F32),
        compiler_params=_params("arbitrary", "arbitrary"),
        name="swiglu_ffn",
    )(tile_expert, n_used, *args)


def _rope(x, c, sg):
    lane = lax.broadcasted_iota(jnp.int32, x.shape, 1)
    first_half = jnp.bitwise_and(lane, HEAD_DIM // 2 - 1) < HEAD_DIM // 4
    partner = jnp.where(first_half, pltpu.roll(x, HEAD_DIM - HEAD_DIM // 4, 1), pltpu.roll(x, HEAD_DIM // 4, 1))
    return x * c + partner * sg


def _attn_kernel(sink_ref, q_ref, *rest, has_window, seq_len):
    if has_window:
        (kp_ref, ko_ref, kn_ref, vp_ref, vo_ref, vn_ref, cp_ref, co_ref, cn_ref, sp_ref, so_ref, sn_ref,
         kc_ref, vc_ref, o_ref) = rest
    else:
        kc_ref, vc_ref, o_ref = rest
    i = pl.program_id(0)
    BQ = q_ref.shape[0]
    L = kc_ref.shape[0]
    G = ATT_KV_HEADS
    R = ATT_HEADS // G
    n_win = 3 * BQ if has_window else 0
    nk = n_win + L
    if has_window:
        row = lax.broadcasted_iota(jnp.int32, (R * BQ, nk), 0)
        col = lax.broadcasted_iota(jnp.int32, (R * BQ, nk), 1)
        qi = jnp.bitwise_and(row, BQ - 1)
        kpos = i * BQ + col - BQ
        in_band = jnp.abs(col - BQ - qi) <= WINDOW
        in_seq = jnp.logical_and(kpos >= 0, kpos < seq_len)
        valid = jnp.logical_or(col >= n_win, jnp.logical_and(in_band, in_seq))
    for g in range(G):
        cs = slice(g * HEAD_DIM, (g + 1) * HEAD_DIM)
        q_heads = [q_ref[:, (g * R + r) * HEAD_DIM:(g * R + r + 1) * HEAD_DIM] for r in range(R)]
        if has_window:
            q_heads = [_rope(q, co_ref[...], so_ref[...]) for q in q_heads]
            k_all = jnp.concatenate([_rope(kp_ref[:, cs], cp_ref[...], sp_ref[...]),
                                     _rope(ko_ref[:, cs], co_ref[...], so_ref[...]),
                                     _rope(kn_ref[:, cs], cn_ref[...], sn_ref[...]), kc_ref[:, cs]], axis=0)
            v_all = jnp.concatenate([vp_ref[:, cs], vo_ref[:, cs], vn_ref[:, cs], vc_ref[:, cs]], axis=0)
        else:
            k_all, v_all = kc_ref[:, cs], vc_ref[:, cs]
        q_g = (jnp.concatenate(q_heads, axis=0) * (HEAD_DIM ** -0.5)).astype(BF16)
        s = lax.dot_general(q_g, k_all.astype(BF16), (((1,), (1,)), ((), ())),
                            preferred_element_type=F32)
        if has_window:
            s = jnp.where(valid, s, -jnp.inf)
        sink = jnp.concatenate([jnp.full((BQ, 1), sink_ref[g * R + r], F32) for r in range(R)], axis=0)
        m = jnp.maximum(sink, jnp.max(s, axis=-1, keepdims=True))
        p = jnp.exp(s - m)
        denom = jnp.exp(sink - m) + jnp.sum(p, axis=-1, keepdims=True)
        o = jnp.dot(p.astype(BF16), v_all.astype(BF16), preferred_element_type=F32) / denom
        for r in range(R):
            h = g * R + r
            o_ref[:, h * HEAD_DIM:(h + 1) * HEAD_DIM] = o[r * BQ:(r + 1) * BQ].astype(o_ref.dtype)


def sink_attention(proj, q_col, k_col, v_col, L, sink, rope):
    QW, KW = ATT_HEADS * HEAD_DIM, ATT_KV_HEADS * HEAD_DIM
    has_window = rope is not None
    S = proj.shape[0] - L if has_window else L
    BQ = ATT_BLOCK if has_window else L
    nb = S // BQ
    assert S % BQ == 0 and L % BQ == 0
    r0 = L // BQ if has_window else 0
    in_specs = [pl.BlockSpec(memory_space=pltpu.SMEM), pl.BlockSpec((BQ, QW), lambda i: (r0 + i, q_col))]
    args = [sink, proj]
    if has_window:
        prev = lambda i: jnp.maximum(i - 1, 0)
        nxt = lambda i: jnp.minimum(i + 1, nb - 1)
        for col in (k_col, v_col):
            in_specs += [pl.BlockSpec((BQ, KW), lambda i, col=col: (r0 + prev(i), col)),
                         pl.BlockSpec((BQ, KW), lambda i, col=col: (r0 + i, col)),
                         pl.BlockSpec((BQ, KW), lambda i, col=col: (r0 + nxt(i), col))]
            args += [proj] * 3
        for tab in rope:
            in_specs += [pl.BlockSpec((BQ, HEAD_DIM), lambda i: (prev(i), 0)),
                         pl.BlockSpec((BQ, HEAD_DIM), lambda i: (i, 0)),
                         pl.BlockSpec((BQ, HEAD_DIM), lambda i: (nxt(i), 0))]
            args += [tab] * 3
    in_specs += [pl.BlockSpec((L, KW), lambda i: (0, k_col)), pl.BlockSpec((L, KW), lambda i: (0, v_col))]
    args += [proj, proj]
    return pl.pallas_call(
        functools.partial(_attn_kernel, has_window=has_window, seq_len=S),
        grid=(nb,),
        in_specs=in_specs,
        out_specs=pl.BlockSpec((BQ, QW), lambda i: (i, 0)),
        out_shape=jax.ShapeDtypeStruct((S, QW), BF16),
        compiler_params=_params("arbitrary"),
        name="sink_attention",
    )(*args)


def _rope_tables(n_tokens):
    n_freq = HEAD_DIM // 4
    t = jnp.arange(n_tokens)
    row = (t // GRID_W).astype(F32)
    col = (t % GRID_W).astype(F32)
    inv = ROPE_THETA ** (-jnp.arange(n_freq, dtype=F32) / n_freq)
    ar, ac = row[:, None] * inv, col[:, None] * inv
    c = jnp.concatenate([jnp.cos(ar), jnp.cos(ar), jnp.cos(ac), jnp.cos(ac)], axis=1)
    sg = jnp.concatenate([-jnp.sin(ar), jnp.sin(ar), -jnp.sin(ac), jnp.sin(ac)], axis=1)
    return c, sg


CHUNK_SHIFT = CHUNK.bit_length() - 1
assert 1 << CHUNK_SHIFT == CHUNK


def _chunk_masks(tb, reverse, strict):
    r = lax.broadcasted_iota(jnp.int32, (tb, tb), 0)
    c = lax.broadcasted_iota(jnp.int32, (tb, tb), 1)
    same = jnp.right_shift(r, CHUNK_SHIFT) == jnp.right_shift(c, CHUNK_SHIFT)
    if reverse:
        tri = (c > r) if strict else (c >= r)
    else:
        tri = (c < r) if strict else (c <= r)
    return same, jnp.logical_and(same, tri)


def _chunk_sums(ld, same, tri):
    tb = ld.shape[0]
    sel = jnp.concatenate([jnp.where(tri, 1.0, 0.0), jnp.where(same, 1.0, 0.0)], axis=0).astype(BF16)
    hi = ld.astype(BF16)
    rest = ld - hi.astype(F32)
    mid = rest.astype(BF16)
    lo = (rest - mid.astype(F32)).astype(BF16)
    dot = lambda p: jnp.dot(sel, p, preferred_element_type=F32)
    g = (dot(lo) + dot(mid)) + dot(hi)
    return g[:tb], g[tb:]


def _dot_tn(a, b):
    return lax.dot_general(a, b, (((0,), (0,)), ((), ())), preferred_element_type=F32)


def _dot_nt(a, b):
    return lax.dot_general(a, b, (((1,), (1,)), ((), ())), preferred_element_type=F32)


def _chunkrec_core(load_head, n_heads, tb, V, reverse, st_ref, dst_ref):
    @pl.when(pl.program_id(0) == 0)
    def _():
        st_ref[...] = jnp.zeros_like(st_ref)

    same, tri = _chunk_masks(tb, reverse, strict=False)
    loaded = [load_head(h) for h in range(n_heads)]
    K = loaded[0][3].shape[1]
    g_cum_all, g_tot_all = _chunk_sums(jnp.concatenate([ld for _, _, _, ld in loaded], axis=1), same, tri)
    per_head = []
    for h in range(n_heads):
        q, k, v, _ = loaded[h]
        g_cum, g_tot = g_cum_all[:, h * K:(h + 1) * K], g_tot_all[:, h * K:(h + 1) * K]
        q_dec = (q * jnp.exp(g_cum)).astype(BF16)
        k_inv = (k * jnp.exp(-g_cum)).astype(BF16)
        k_tail = k * jnp.exp(g_tot - g_cum)
        a = jnp.where(tri, _dot_nt(q_dec, k_inv), 0.0)
        o_intra = jnp.dot(a.astype(BF16), v.astype(BF16), preferred_element_type=F32)
        per_head.append((q_dec, k_tail, v, g_tot, o_intra))
    n_chunks = tb // CHUNK
    order = range(n_chunks - 1, -1, -1) if reverse else range(n_chunks)
    sts = [st_ref[h] for h in range(n_heads)]
    for c in order:
        sl = slice(c * CHUNK, (c + 1) * CHUNK)
        for h in range(n_heads):
            q_dec, k_tail, v, g_tot, o_intra = per_head[h]
            st = sts[h]
            dst_ref[sl, h * V:(h + 1) * V] = o_intra[sl] + _dot_nt(q_dec[sl], st.astype(BF16))
            d = jnp.exp(g_tot[c * CHUNK:c * CHUNK + 1, :])
            sts[h] = st * d + _dot_tn(v[sl], k_tail[sl])
    for h in range(n_heads):
        st_ref[h] = sts[h]


def _gated_head_norm(o, gain, gate_raw):
    y = o * lax.rsqrt(jnp.mean(o * o, axis=-1, keepdims=True) + NORM_EPS) * gain
    return y * (gate_raw * jax.nn.sigmoid(gate_raw))


def _chunk_mixer_kernel(*refs, reverse, n_heads, K, V, load_heads):
    if reverse:
        *in_refs, o_ref, st_ref = refs
        dst_ref = o_ref
    else:
        *in_refs, gate_ref, orev_ref, gain_ref, o_ref, st_ref, dst_ref = refs
    tb = o_ref.shape[0]
    _chunkrec_core(load_heads(*in_refs), n_heads, tb, V, reverse, st_ref, dst_ref)
    if not reverse:
        for h in range(n_heads):
            vs = slice(h * V, (h + 1) * V)
            o_ref[:, vs] = _gated_head_norm(dst_ref[:, vs] + orev_ref[:, vs], gain_ref[...],
                                            gate_ref[:, vs]).astype(o_ref.dtype)


def _hgrn_heads(qh_ref, ih_ref, fr_ref, lb_ref):
    def load(h):
        cs = slice(h * 128, (h + 1) * 128)
        lb = lb_ref[:, cs]
        f = lb + (1.0 - lb) * jax.nn.sigmoid(fr_ref[:, cs])
        qh = qh_ref[:, cs]
        return qh * jax.nn.sigmoid(qh), 1.0 - f, ih_ref[:, cs], jnp.log(f)
    return load


def _gla_heads(gq_ref, gk_ref, gv_ref, gd_ref, up_ref, bias_ref):
    x = jnp.dot(gd_ref[...].astype(BF16), up_ref[...].astype(BF16), preferred_element_type=F32) + bias_ref[...]
    lg = (jnp.minimum(x, 0.0) - jnp.log(1.0 + jnp.exp(-jnp.abs(x)))) * (1.0 / GLA_GATE_NORMALIZER)

    def load(h):
        ks, vs = slice(h * 128, (h + 1) * 128), slice(h * 256, (h + 1) * 256)
        return gq_ref[:, ks] * (128 ** -0.5), gk_ref[:, ks], gv_ref[:, vs], lg[:, ks]
    return load


def _seq_block_index(nblk, nctx, reverse):
    if not reverse:
        return lambda t: t
    return lambda t: jnp.where(t < nctx, nctx - 1 - t, nblk - 1 - (t - nctx))


def chunk_mixer(load_heads, inputs, n_heads, K, V, n_ctx_rows, reverse, final=None):
    T = inputs[0][0].shape[0]
    tb = TOK_BLOCK
    assert T % tb == 0 and n_ctx_rows % tb == 0
    nblk, nctx = T // tb, n_ctx_rows // tb
    blk = _seq_block_index(nblk, nctx, reverse)

    def spec(item):
        if len(item) == 1:
            return pl.BlockSpec(item[0].shape, lambda t: (0, 0))
        _, width, cb = item
        return pl.BlockSpec((tb, width), lambda t: (blk(t), cb))

    HV = n_heads * V
    ospec = pl.BlockSpec((tb, HV), lambda t: (blk(t), 0))
    scratch = [pltpu.VMEM((n_heads, V, K), F32)]
    if not reverse:
        gate, o_rev, gain = final
        inputs = list(inputs) + [gate, (o_rev, HV, 0), (gain.reshape(1, V),)]
        scratch.append(pltpu.VMEM((tb, HV), F32))
    return pl.pallas_call(
        functools.partial(_chunk_mixer_kernel, reverse=reverse, n_heads=n_heads, K=K, V=V, load_heads=load_heads),
        grid=(nblk,),
        in_specs=[spec(it) for it in inputs],
        out_specs=ospec,
        out_shape=jax.ShapeDtypeStruct((T, HV), F32 if reverse else BF16),
        scratch_shapes=scratch,
        compiler_params=_params("arbitrary"),
        name="chunk_mixer",
    )(*[it[0] for it in inputs])


def _mm_bf(a, b):
    return jnp.dot(a.astype(BF16), b.astype(BF16), preferred_element_type=F32)


def _softplus(x):
    return jnp.maximum(x, 0.0) + jnp.log(1.0 + jnp.exp(-jnp.abs(x)))


def _head_sums(x, bd_ref):
    hi = x.astype(BF16)
    lo = (x - hi.astype(F32)).astype(BF16)
    bd = bd_ref[...]
    return jnp.dot(lo, bd, preferred_element_type=F32) + jnp.dot(hi, bd, preferred_element_type=F32)


def _rwkv_prep_kernel(p_ref, pp_ref, pn_ref, g_ref, gp_ref, gn_ref, s_ref, sp_ref, sn_ref,
                      mup_ref, mun_ref, mugp_ref, mugn_ref, musp_ref, musn_ref,
                      w2f_ref, w2b_ref, a2_ref, g2_ref, w0f_ref, w0b_ref, a0_ref, kk_ref, ka_ref, rk_ref, bd_ref,
                      r_out, k_out, v_out, kk_out, b_out, lwf_out, lwb_out, go_out, bon_out,
                      gcf_out, gtf_out, gcb_out, gtb_out, *, n_ctx_blocks):
    t, nblk = pl.program_id(0), pl.num_programs(0)
    tb = p_ref.shape[0]
    C = RWKV_HEADS * RWKV_N
    keep_prev = jnp.where(jnp.logical_or(t == 0, t == n_ctx_blocks), 0.0, 1.0)
    keep_next = jnp.where(jnp.logical_or(t == n_ctx_blocks - 1, t == nblk - 1), 0.0, 1.0)

    def shifted(x_ref, xp_ref, xn_ref, mu_p_ref, mu_n_ref):
        x = x_ref[...]
        rows = lax.broadcasted_iota(jnp.int32, x.shape, 0)
        prev = jnp.where(rows == 0, xp_ref[7:8, :] * keep_prev, pltpu.roll(x, 1, 0))
        nxt = jnp.where(rows == tb - 1, xn_ref[0:1, :] * keep_next, pltpu.roll(x, tb - 1, 0))
        return x + mu_p_ref[...] * (prev - x) + mu_n_ref[...] * (nxt - x)

    rkv = shifted(p_ref, pp_ref, pn_ref, mup_ref, mun_ref)
    rr, rk, rv = rkv[:, :C], rkv[:, C:2 * C], rkv[:, 2 * C:]
    low = shifted(s_ref, sp_ref, sn_ref, musp_ref, musn_ref)
    gd = shifted(g_ref, gp_ref, gn_ref, mugp_ref, mugn_ref)
    tl = jnp.tanh(low).astype(BF16)
    for w0_ref, w2_ref, lw_out, gc_out, gt_out, reverse in ((w0f_ref, w2f_ref, lwf_out, gcf_out, gtf_out, False),
                                                            (w0b_ref, w2b_ref, lwb_out, gcb_out, gtb_out, True)):
        lw = -jnp.exp(-_softplus(-(w0_ref[...] + jnp.dot(tl, w2_ref[...], preferred_element_type=F32))) - 0.5)
        lw_out[...] = lw
        same, tri = _chunk_masks(tb, reverse, strict=False)
        gc_out[...], gt_out[...] = _chunk_sums(lw, same, tri)
    a_sig = jax.nn.sigmoid(a0_ref[...] + jnp.dot(low.astype(BF16), a2_ref[...], preferred_element_type=F32))
    go_out[...] = jnp.dot(jax.nn.sigmoid(gd).astype(BF16), g2_ref[...], preferred_element_type=F32)
    kk = rk * kk_ref[...]
    kk = kk * lax.rsqrt(_head_sums(kk * kk, bd_ref) + 1e-12)
    k_mod = rk * (1.0 + (a_sig - 1.0) * ka_ref[...])
    r_out[...] = rr
    k_out[...] = k_mod
    v_out[...] = rv
    kk_out[...] = kk
    b_out[...] = kk * a_sig
    bon_out[...] = _head_sums(rr * k_mod * rk_ref[...], bd_ref) * rv


def _rwkv_pre_kernel(r_ref, k_ref, v_ref, kk_ref, b_ref, lw_ref, gc_ref, gt_ref,
                     qp_ref, ol_ref, plt_ref, zt_ref, *, reverse):
    N = RWKV_N
    tb = r_ref.shape[0]
    hb = r_ref.shape[1] // N
    same, tri_incl = _chunk_masks(tb, reverse, strict=False)
    _, tri_strict = _chunk_masks(tb, reverse, strict=True)
    xs, lps, rest = [], [], []
    for h in range(hb):
        hs = slice(h * N, (h + 1) * N)
        r, k, v, b, lw, g_cum, g_tot = (ref[:, hs] for ref in (r_ref, k_ref, v_ref, b_ref, lw_ref, gc_ref, gt_ref))
        a = -kk_ref[:, hs]
        e_neg = jnp.exp(-g_cum)
        e_tail = jnp.exp(g_tot - g_cum)
        a_t = (a * jnp.exp(g_cum - lw)).astype(BF16)
        r_t = (r * jnp.exp(g_cum)).astype(BF16)
        b_t = (b * e_neg).astype(BF16)
        k_t = (k * e_neg).astype(BF16)
        a_ab = jnp.where(tri_strict, _dot_nt(a_t, b_t), 0.0)
        a_ak = jnp.where(tri_strict, _dot_nt(a_t, k_t), 0.0)
        a_rb = jnp.where(tri_incl, _dot_nt(r_t, b_t), 0.0)
        a_rk = jnp.where(tri_incl, _dot_nt(r_t, k_t), 0.0)
        xs.append(jnp.concatenate([a_t.astype(F32), _mm_bf(a_ak, v)], axis=1))
        lps.append(a_ab)
        rest.append((v, b * e_tail, k * e_tail, a_rb,
                     jnp.concatenate([r_t.astype(F32), _mm_bf(a_rk, v)], axis=1)))
    for j in range(CHUNK_SHIFT):
        xs = [x + _mm_bf(lp, x) for x, lp in zip(xs, lps)]
        if j < CHUNK_SHIFT - 1:
            lps = [_mm_bf(lp, lp) for lp in lps]
    for h in range(hb):
        v, b_h, k_h, a_rb, qo0 = rest[h]
        x = xs[h]
        qo = qo0 + _mm_bf(a_rb, x)
        qp_ref[:, h * N:(h + 1) * N] = qo[:, :N]
        ol_ref[:, h * N:(h + 1) * N] = qo[:, N:]
        w, uloc = x[:, :N], x[:, N:]
        for c in range(tb // CHUNK):
            sl = slice(c * CHUNK, (c + 1) * CHUNK)
            plt_ref[h, c] = _dot_tn(w[sl], b_h[sl])
            zt_ref[h, c] = _dot_tn(uloc[sl], b_h[sl]) + _dot_tn(v[sl], k_h[sl])


def _rwkv_scan_kernel(qp_ref, ol_ref, plt_ref, zt_ref, gt_ref, *rest, reverse):
    if reverse:
        o_ref, st_ref = rest
        dst_ref = o_ref
    else:
        orev_ref, go_ref, bon_ref, lnw_ref, lnb_ref, o_ref, st_ref, dst_ref = rest

    @pl.when(pl.program_id(0) == 0)
    def _():
        st_ref[...] = jnp.zeros_like(st_ref)

    N = RWKV_N
    tb = qp_ref.shape[0]
    n_heads = qp_ref.shape[1] // N
    n_chunks = tb // CHUNK
    order = range(n_chunks - 1, -1, -1) if reverse else range(n_chunks)
    sts = [st_ref[h] for h in range(n_heads)]
    for c in order:
        sl = slice(c * CHUNK, (c + 1) * CHUNK)
        for h in range(n_heads):
            hs = slice(h * N, (h + 1) * N)
            st = sts[h]
            dst_ref[sl, hs] = _dot_nt(qp_ref[sl, hs].astype(BF16), st.astype(BF16)) + ol_ref[sl, hs]
            d = jnp.exp(gt_ref[c * CHUNK:c * CHUNK + 1, hs])
            sts[h] = st * d + _mm_bf(st, plt_ref[h, c]) + zt_ref[h, c]
    for h in range(n_heads):
        st_ref[h] = sts[h]
    if not reverse:
        for h in range(n_heads):
            hs = slice(h * N, (h + 1) * N)
            o = dst_ref[:, hs] + orev_ref[:, hs]
            mu = jnp.mean(o, axis=-1, keepdims=True)
            var = jnp.mean(jnp.square(o - mu), axis=-1, keepdims=True)
            on = (o - mu) * lax.rsqrt(var + RWKV_LN_EPS) * lnw_ref[:, hs] + lnb_ref[:, hs]
            o_ref[:, hs] = ((on + bon_ref[:, hs]) * go_ref[:, hs]).astype(o_ref.dtype)


def rwkv7_branch(pb, ps, rkv_col, gd_col, n_ctx_rows, mu_prev, mu_next, low_cols, w0_f, w0_b, w2_f, w2_b,
                 a0, a2, g2, k_k, k_a, r_k, ln_w, ln_b):
    T = pb.shape[0]
    H, N = RWKV_HEADS, RWKV_N
    C = H * N
    tb = TOK_BLOCK
    assert T % tb == 0 and n_ctx_rows % tb == 0
    nblk, nctx = T // tb, n_ctx_rows // tb
    ncb = tb // CHUNK
    GW, SW, R = g2.shape[0], ps.shape[1], w2_f.shape[0]
    wf_col, wb_col, ad_col = low_cols
    s0 = 3 * C
    row = lambda v: v.reshape(1, -1)

    def padded_rows(w, r0):
        return jnp.zeros((SW, C), F32).at[r0:r0 + w.shape[0]].set(w).astype(BF16)

    def mu_low(mu):
        return jnp.zeros((1, SW), F32).at[0, wf_col:wf_col + 3 * R].set(mu[s0:s0 + 3 * R])

    bd = (jnp.arange(C)[:, None] // N == jnp.arange(C)[None, :] // N).astype(BF16)
    r8 = tb // 8

    def halo(width, col):
        return (pl.BlockSpec((tb, width), lambda t: (t, col)),
                pl.BlockSpec((8, width), lambda t: (jnp.maximum(t * r8 - 1, 0), col)),
                pl.BlockSpec((8, width), lambda t: (jnp.minimum((t + 1) * r8, T // 8 - 1), col)))

    whole = lambda a: pl.BlockSpec(a.shape, lambda t: (0, 0))
    consts = [row(mu_prev[:s0]), row(mu_next[:s0]), row(mu_prev[s0 + 3 * R:]), row(mu_next[s0 + 3 * R:]),
              mu_low(mu_prev), mu_low(mu_next),
              padded_rows(w2_f, wf_col), padded_rows(w2_b, wb_col), padded_rows(a2, ad_col), g2.astype(BF16),
              row(w0_f), row(w0_b), row(a0), row(k_k), row(k_a), row(r_k), bd]
    tokC = pl.BlockSpec((tb, C), lambda t: (t, 0))
    shapeC = jax.ShapeDtypeStruct((T, C), F32)
    r, k, v, kk, b, lw_f, lw_b, g_out, bonus, gc_f, gt_f, gc_b, gt_b = pl.pallas_call(
        functools.partial(_rwkv_prep_kernel, n_ctx_blocks=nctx),
        grid=(nblk,),
        in_specs=[*halo(3 * C, rkv_col), *halo(GW, gd_col), *halo(SW, 0), *[whole(a) for a in consts]],
        out_specs=[tokC] * 13,
        out_shape=[shapeC] * 13,
        compiler_params=_params("arbitrary"),
        name="rwkv_prep",
    )(pb, pb, pb, pb, pb, pb, ps, ps, ps, *consts)

    hb = RWKV_HEAD_BLOCK
    tok = pl.BlockSpec((tb, hb * N), lambda h, t: (t, h))
    mat = pl.BlockSpec((hb, ncb, N, N), lambda h, t: (h, t, 0, 0))
    mat_shape = jax.ShapeDtypeStruct((H, T // CHUNK, N, N), F32)
    o_rev = None
    for lw, gc, gt, reverse in ((lw_b, gc_b, gt_b, True), (lw_f, gc_f, gt_f, False)):
        qp, ol, plt, zt = pl.pallas_call(
            functools.partial(_rwkv_pre_kernel, reverse=reverse),
            grid=(H // hb, nblk),
            in_specs=[tok] * 8,
            out_specs=[tok, tok, mat, mat],
            out_shape=[shapeC, shapeC, mat_shape, mat_shape],
            compiler_params=_params("arbitrary", "arbitrary"),
            name="rwkv_pre",
        )(r, k, v, kk, b, lw, gc, gt)
        blk = _seq_block_index(nblk, nctx, reverse)
        tok_all = pl.BlockSpec((tb, C), lambda t: (blk(t), 0))
        mat_all = pl.BlockSpec((H, ncb, N, N), lambda t: (0, blk(t), 0, 0))
        in_specs = [tok_all, tok_all, mat_all, mat_all, tok_all]
        args = [qp, ol, plt, zt, gt]
        scratch = [pltpu.VMEM((H, N, N), F32)]
        if not reverse:
            in_specs += [tok_all, tok_all, tok_all, whole(row(ln_w)), whole(row(ln_b))]
            args += [o_rev, g_out, bonus, row(ln_w), row(ln_b)]
            scratch.append(pltpu.VMEM((tb, C), F32))
        out = pl.pallas_call(
            functools.partial(_rwkv_scan_kernel, reverse=reverse),
            grid=(nblk,),
            in_specs=in_specs,
            out_specs=tok_all,
            out_shape=jax.ShapeDtypeStruct((T, C), F32 if reverse else BF16),
            scratch_shapes=scratch,
            compiler_params=_params("arbitrary"),
            name="rwkv_scan",
        )(*args)
        if reverse:
            o_rev = out
    return out


def _row_copy(src_hbm, src_row, dst_ref, dst_row, sem):
    return pltpu.make_async_copy(src_hbm.at[pl.ds(src_row, 1)], dst_ref.at[pl.ds(dst_row, 1)], sem)


def _gather_kernel(src_ref, nxt_ref, x_hbm, o_ref, buf, sems):
    i, n_steps = pl.program_id(0), pl.num_programs(0)
    n = o_ref.shape[0]

    def issue(idx_ref, slot):
        def body(r, carry):
            for u in range(2):
                row = 2 * r + u
                _row_copy(x_hbm, idx_ref[0, 0, row], buf.at[slot], row, sems.at[slot]).start(priority=u)
            return carry
        lax.fori_loop(0, n // 2, body, 0, unroll=ROW_COPY_UNROLL)

    @pl.when(i == 0)
    def _():
        issue(src_ref, 0)

    @pl.when(i + 1 < n_steps)
    def _():
        issue(nxt_ref, (i + 1) % 2)

    slot = i % 2

    pltpu.make_async_copy(x_hbm.at[pl.ds(0, n)], buf.at[slot], sems.at[slot]).wait()
    o_ref[...] = buf[slot].astype(o_ref.dtype)


def gather_rows(x, src, tg):
    S, D = x.shape
    P = src.shape[0]
    sub = D // LANES
    assert P % tg == 0 and x.dtype == BF16 and sub == BF16_SUBLANES
    n_steps = P // tg
    src3 = src.reshape(n_steps, 1, tg)
    out = pl.pallas_call(
        _gather_kernel,
        grid=(n_steps,),
        in_specs=[pl.BlockSpec((1, 1, tg), lambda i: (i, 0, 0), memory_space=pltpu.SMEM),
                  pl.BlockSpec((1, 1, tg), lambda i: (jnp.minimum(i + 1, n_steps - 1), 0, 0),
                               memory_space=pltpu.SMEM),
                  pl.BlockSpec(memory_space=pl.ANY)],
        out_specs=pl.BlockSpec((tg, sub, LANES), lambda i: (i, 0, 0)),
        out_shape=jax.ShapeDtypeStruct((P, sub, LANES), BF16),
        scratch_shapes=[pltpu.VMEM((2, tg, sub, LANES), BF16), pltpu.SemaphoreType.DMA((2,))],
        compiler_params=_params("arbitrary"),
        name="gather_rows",
    )(src3, src3, x.reshape(S, sub, LANES))
    return out.reshape(P, D)


def _combine_kernel(pos_ref, nxt_ref, x_ref, w_ref, ys_hbm, gpost_ref, g_ref, o_ref, buf, sems):
    i, n_steps = pl.program_id(0), pl.num_programs(0)
    n = x_ref.shape[0]

    def issue(idx_ref, slot):
        def body(r, carry):
            for k in range(2):
                _row_copy(ys_hbm, idx_ref[0, 0, 2 * r + k], buf.at[slot, k], r, sems.at[slot, k]).start(priority=k)
            return carry
        lax.fori_loop(0, n, body, 0, unroll=ROW_COPY_UNROLL)

    @pl.when(i == 0)
    def _():
        issue(pos_ref, 0)

    @pl.when(i + 1 < n_steps)
    def _():
        issue(nxt_ref, (i + 1) % 2)

    slot = i % 2

    for k in range(2):
        pltpu.make_async_copy(ys_hbm.at[pl.ds(0, n)], buf.at[slot, k], sems.at[slot, k]).wait()
    w = w_ref[...]
    f = w[:, 0:1] * buf[slot, 0] + w[:, 1:2] * buf[slot, 1]
    o_ref[...] = x_ref[...] + g_ref[0] * _rms(f, gpost_ref[...])


def combine_resid(x, ys, pos, weights, gain_post, mods, g_slot):
    S, D = x.shape
    tc = TOK_BLOCK
    assert S % tc == 0
    n_steps = S // tc
    row = pl.BlockSpec((tc, D), lambda i: (i, 0))
    pos3 = pos.reshape(n_steps, 1, 2 * tc)
    return pl.pallas_call(
        _combine_kernel,
        grid=(n_steps,),
        in_specs=[pl.BlockSpec((1, 1, 2 * tc), lambda i: (i, 0, 0), memory_space=pltpu.SMEM),
                  pl.BlockSpec((1, 1, 2 * tc), lambda i: (jnp.minimum(i + 1, n_steps - 1), 0, 0),
                               memory_space=pltpu.SMEM),
                  row, pl.BlockSpec((tc, 2), lambda i: (i, 0)), pl.BlockSpec(memory_space=pl.ANY),
                  pl.BlockSpec((1, D), lambda i: (0, 0)), _mod_spec(D, g_slot, 0)],
        out_specs=row,
        out_shape=jax.ShapeDtypeStruct((S, D), F32),
        scratch_shapes=[pltpu.VMEM((2, 2, tc, D), F32), pltpu.SemaphoreType.DMA((2, 2))],
        compiler_params=_params("arbitrary"),
        name="combine_resid",
    )(pos3, pos3, x, weights, ys, gain_post.reshape(1, D), mods)


def _route(logits, tm):
    S = logits.shape[0]
    top_val, top_idx = lax.top_k(logits, 2)
    weights = jax.nn.softmax(top_val, axis=-1)
    e_flat = top_idx.reshape(-1)
    onehot = (e_flat[:, None] == jnp.arange(N_EXPERTS)[None, :]).astype(jnp.int32)
    rank = jnp.take_along_axis(jnp.cumsum(onehot, axis=0), e_flat[:, None], axis=1)[:, 0] - 1
    counts = jnp.sum(onehot, axis=0)
    tiles_per = (counts + tm - 1) // tm
    tile_end = jnp.cumsum(tiles_per)
    start = (tile_end - tiles_per) * tm
    dest = start[e_flat] + rank
    n_rows = 2 * S + N_EXPERTS * tm
    n_tiles = n_rows // tm
    token = jnp.arange(2 * S, dtype=jnp.int32) // 2
    src = jnp.zeros((n_rows,), jnp.int32).at[dest].set(token)
    n_used = tile_end[-1].astype(jnp.int32)
    tile_id = jnp.minimum(jnp.arange(n_tiles, dtype=jnp.int32), n_used - 1)
    tile_expert = jnp.sum((tile_end[None, :] <= tile_id[:, None]).astype(jnp.int32), axis=1)
    tile_expert = jnp.minimum(tile_expert, N_EXPERTS - 1)
    return src, weights, dest.astype(jnp.int32), tile_expert, n_used.reshape(1)


def _adaln(c, c_ctx, w, b):
    D = c.shape[-1]
    rows = jnp.zeros((8, D), F32).at[0].set(c[0]).at[1].set(c_ctx)
    m = matmul(jax.nn.silu(rows), w, 8, 2048)[:2] + b[None, :]
    return m.reshape(12, 1, D)


def _head_rms(o, gain, n_heads):
    T = o.shape[0]
    oh = o.reshape(T, n_heads, -1)
    y = oh * lax.rsqrt(jnp.mean(oh * oh, axis=-1, keepdims=True) + NORM_EPS) * gain
    return y.reshape(T, -1)


def _even_mixer(h, L, w_in, w_out, attn_sink, hgrn_norm, hgrn_lb):
    T = h.shape[0]
    S = T - L
    proj = matmul(h, jnp.concatenate([w_in[:, 1536:], w_in[:, :1536]], axis=1), 1280, 512)
    att = sink_attention(proj, 5, 24, 25, L, attn_sink, _rope_tables(S))
    att_c = sink_attention(proj, 5, 24, 25, L, attn_sink, None)
    lb = (hgrn_lb.reshape(1, -1),)
    qh, ih = (proj, 1024, 0), (proj, 1024, 1)
    o_rev = chunk_mixer(_hgrn_heads, [qh, ih, (proj, 1024, 3), lb], 8, 128, 128, L, True)
    hg = chunk_mixer(_hgrn_heads, [qh, ih, (proj, 1024, 2), lb], 8, 128, 128, L, False,
                     final=((proj, 1024, 4), o_rev, hgrn_norm))
    ycat = jnp.concatenate([jnp.concatenate([att_c, att], axis=0), hg], axis=-1)
    return matmul(ycat, w_out, 1280, 1024)


def _token_shift(p, L, mu_prev, mu_next):
    def one(s):
        zero = jnp.zeros_like(s[:1])
        prev = jnp.concatenate([zero, s[:-1]], axis=0)
        nxt = jnp.concatenate([s[1:], zero], axis=0)
        return s + mu_prev * (prev - s) + mu_next * (nxt - s)
    return jnp.concatenate([one(p[:L]), one(p[L:])], axis=0)


def _odd_mixer(h, L, w_in, w_out, gla_gate_up_f, gla_gate_up_b, gla_gate_bias_f, gla_gate_bias_b, gla_norm,
               mu_prev, mu_next, w0_f, w0_b, w2_f, w2_b, a0, a2, g2, k_k, k_a, r_k, ln_w, ln_b):
    T = h.shape[0]
    GO = 3104
    cols = lambda a, b: w_in[:, a:b]
    w_big = jnp.concatenate([cols(0, 2048), cols(2080, 3104), cols(GO, GO + 3072), cols(GO + 3360, GO + 3616)], axis=1)
    w_small = jnp.concatenate([cols(2048, 2080), cols(GO + 3072, GO + 3360),
                               jnp.zeros((w_in.shape[0], 64), F32)], axis=1)
    pb = matmul(h, w_big, 1280, 640)
    ps = matmul(h, w_small, 1280, 384)
    def gla_inputs(up, row0, bias):
        up_rows = jnp.zeros((ps.shape[1], up.shape[1]), F32).at[row0:row0 + up.shape[0]].set(up)
        return [(pb, 512, 0), (pb, 512, 1), (pb, 1024, 1), (ps, ps.shape[1], 0), (up_rows,), (bias.reshape(1, -1),)]
    o_rev = chunk_mixer(_gla_heads, gla_inputs(gla_gate_up_b, 16, gla_gate_bias_b), 4, 128, 256, L, True)
    gla = chunk_mixer(_gla_heads, gla_inputs(gla_gate_up_f, 0, gla_gate_bias_f), 4, 128, 256, L, False,
                      final=((pb, 1024, 2), o_rev, gla_norm))
    rw = rwkv7_branch(pb, ps, 1, 24, L, mu_prev, mu_next, (32, 128, 224), w0_f, w0_b, w2_f, w2_b,
                      a0, a2, g2, k_k, k_a, r_k, ln_w, ln_b)
    ycat = jnp.concatenate([gla[L:], rw[L:]], axis=-1)
    return matmul(ycat, w_out, 1024, 1024)


def kernel(x, c, ctx, c_ctx, hgrn_lb_logits, l0_ada_w, l0_ada_b, l0_norm_mix_pre, l0_norm_mix_post, l0_norm_ffn_pre, l0_norm_ffn_post, l0_w_in, l0_w_out, l0_attn_sink, l0_hgrn_norm, l0_ffn_w_gate, l0_ffn_w_up, l0_ffn_w_down, l1_ada_w, l1_ada_b, l1_norm_mix_pre, l1_norm_mix_post, l1_norm_ffn_pre, l1_norm_ffn_post, l1_w_in, l1_w_out, l1_gla_gate_up_f, l1_gla_gate_up_b, l1_gla_gate_bias_f, l1_gla_gate_bias_b, l1_gla_norm, l1_rwkv_mu_prev, l1_rwkv_mu_next, l1_rwkv_w0_f, l1_rwkv_w0_b, l1_rwkv_w2_f, l1_rwkv_w2_b, l1_rwkv_a0, l1_rwkv_a2, l1_rwkv_g2, l1_rwkv_k_k, l1_rwkv_k_a, l1_rwkv_r_k, l1_rwkv_ln_w, l1_rwkv_ln_b, l1_moe_router, l1_moe_w_gate, l1_moe_w_up, l1_moe_w_down):
    B, S, D = x.shape
    L = ctx.shape[1]
    assert B == 1
    T = L + S
    SH1, SC1, G1, SH2, SC2, G2 = range(6)
    xa = jnp.concatenate([ctx[0], x[0]], axis=0)
    hgrn_lb = jnp.cumsum(jax.nn.softmax(hgrn_lb_logits.astype(F32), axis=0), axis=0)
    m0 = _adaln(c, c_ctx, l0_ada_w, l0_ada_b)
    m1 = _adaln(c, c_ctx, l1_ada_w, l1_ada_b)

    h = normmod(xa, l0_norm_mix_pre, m0, SH1, SC1, L)
    y = _even_mixer(h, L, l0_w_in, l0_w_out, l0_attn_sink, l0_hgrn_norm, hgrn_lb[0])
    xa, h = resid_norm(xa, y, l0_norm_mix_post, m0, G1, L, nxt=(l0_norm_ffn_pre, m0, SH2, SC2))
    n_t = T // DENSE_TILE
    f = swiglu_ffn(h, l0_ffn_w_gate[None].astype(BF16), l0_ffn_w_up[None].astype(BF16),
                   l0_ffn_w_down[None].astype(BF16),
                   jnp.zeros((n_t,), jnp.int32), jnp.full((1,), n_t, jnp.int32), DENSE_TILE, 512)
    xa, h = resid_norm(xa, f, l0_norm_ffn_post, m0, G2, L, nxt=(l1_norm_mix_pre, m1, SH1, SC1))

    y = _odd_mixer(h, L, l1_w_in, l1_w_out, l1_gla_gate_up_f, l1_gla_gate_up_b, l1_gla_gate_bias_f,
                   l1_gla_gate_bias_b, l1_gla_norm, l1_rwkv_mu_prev, l1_rwkv_mu_next, l1_rwkv_w0_f, l1_rwkv_w0_b,
                   l1_rwkv_w2_f, l1_rwkv_w2_b, l1_rwkv_a0, l1_rwkv_a2, l1_rwkv_g2, l1_rwkv_k_k, l1_rwkv_k_a,
                   l1_rwkv_r_k, l1_rwkv_ln_w, l1_rwkv_ln_b)
    xl, h = resid_norm(xa[L:], y, l1_norm_mix_post, m1, G1, 0, nxt=(l1_norm_ffn_pre, m1, SH2, SC2))
    router = jnp.concatenate([l1_moe_router, jnp.zeros((D, 128 - N_EXPERTS), F32)], axis=1)
    logits = matmul(h, router, 1024, 128)[:, :N_EXPERTS]
    src, gate_w, dest, tile_expert, n_used = _route(logits, MOE_TILE)
    hs = gather_rows(h, src, TOK_BLOCK)
    ys = swiglu_ffn(hs, l1_moe_w_gate.astype(BF16), l1_moe_w_up.astype(BF16), l1_moe_w_down.astype(BF16),
                    tile_expert, n_used, MOE_TILE, 1024)
    out = combine_resid(xl, ys, dest, gate_w, l1_norm_ffn_post, m1, G2)
    return out[None]
```

```python
import functools

import jax
import jax.numpy as jnp
from jax import lax
from jax.experimental import pallas as pl
from jax.experimental.pallas import tpu as pltpu

F32 = jnp.float32
BF16 = jnp.bfloat16

NORM_EPS = 1e-6
CHUNK = 32
TOK_BLOCK = 256
ATT_BLOCK = 128
WINDOW = 128
HEAD_DIM = 128
ATT_HEADS = 8
ATT_KV_HEADS = 2
ROPE_THETA = 10000.0
GRID_W = 64
RWKV_N = 64
RWKV_HEADS = 16
RWKV_LN_EPS = 64e-5
RWKV_HEAD_BLOCK = 4
GLA_GATE_NORMALIZER = 16.0
N_EXPERTS = 8
MOE_TILE = 512
DENSE_TILE = 416
DENSE_FF_BLOCK = 1408
VMEM_LIMIT_BYTES = 56 * 1024 * 1024
LANES = 128
BF16_SUBLANES = 16
ROW_COPY_UNROLL = 8


def _params(*sem):
    return pltpu.CompilerParams(dimension_semantics=sem, vmem_limit_bytes=VMEM_LIMIT_BYTES)


def _mm_kernel(x_ref, w_ref, o_ref, wbf_ref):
    @pl.when(pl.program_id(1) == 0)
    def _():
        wbf_ref[...] = w_ref[...].astype(BF16)

    o_ref[...] = jnp.dot(x_ref[...].astype(BF16), wbf_ref[...],
                         preferred_element_type=F32).astype(o_ref.dtype)


def matmul(x, w, tm, tn, out_dtype=F32):
    M, K = x.shape
    N = w.shape[1]
    assert M % tm == 0 and N % tn == 0, (M, tm, N, tn)
    return pl.pallas_call(
        _mm_kernel,
        grid=(N // tn, M // tm),
        in_specs=[pl.BlockSpec((tm, K), lambda j, i: (i, 0)),
                  pl.BlockSpec((K, tn), lambda j, i: (0, j))],
        out_specs=pl.BlockSpec((tm, tn), lambda j, i: (i, j)),
        out_shape=jax.ShapeDtypeStruct((M, N), out_dtype),
        scratch_shapes=[pltpu.VMEM((K, tn), BF16)],
        compiler_params=_params("arbitrary", "arbitrary"),
        name="matmul",
    )(x, w)


def _rms(x, gain):
    ms = jnp.mean(x * x, axis=-1, keepdims=True)
    return x * lax.rsqrt(ms + NORM_EPS) * gain


def _normmod_kernel(x_ref, gain_ref, sc_ref, sh_ref, h_ref):
    y = _rms(x_ref[...], gain_ref[...])
    h_ref[...] = (y * (1.0 + sc_ref[0]) + sh_ref[0]).astype(h_ref.dtype)


def _mod_spec(D, slot, n_ctx_tiles):
    return pl.BlockSpec((1, 1, D), lambda i: (jnp.where(i < n_ctx_tiles, 6, 0) + slot, 0, 0))


def normmod(x, gain, mods, sh_slot, sc_slot, n_ctx_rows):
    R, D = x.shape
    tr = TOK_BLOCK
    assert R % tr == 0 and n_ctx_rows % tr == 0
    nct = n_ctx_rows // tr
    return pl.pallas_call(
        _normmod_kernel,
        grid=(R // tr,),
        in_specs=[pl.BlockSpec((tr, D), lambda i: (i, 0)),
                  pl.BlockSpec((1, D), lambda i: (0, 0)),
                  _mod_spec(D, sc_slot, nct), _mod_spec(D, sh_slot, nct)],
        out_specs=pl.BlockSpec((tr, D), lambda i: (i, 0)),
        out_shape=jax.ShapeDtypeStruct((R, D), BF16),
        compiler_params=_params("arbitrary"),
        name="normmod",
    )(x, gain.reshape(1, D), mods, mods)


def _resid_kernel(x_ref, y_ref, gpost_ref, g_ref, *rest, with_next):
    xn = x_ref[...] + g_ref[0] * _rms(y_ref[...], gpost_ref[...])
    if with_next:
        gpre_ref, sc_ref, sh_ref, xo_ref, h_ref = rest
        xo_ref[...] = xn
        h_ref[...] = (_rms(xn, gpre_ref[...]) * (1.0 + sc_ref[0]) + sh_ref[0]).astype(h_ref.dtype)
    else:
        (xo_ref,) = rest
        xo_ref[...] = xn


def resid_norm(x, y, gain_post, mods, g_slot, n_ctx_rows, nxt=None, x_row0=0):
    R, D = y.shape
    tr = TOK_BLOCK
    assert R % tr == 0 and n_ctx_rows % tr == 0 and x_row0 % tr == 0 and x.shape[0] == R + x_row0
    nct = n_ctx_rows // tr
    row = pl.BlockSpec((tr, D), lambda i: (i, 0))
    vec = pl.BlockSpec((1, D), lambda i: (0, 0))
    in_specs = [pl.BlockSpec((tr, D), lambda i: (i + x_row0 // tr, 0)), row, vec, _mod_spec(D, g_slot, nct)]
    args = [x, y, gain_post.reshape(1, D), mods]
    out_specs = [row]
    out_shape = [jax.ShapeDtypeStruct((R, D), F32)]
    if nxt is not None:
        gain_pre, mods_n, sh_slot, sc_slot = nxt
        in_specs += [vec, _mod_spec(D, sc_slot, nct), _mod_spec(D, sh_slot, nct)]
        args += [gain_pre.reshape(1, D), mods_n, mods_n]
        out_specs.append(row)
        out_shape.append(jax.ShapeDtypeStruct((R, D), BF16))
    out = pl.pallas_call(
        functools.partial(_resid_kernel, with_next=nxt is not None),
        grid=(R // tr,),
        in_specs=in_specs, out_specs=out_specs, out_shape=out_shape,
        compiler_params=_params("arbitrary"),
        name="resid_norm",
    )(*args)
    return out if nxt is not None else out[0]


def _ffn_kernel(te_ref, nu_ref, x_ref, wg_ref, wu_ref, wd_ref, o_ref):
    i, j = pl.program_id(0), pl.program_id(1)

    @pl.when(i < nu_ref[0])
    def _():
        x = x_ref[...]
        g = jnp.dot(x, wg_ref[0], preferred_element_type=F32)
        u = jnp.dot(x, wu_ref[0], preferred_element_type=F32)
        hid = (g * jax.nn.sigmoid(g) * u).astype(BF16)
        part = jnp.dot(hid, wd_ref[0], preferred_element_type=F32)

        @pl.when(j == 0)
        def _():
            o_ref[...] = part

        @pl.when(j > 0)
        def _():
            o_ref[...] += part

    @pl.when(jnp.logical_and(i >= nu_ref[0], j == 0))
    def _():
        o_ref[...] = jnp.zeros_like(o_ref)


def swiglu_ffn(x, w_gate, w_up, w_down, tile_expert, n_used, tm, tf):
    R, D = x.shape
    E, _, F = w_gate.shape
    assert R % tm == 0 and F % tf == 0 and w_gate.dtype == BF16 and x.dtype == BF16
    nf = F // tf

    def fblk(i, j, te, nu):
        return jnp.where(i < nu[0], j, nf - 1)

    in_specs = [pl.BlockSpec((tm, D), lambda i, j, te, nu: (i, 0)),
                pl.BlockSpec((1, D, tf), lambda i, j, te, nu: (te[i], 0, fblk(i, j, te, nu))),
                pl.BlockSpec((1, D, tf), lambda i, j, te, nu: (te[i], 0, fblk(i, j, te, nu))),
                pl.BlockSpec((1, tf, D), lambda i, j, te, nu: (te[i], fblk(i, j, te, nu), 0))]
    args = [x, w_gate, w_up, w_down]
    return pl.pallas_call(
        _ffn_kernel,
        grid_spec=pltpu.PrefetchScalarGridSpec(
            num_scalar_prefetch=2,
            grid=(R // tm, nf),
            in_specs=in_specs,
            out_specs=pl.BlockSpec((tm, D), lambda i, j, te, nu: (i, 0)),
        ),
        out_shape=jax.ShapeDtypeStruct((R, D), F32),
        compiler_params=_params("arbitrary", "arbitrary"),
        name="swiglu_ffn",
    )(tile_expert, n_used, *args)


def _rope(x, c, sg):
    lane = lax.broadcasted_iota(jnp.int32, x.shape, 1)
    first_half = jnp.bitwise_and(lane, HEAD_DIM // 2 - 1) < HEAD_DIM // 4
    partner = jnp.where(first_half, pltpu.roll(x, HEAD_DIM - HEAD_DIM // 4, 1), pltpu.roll(x, HEAD_DIM // 4, 1))
    return x * c + partner * sg


def _attn_kernel(sink_ref, q_ref, *rest, has_window, seq_len):
    if has_window:
        (kp_ref, ko_ref, kn_ref, vp_ref, vo_ref, vn_ref, cp_ref, co_ref, cn_ref, sp_ref, so_ref, sn_ref,
         kc_ref, vc_ref, o_ref) = rest
    else:
        kc_ref, vc_ref, o_ref = rest
    i = pl.program_id(0)
    BQ = q_ref.shape[0]
    L = kc_ref.shape[0]
    G = ATT_KV_HEADS
    R = ATT_HEADS // G
    n_win = 3 * BQ if has_window else 0
    nk = n_win + L
    if has_window:
        row = lax.broadcasted_iota(jnp.int32, (R * BQ, nk), 0)
        col = lax.broadcasted_iota(jnp.int32, (R * BQ, nk), 1)
        qi = jnp.bitwise_and(row, BQ - 1)
        kpos = i * BQ + col - BQ
        in_band = jnp.abs(col - BQ - qi) <= WINDOW
        in_seq = jnp.logical_and(kpos >= 0, kpos < seq_len)
        valid = jnp.logical_or(col >= n_win, jnp.logical_and(in_band, in_seq))
    for g in range(G):
        cs = slice(g * HEAD_DIM, (g + 1) * HEAD_DIM)
        q_heads = [q_ref[:, (g * R + r) * HEAD_DIM:(g * R + r + 1) * HEAD_DIM] for r in range(R)]
        if has_window:
            q_heads = [_rope(q, co_ref[...], so_ref[...]) for q in q_heads]
            k_all = jnp.concatenate([_rope(kp_ref[:, cs], cp_ref[...], sp_ref[...]),
                                     _rope(ko_ref[:, cs], co_ref[...], so_ref[...]),
                                     _rope(kn_ref[:, cs], cn_ref[...], sn_ref[...]), kc_ref[:, cs]], axis=0)
            v_all = jnp.concatenate([vp_ref[:, cs], vo_ref[:, cs], vn_ref[:, cs], vc_ref[:, cs]], axis=0)
        else:
            k_all, v_all = kc_ref[:, cs], vc_ref[:, cs]
        q_g = (jnp.concatenate(q_heads, axis=0) * (HEAD_DIM ** -0.5)).astype(BF16)
        s = lax.dot_general(q_g, k_all.astype(BF16), (((1,), (1,)), ((), ())),
                            preferred_element_type=F32)
        if has_window:
            s = jnp.where(valid, s, -jnp.inf)
        sink = jnp.concatenate([jnp.full((BQ, 1), sink_ref[g * R + r], F32) for r in range(R)], axis=0)
        m = jnp.maximum(sink, jnp.max(s, axis=-1, keepdims=True))
        p = jnp.exp(s - m)
        denom = jnp.exp(sink - m) + jnp.sum(p, axis=-1, keepdims=True)
        o = jnp.dot(p.astype(BF16), v_all.astype(BF16), preferred_element_type=F32) / denom
        for r in range(R):
            h = g * R + r
            o_ref[:, h * HEAD_DIM:(h + 1) * HEAD_DIM] = o[r * BQ:(r + 1) * BQ].astype(o_ref.dtype)


def sink_attention(proj, q_col, k_col, v_col, L, sink, rope):
    QW, KW = ATT_HEADS * HEAD_DIM, ATT_KV_HEADS * HEAD_DIM
    has_window = rope is not None
    S = proj.shape[0] - L if has_window else L
    BQ = ATT_BLOCK if has_window else L
    nb = S // BQ
    assert S % BQ == 0 and L % BQ == 0
    r0 = L // BQ if has_window else 0
    in_specs = [pl.BlockSpec(memory_space=pltpu.SMEM), pl.BlockSpec((BQ, QW), lambda i: (r0 + i, q_col))]
    args = [sink, proj]
    if has_window:
        prev = lambda i: jnp.maximum(i - 1, 0)
        nxt = lambda i: jnp.minimum(i + 1, nb - 1)
        for col in (k_col, v_col):
            in_specs += [pl.BlockSpec((BQ, KW), lambda i, col=col: (r0 + prev(i), col)),
                         pl.BlockSpec((BQ, KW), lambda i, col=col: (r0 + i, col)),
                         pl.BlockSpec((BQ, KW), lambda i, col=col: (r0 + nxt(i), col))]
            args += [proj] * 3
        for tab in rope:
            in_specs += [pl.BlockSpec((BQ, HEAD_DIM), lambda i: (prev(i), 0)),
                         pl.BlockSpec((BQ, HEAD_DIM), lambda i: (i, 0)),
                         pl.BlockSpec((BQ, HEAD_DIM), lambda i: (nxt(i), 0))]
            args += [tab] * 3
    in_specs += [pl.BlockSpec((L, KW), lambda i: (0, k_col)), pl.BlockSpec((L, KW), lambda i: (0, v_col))]
    args += [proj, proj]
    return pl.pallas_call(
        functools.partial(_attn_kernel, has_window=has_window, seq_len=S),
        grid=(nb,),
        in_specs=in_specs,
        out_specs=pl.BlockSpec((BQ, QW), lambda i: (i, 0)),
        out_shape=jax.ShapeDtypeStruct((S, QW), BF16),
        compiler_params=_params("arbitrary"),
        name="sink_attention",
    )(*args)


def _rope_tables(n_tokens):
    n_freq = HEAD_DIM // 4
    t = jnp.arange(n_tokens)
    row = (t // GRID_W).astype(F32)
    col = (t % GRID_W).astype(F32)
    inv = ROPE_THETA ** (-jnp.arange(n_freq, dtype=F32) / n_freq)
    ar, ac = row[:, None] * inv, col[:, None] * inv
    c = jnp.concatenate([jnp.cos(ar), jnp.cos(ar), jnp.cos(ac), jnp.cos(ac)], axis=1)
    sg = jnp.concatenate([-jnp.sin(ar), jnp.sin(ar), -jnp.sin(ac), jnp.sin(ac)], axis=1)
    return c, sg


CHUNK_SHIFT = CHUNK.bit_length() - 1
assert 1 << CHUNK_SHIFT == CHUNK


def _chunk_masks(tb, reverse, strict):
    r = lax.broadcasted_iota(jnp.int32, (tb, tb), 0)
    c = lax.broadcasted_iota(jnp.int32, (tb, tb), 1)
    same = jnp.right_shift(r, CHUNK_SHIFT) == jnp.right_shift(c, CHUNK_SHIFT)
    if reverse:
        tri = (c > r) if strict else (c >= r)
    else:
        tri = (c < r) if strict else (c <= r)
    return same, jnp.logical_and(same, tri)


def _chunk_sums(ld, same, tri):
    tb = ld.shape[0]
    sel = jnp.concatenate([jnp.where(tri, 1.0, 0.0), jnp.where(same, 1.0, 0.0)], axis=0).astype(BF16)
    hi = ld.astype(BF16)
    rest = ld - hi.astype(F32)
    mid = rest.astype(BF16)
    lo = (rest - mid.astype(F32)).astype(BF16)
    dot = lambda p: jnp.dot(sel, p, preferred_element_type=F32)
    g = (dot(lo) + dot(mid)) + dot(hi)
    return g[:tb], g[tb:]


def _dot_tn(a, b):
    return lax.dot_general(a, b, (((0,), (0,)), ((), ())), preferred_element_type=F32)


def _dot_nt(a, b):
    return lax.dot_general(a, b, (((1,), (1,)), ((), ())), preferred_element_type=F32)


def _chunkrec_core(load_head, n_heads, tb, V, reverse, st_ref, dst_ref):
    @pl.when(pl.program_id(0) == 0)
    def _():
        st_ref[...] = jnp.zeros_like(st_ref)

    same, tri = _chunk_masks(tb, reverse, strict=False)
    loaded = [load_head(h) for h in range(n_heads)]
    K = loaded[0][3].shape[1]
    g_cum_all, g_tot_all = _chunk_sums(jnp.concatenate([ld for _, _, _, ld in loaded], axis=1), same, tri)
    per_head = []
    for h in range(n_heads):
        q, k, v, _ = loaded[h]
        g_cum, g_tot = g_cum_all[:, h * K:(h + 1) * K], g_tot_all[:, h * K:(h + 1) * K]
        q_dec = (q * jnp.exp(g_cum)).astype(BF16)
        k_inv = (k * jnp.exp(-g_cum)).astype(BF16)
        k_tail = k * jnp.exp(g_tot - g_cum)
        a = jnp.where(tri, _dot_nt(q_dec, k_inv), 0.0)
        o_intra = jnp.dot(a.astype(BF16), v.astype(BF16), preferred_element_type=F32)
        per_head.append((q_dec, k_tail, v, g_tot, o_intra))
    n_chunks = tb // CHUNK
    order = range(n_chunks - 1, -1, -1) if reverse else range(n_chunks)
    sts = [st_ref[h] for h in range(n_heads)]
    for c in order:
        sl = slice(c * CHUNK, (c + 1) * CHUNK)
        for h in range(n_heads):
            q_dec, k_tail, v, g_tot, o_intra = per_head[h]
            st = sts[h]
            dst_ref[sl, h * V:(h + 1) * V] = o_intra[sl] + _dot_nt(q_dec[sl], st.astype(BF16))
            d = jnp.exp(g_tot[c * CHUNK:c * CHUNK + 1, :])
            sts[h] = st * d + _dot_tn(v[sl], k_tail[sl])
    for h in range(n_heads):
        st_ref[h] = sts[h]


def _gated_head_norm(o, gain, gate_raw):
    y = o * lax.rsqrt(jnp.mean(o * o, axis=-1, keepdims=True) + NORM_EPS) * gain
    return y * (gate_raw * jax.nn.sigmoid(gate_raw))


def _chunk_mixer_kernel(*refs, reverse, n_heads, K, V, load_heads):
    if reverse:
        *in_refs, o_ref, st_ref = refs
        dst_ref = o_ref
    else:
        *in_refs, gate_ref, orev_ref, gain_ref, o_ref, st_ref, dst_ref = refs
    tb = o_ref.shape[0]
    _chunkrec_core(load_heads(*in_refs), n_heads, tb, V, reverse, st_ref, dst_ref)
    if not reverse:
        for h in range(n_heads):
            vs = slice(h * V, (h + 1) * V)
            o_ref[:, vs] = _gated_head_norm(dst_ref[:, vs] + orev_ref[:, vs], gain_ref[...],
                                            gate_ref[:, vs]).astype(o_ref.dtype)


def _hgrn_heads(qh_ref, ih_ref, fr_ref, lb_ref):
    def load(h):
        cs = slice(h * 128, (h + 1) * 128)
        lb = lb_ref[:, cs]
        f = lb + (1.0 - lb) * jax.nn.sigmoid(fr_ref[:, cs])
        qh = qh_ref[:, cs]
        return qh * jax.nn.sigmoid(qh), 1.0 - f, ih_ref[:, cs], jnp.log(f)
    return load


def _gla_heads(gq_ref, gk_ref, gv_ref, gd_ref, up_ref, bias_ref):
    x = jnp.dot(gd_ref[...].astype(BF16), up_ref[...].astype(BF16), preferred_element_type=F32) + bias_ref[...]
    lg = (jnp.minimum(x, 0.0) - jnp.log(1.0 + jnp.exp(-jnp.abs(x)))) * (1.0 / GLA_GATE_NORMALIZER)

    def load(h):
        ks, vs = slice(h * 128, (h + 1) * 128), slice(h * 256, (h + 1) * 256)
        return gq_ref[:, ks] * (128 ** -0.5), gk_ref[:, ks], gv_ref[:, vs], lg[:, ks]
    return load


def _seq_block_index(nblk, nctx, reverse):
    if not reverse:
        return lambda t: t
    return lambda t: jnp.where(t < nctx, nctx - 1 - t, nblk - 1 - (t - nctx))


def chunk_mixer(load_heads, inputs, n_heads, K, V, n_ctx_rows, reverse, final=None):
    T = inputs[0][0].shape[0]
    tb = TOK_BLOCK
    assert T % tb == 0 and n_ctx_rows % tb == 0
    nblk, nctx = T // tb, n_ctx_rows // tb
    blk = _seq_block_index(nblk, nctx, reverse)

    def spec(item):
        if len(item) == 1:
            return pl.BlockSpec(item[0].shape, lambda t: (0, 0))
        _, width, cb = item
        return pl.BlockSpec((tb, width), lambda t: (blk(t), cb))

    HV = n_heads * V
    ospec = pl.BlockSpec((tb, HV), lambda t: (blk(t), 0))
    scratch = [pltpu.VMEM((n_heads, V, K), F32)]
    if not reverse:
        gate, o_rev, gain = final
        inputs = list(inputs) + [gate, (o_rev, HV, 0), (gain.reshape(1, V),)]
        scratch.append(pltpu.VMEM((tb, HV), F32))
    return pl.pallas_call(
        functools.partial(_chunk_mixer_kernel, reverse=reverse, n_heads=n_heads, K=K, V=V, load_heads=load_heads),
        grid=(nblk,),
        in_specs=[spec(it) for it in inputs],
        out_specs=ospec,
        out_shape=jax.ShapeDtypeStruct((T, HV), F32 if reverse else BF16),
        scratch_shapes=scratch,
        compiler_params=_params("arbitrary"),
        name="chunk_mixer",
    )(*[it[0] for it in inputs])


def _mm_bf(a, b):
    return jnp.dot(a.astype(BF16), b.astype(BF16), preferred_element_type=F32)


def _softplus(x):
    return jnp.maximum(x, 0.0) + jnp.log(1.0 + jnp.exp(-jnp.abs(x)))


def _head_sums(x, bd_ref):
    hi = x.astype(BF16)
    lo = (x - hi.astype(F32)).astype(BF16)
    bd = bd_ref[...]
    W = bd.shape[0]
    groups = []
    for g in range(x.shape[1] // W):
        gs = slice(g * W, (g + 1) * W)
        groups.append(jnp.dot(lo[:, gs], bd, preferred_element_type=F32)
                      + jnp.dot(hi[:, gs], bd, preferred_element_type=F32))
    return jnp.concatenate(groups, axis=1)


def _rwkv_prep_kernel(p_ref, pp_ref, pn_ref, g_ref, gp_ref, gn_ref, s_ref, sp_ref, sn_ref,
                      mup_ref, mun_ref, mugp_ref, mugn_ref, musp_ref, musn_ref,
                      w2f_ref, w2b_ref, a2_ref, g2_ref, w0f_ref, w0b_ref, a0_ref, kk_ref, ka_ref, rk_ref, bd_ref,
                      r_out, k_out, v_out, kk_out, b_out, lwf_out, lwb_out, go_out, bon_out,
                      gcf_out, gtf_out, gcb_out, gtb_out, *, n_ctx_blocks):
    t, nblk = pl.program_id(0), pl.num_programs(0)
    tb = p_ref.shape[0]
    C = RWKV_HEADS * RWKV_N
    keep_prev = jnp.where(jnp.logical_or(t == 0, t == n_ctx_blocks), 0.0, 1.0)
    keep_next = jnp.where(jnp.logical_or(t == n_ctx_blocks - 1, t == nblk - 1), 0.0, 1.0)

    def shifted(x_ref, xp_ref, xn_ref, mu_p_ref, mu_n_ref):
        x = x_ref[...]
        rows = lax.broadcasted_iota(jnp.int32, x.shape, 0)
        prev = jnp.where(rows == 0, xp_ref[7:8, :] * keep_prev, pltpu.roll(x, 1, 0))
        nxt = jnp.where(rows == tb - 1, xn_ref[0:1, :] * keep_next, pltpu.roll(x, tb - 1, 0))
        return x + mu_p_ref[...] * (prev - x) + mu_n_ref[...] * (nxt - x)

    rkv = shifted(p_ref, pp_ref, pn_ref, mup_ref, mun_ref)
    rr, rk, rv = rkv[:, :C], rkv[:, C:2 * C], rkv[:, 2 * C:]
    low = shifted(s_ref, sp_ref, sn_ref, musp_ref, musn_ref)
    gd = shifted(g_ref, gp_ref, gn_ref, mugp_ref, mugn_ref)
    tl = jnp.tanh(low).astype(BF16)
    for w0_ref, w2_ref, lw_out, gc_out, gt_out, reverse in ((w0f_ref, w2f_ref, lwf_out, gcf_out, gtf_out, False),
                                                            (w0b_ref, w2b_ref, lwb_out, gcb_out, gtb_out, True)):
        lw = -jnp.exp(-_softplus(-(w0_ref[...] + jnp.dot(tl, w2_ref[...], preferred_element_type=F32))) - 0.5)
        lw_out[...] = lw
        same, tri = _chunk_masks(tb, reverse, strict=False)
        gc_out[...], gt_out[...] = _chunk_sums(lw, same, tri)
    a_sig = jax.nn.sigmoid(a0_ref[...] + jnp.dot(low.astype(BF16), a2_ref[...], preferred_element_type=F32))
    go_out[...] = jnp.dot(jax.nn.sigmoid(gd).astype(BF16), g2_ref[...], preferred_element_type=F32)
    kk = rk * kk_ref[...]
    kk = kk * lax.rsqrt(_head_sums(kk * kk, bd_ref) + 1e-12)
    k_mod = rk * (1.0 + (a_sig - 1.0) * ka_ref[...])
    r_out[...] = rr
    k_out[...] = k_mod
    v_out[...] = rv
    kk_out[...] = kk
    b_out[...] = kk * a_sig
    bon_out[...] = _head_sums(rr * k_mod * rk_ref[...], bd_ref) * rv


def _rwkv_pre_kernel(r_ref, k_ref, v_ref, kk_ref, b_ref, lw_ref, gc_ref, gt_ref, w32_ref,
                     qp_ref, ol_ref, plt_ref, zt_ref, w16_ref, *, reverse):
    w16_ref[...] = w32_ref[...].astype(BF16)
    N = RWKV_N
    tb = r_ref.shape[0]
    hb = r_ref.shape[1] // N
    same, tri_incl = _chunk_masks(tb, reverse, strict=False)
    _, tri_strict = _chunk_masks(tb, reverse, strict=True)
    xs, lps, rest = [], [], []
    for h in range(hb):
        hs = slice(h * N, (h + 1) * N)
        r, k, v, b, lw, g_cum, g_tot = (ref[:, hs] for ref in (r_ref, k_ref, v_ref, b_ref, lw_ref, gc_ref, gt_ref))
        a = -kk_ref[:, hs]
        e_neg = jnp.exp(-g_cum)
        e_tail = jnp.exp(g_tot - g_cum)
        a_t = (a * jnp.exp(g_cum - lw)).astype(BF16)
        r_t = (r * jnp.exp(g_cum)).astype(BF16)
        b_t = (b * e_neg).astype(BF16)
        k_t = (k * e_neg).astype(BF16)
        a_ab = jnp.where(tri_strict, _dot_nt(a_t, b_t), 0.0)
        a_ak = jnp.where(tri_strict, _dot_nt(a_t, k_t), 0.0)
        a_rb = jnp.where(tri_incl, _dot_nt(r_t, b_t), 0.0)
        a_rk = jnp.where(tri_incl, _dot_nt(r_t, k_t), 0.0)
        xs.append(jnp.concatenate([a_t.astype(F32), _mm_bf(a_ak, v)], axis=1))
        lps.append(a_ab)
        rest.append((v, b * e_tail, k * e_tail, a_rb,
                     jnp.concatenate([r_t.astype(F32), _mm_bf(a_rk, v)], axis=1)))
    for j in range(CHUNK_SHIFT):
        xs = [x + _mm_bf(lp, x) for x, lp in zip(xs, lps)]
        if j < CHUNK_SHIFT - 1:
            lps = [_mm_bf(lp, lp) for lp in lps]
    for h in range(hb):
        v, b_h, k_h, a_rb, qo0 = rest[h]
        x = xs[h]
        qo = qo0 + _mm_bf(a_rb, x)
        qp_ref[:, h * N:(h + 1) * N] = qo[:, :N]
        ol_ref[:, h * N:(h + 1) * N] = qo[:, N:]
        w, uloc = x[:, :N], x[:, N:]
        for c in range(tb // CHUNK):
            sl = slice(c * CHUNK, (c + 1) * CHUNK)
            plt_ref[h, c] = _dot_tn(w[sl], b_h[sl])
            zt_ref[h, c] = _dot_tn(uloc[sl], b_h[sl]) + _dot_tn(v[sl], k_h[sl])


def _rwkv_scan_kernel(qp_ref, ol_ref, plt_ref, zt_ref, gt_ref, *rest, reverse):
    if reverse:
        o_ref, st_ref = rest
        dst_ref = o_ref
    else:
        orev_ref, go_ref, bon_ref, lnw_ref, lnb_ref, o_ref, st_ref, dst_ref = rest

    @pl.when(pl.program_id(0) == 0)
    def _():
        st_ref[...] = jnp.zeros_like(st_ref)

    N = RWKV_N
    tb = qp_ref.shape[0]
    n_heads = qp_ref.shape[1] // N
    n_chunks = tb // CHUNK
    order = range(n_chunks - 1, -1, -1) if reverse else range(n_chunks)
    sts = [st_ref[h] for h in range(n_heads)]
    for c in order:
        sl = slice(c * CHUNK, (c + 1) * CHUNK)
        for h in range(n_heads):
            hs = slice(h * N, (h + 1) * N)
            st = sts[h]
            dst_ref[sl, hs] = _dot_nt(qp_ref[sl, hs].astype(BF16), st.astype(BF16)) + ol_ref[sl, hs]
            d = jnp.exp(gt_ref[c * CHUNK:c * CHUNK + 1, hs])
            sts[h] = st * d + _mm_bf(st, plt_ref[h, c]) + zt_ref[h, c]
    for h in range(n_heads):
        st_ref[h] = sts[h]
    if not reverse:
        for h in range(n_heads):
            hs = slice(h * N, (h + 1) * N)
            o = dst_ref[:, hs] + orev_ref[:, hs]
            mu = jnp.mean(o, axis=-1, keepdims=True)
            var = jnp.mean(jnp.square(o - mu), axis=-1, keepdims=True)
            on = (o - mu) * lax.rsqrt(var + RWKV_LN_EPS) * lnw_ref[:, hs] + lnb_ref[:, hs]
            o_ref[:, hs] = ((on + bon_ref[:, hs]) * go_ref[:, hs]).astype(o_ref.dtype)


def rwkv7_branch(pb, ps, rkv_col, gd_col, n_ctx_rows, mu_prev, mu_next, low_cols, w0_f, w0_b, w2_f, w2_b,
                 a0, a2, g2, k_k, k_a, r_k, ln_w, ln_b, to_bf16):
    T = pb.shape[0]
    H, N = RWKV_HEADS, RWKV_N
    C = H * N
    tb = TOK_BLOCK
    assert T % tb == 0 and n_ctx_rows % tb == 0
    nblk, nctx = T // tb, n_ctx_rows // tb
    ncb = tb // CHUNK
    GW, SW, R = g2.shape[0], ps.shape[1], w2_f.shape[0]
    wf_col, wb_col, ad_col = low_cols
    s0 = 3 * C
    row = lambda v: v.reshape(1, -1)

    def padded_rows(w, r0):
        return jnp.zeros((SW, C), F32).at[r0:r0 + w.shape[0]].set(w).astype(BF16)

    def mu_low(mu):
        return jnp.zeros((1, SW), F32).at[0, wf_col:wf_col + 3 * R].set(mu[s0:s0 + 3 * R])

    bdw = 2 * LANES
    bd = (jnp.arange(bdw)[:, None] // N == jnp.arange(bdw)[None, :] // N).astype(BF16)
    r8 = tb // 8

    def halo(width, col):
        return (pl.BlockSpec((tb, width), lambda t: (t, col)),
                pl.BlockSpec((8, width), lambda t: (jnp.maximum(t * r8 - 1, 0), col)),
                pl.BlockSpec((8, width), lambda t: (jnp.minimum((t + 1) * r8, T // 8 - 1), col)))

    whole = lambda a: pl.BlockSpec(a.shape, lambda t: (0, 0))
    consts = [row(mu_prev[:s0]), row(mu_next[:s0]), row(mu_prev[s0 + 3 * R:]), row(mu_next[s0 + 3 * R:]),
              mu_low(mu_prev), mu_low(mu_next),
              padded_rows(w2_f, wf_col), padded_rows(w2_b, wb_col), padded_rows(a2, ad_col), g2.astype(BF16),
              row(w0_f), row(w0_b), row(a0), row(k_k), row(k_a), row(r_k), bd]
    tokC = pl.BlockSpec((tb, C), lambda t: (t, 0))
    shapeC = jax.ShapeDtypeStruct((T, C), F32)
    r, k, v, kk, b, lw_f, lw_b, g_out, bonus, gc_f, gt_f, gc_b, gt_b = pl.pallas_call(
        functools.partial(_rwkv_prep_kernel, n_ctx_blocks=nctx),
        grid=(nblk,),
        in_specs=[*halo(3 * C, rkv_col), *halo(GW, gd_col), *halo(SW, 0), *[whole(a) for a in consts]],
        out_specs=[tokC] * 13,
        out_shape=[shapeC] * 13,
        compiler_params=_params("arbitrary"),
        name="rwkv_prep",
    )(pb, pb, pb, pb, pb, pb, ps, ps, ps, *consts)

    hb = RWKV_HEAD_BLOCK
    tok = pl.BlockSpec((tb, hb * N), lambda h, t: (t, h))
    mat = pl.BlockSpec((hb, ncb, N, N), lambda h, t: (h, t, 0, 0))
    mat_shape = jax.ShapeDtypeStruct((H, T // CHUNK, N, N), F32)
    o_rev = None
    cast_done = []
    for lw, gc, gt, w32, reverse in ((lw_b, gc_b, gt_b, to_bf16[0], True), (lw_f, gc_f, gt_f, to_bf16[1], False)):
        wr, wc = w32.shape
        rows = BF16_SUBLANES * pl.cdiv(wr, BF16_SUBLANES * (H // hb) * nblk)
        assert wr % rows == 0
        wspec = pl.BlockSpec((rows, wc), lambda h, t: (jnp.minimum(h * nblk + t, wr // rows - 1), 0))
        qp, ol, plt, zt, w16 = pl.pallas_call(
            functools.partial(_rwkv_pre_kernel, reverse=reverse),
            grid=(H // hb, nblk),
            in_specs=[tok] * 8 + [wspec],
            out_specs=[tok, tok, mat, mat, wspec],
            out_shape=[shapeC, shapeC, mat_shape, mat_shape, jax.ShapeDtypeStruct((wr, wc), BF16)],
            compiler_params=_params("arbitrary", "arbitrary"),
            name="rwkv_pre",
        )(r, k, v, kk, b, lw, gc, gt, w32)
        cast_done.append(w16)
        blk = _seq_block_index(nblk, nctx, reverse)
        tok_all = pl.BlockSpec((tb, C), lambda t: (blk(t), 0))
        mat_all = pl.BlockSpec((H, ncb, N, N), lambda t: (0, blk(t), 0, 0))
        in_specs = [tok_all, tok_all, mat_all, mat_all, tok_all]
        args = [qp, ol, plt, zt, gt]
        scratch = [pltpu.VMEM((H, N, N), F32)]
        if not reverse:
            in_specs += [tok_all, tok_all, tok_all, whole(row(ln_w)), whole(row(ln_b))]
            args += [o_rev, g_out, bonus, row(ln_w), row(ln_b)]
            scratch.append(pltpu.VMEM((tb, C), F32))
        out = pl.pallas_call(
            functools.partial(_rwkv_scan_kernel, reverse=reverse),
            grid=(nblk,),
            in_specs=in_specs,
            out_specs=tok_all,
            out_shape=jax.ShapeDtypeStruct((T, C), F32 if reverse else BF16),
            scratch_shapes=scratch,
            compiler_params=_params("arbitrary"),
            name="rwkv_scan",
        )(*args)
        if reverse:
            o_rev = out
    return out, cast_done


def _row_copy(src_hbm, src_row, dst_ref, dst_row, sem):
    return pltpu.make_async_copy(src_hbm.at[pl.ds(src_row, 1)], dst_ref.at[pl.ds(dst_row, 1)], sem)


def _gather_kernel(src_ref, nxt_ref, x_hbm, o_ref, buf, sems):
    i, n_steps = pl.program_id(0), pl.num_programs(0)
    n = o_ref.shape[0]

    def issue(idx_ref, slot):
        def body(r, carry):
            for u in range(2):
                row = 2 * r + u
                _row_copy(x_hbm, idx_ref[0, 0, row], buf.at[slot], row, sems.at[slot]).start(priority=u)
            return carry
        lax.fori_loop(0, n // 2, body, 0, unroll=ROW_COPY_UNROLL)

    @pl.when(i == 0)
    def _():
        issue(src_ref, 0)

    @pl.when(i + 1 < n_steps)
    def _():
        issue(nxt_ref, (i + 1) % 2)

    slot = i % 2

    pltpu.make_async_copy(x_hbm.at[pl.ds(0, n)], buf.at[slot], sems.at[slot]).wait()
    o_ref[...] = buf[slot].astype(o_ref.dtype)


def gather_rows(x, src, tg):
    S, D = x.shape
    P = src.shape[0]
    sub = D // LANES
    assert P % tg == 0 and x.dtype == BF16 and sub == BF16_SUBLANES
    n_steps = P // tg
    src3 = src.reshape(n_steps, 1, tg)
    out = pl.pallas_call(
        _gather_kernel,
        grid=(n_steps,),
        in_specs=[pl.BlockSpec((1, 1, tg), lambda i: (i, 0, 0), memory_space=pltpu.SMEM),
                  pl.BlockSpec((1, 1, tg), lambda i: (jnp.minimum(i + 1, n_steps - 1), 0, 0),
                               memory_space=pltpu.SMEM),
                  pl.BlockSpec(memory_space=pl.ANY)],
        out_specs=pl.BlockSpec((tg, sub, LANES), lambda i: (i, 0, 0)),
        out_shape=jax.ShapeDtypeStruct((P, sub, LANES), BF16),
        scratch_shapes=[pltpu.VMEM((2, tg, sub, LANES), BF16), pltpu.SemaphoreType.DMA((2,))],
        compiler_params=_params("arbitrary"),
        name="gather_rows",
    )(src3, src3, x.reshape(S, sub, LANES))
    return out.reshape(P, D)


def _combine_kernel(pos_ref, nxt_ref, x_ref, w_ref, ys_hbm, gpost_ref, g_ref, o_ref, buf, sems):
    i, n_steps = pl.program_id(0), pl.num_programs(0)
    n = x_ref.shape[0]

    def issue(idx_ref, slot):
        def body(r, carry):
            for k in range(2):
                _row_copy(ys_hbm, idx_ref[0, 0, 2 * r + k], buf.at[slot, k], r, sems.at[slot, k]).start(priority=k)
            return carry
        lax.fori_loop(0, n, body, 0, unroll=ROW_COPY_UNROLL)

    @pl.when(i == 0)
    def _():
        issue(pos_ref, 0)

    @pl.when(i + 1 < n_steps)
    def _():
        issue(nxt_ref, (i + 1) % 2)

    slot = i % 2

    for k in range(2):
        pltpu.make_async_copy(ys_hbm.at[pl.ds(0, n)], buf.at[slot, k], sems.at[slot, k]).wait()
    w = w_ref[...]
    f = w[:, 0:1] * buf[slot, 0] + w[:, 1:2] * buf[slot, 1]
    o_ref[...] = x_ref[...] + g_ref[0] * _rms(f, gpost_ref[...])


def combine_resid(x, ys, pos, weights, gain_post, mods, g_slot):
    S, D = x.shape
    tc = TOK_BLOCK
    assert S % tc == 0
    n_steps = S // tc
    row = pl.BlockSpec((tc, D), lambda i: (i, 0))
    pos3 = pos.reshape(n_steps, 1, 2 * tc)
    return pl.pallas_call(
        _combine_kernel,
        grid=(n_steps,),
        in_specs=[pl.BlockSpec((1, 1, 2 * tc), lambda i: (i, 0, 0), memory_space=pltpu.SMEM),
                  pl.BlockSpec((1, 1, 2 * tc), lambda i: (jnp.minimum(i + 1, n_steps - 1), 0, 0),
                               memory_space=pltpu.SMEM),
                  row, pl.BlockSpec((tc, 2), lambda i: (i, 0)), pl.BlockSpec(memory_space=pl.ANY),
                  pl.BlockSpec((1, D), lambda i: (0, 0)), _mod_spec(D, g_slot, 0)],
        out_specs=row,
        out_shape=jax.ShapeDtypeStruct((S, D), F32),
        scratch_shapes=[pltpu.VMEM((2, 2, tc, D), F32), pltpu.SemaphoreType.DMA((2, 2))],
        compiler_params=_params("arbitrary"),
        name="combine_resid",
    )(pos3, pos3, x, weights, ys, gain_post.reshape(1, D), mods)


def _route(logits, tm):
    S = logits.shape[0]
    top_val, top_idx = lax.top_k(logits, 2)
    weights = jax.nn.softmax(top_val, axis=-1)
    e_flat = top_idx.reshape(-1)
    onehot = (e_flat[:, None] == jnp.arange(N_EXPERTS)[None, :]).astype(jnp.int32)
    rank = jnp.take_along_axis(jnp.cumsum(onehot, axis=0), e_flat[:, None], axis=1)[:, 0] - 1
    counts = jnp.sum(onehot, axis=0)
    tiles_per = (counts + tm - 1) // tm
    tile_end = jnp.cumsum(tiles_per)
    start = (tile_end - tiles_per) * tm
    dest = start[e_flat] + rank
    n_rows = 2 * S + N_EXPERTS * tm
    n_tiles = n_rows // tm
    token = jnp.arange(2 * S, dtype=jnp.int32) // 2
    src = jnp.zeros((n_rows,), jnp.int32).at[dest].set(token)
    n_used = tile_end[-1].astype(jnp.int32)
    tile_id = jnp.minimum(jnp.arange(n_tiles, dtype=jnp.int32), n_used - 1)
    tile_expert = jnp.sum((tile_end[None, :] <= tile_id[:, None]).astype(jnp.int32), axis=1)
    tile_expert = jnp.minimum(tile_expert, N_EXPERTS - 1)
    return src, weights, dest.astype(jnp.int32), tile_expert, n_used.reshape(1)


def _adaln(c, c_ctx, w, b):
    D = c.shape[-1]
    rows = jnp.zeros((8, D), F32).at[0].set(c[0]).at[1].set(c_ctx)
    m = matmul(jax.nn.silu(rows), w, 8, 2048)[:2] + b[None, :]
    return m.reshape(12, 1, D)


def _head_rms(o, gain, n_heads):
    T = o.shape[0]
    oh = o.reshape(T, n_heads, -1)
    y = oh * lax.rsqrt(jnp.mean(oh * oh, axis=-1, keepdims=True) + NORM_EPS) * gain
    return y.reshape(T, -1)


def _even_mixer(h, L, w_in, w_out, attn_sink, hgrn_norm, hgrn_lb):
    T = h.shape[0]
    S = T - L
    proj = matmul(h, jnp.concatenate([w_in[:, 1536:], w_in[:, :1536]], axis=1), 1280, 512)
    att = sink_attention(proj, 5, 24, 25, L, attn_sink, _rope_tables(S))
    att_c = sink_attention(proj, 5, 24, 25, L, attn_sink, None)
    lb = (hgrn_lb.reshape(1, -1),)
    qh, ih = (proj, 1024, 0), (proj, 1024, 1)
    o_rev = chunk_mixer(_hgrn_heads, [qh, ih, (proj, 1024, 3), lb], 8, 128, 128, L, True)
    hg = chunk_mixer(_hgrn_heads, [qh, ih, (proj, 1024, 2), lb], 8, 128, 128, L, False,
                     final=((proj, 1024, 4), o_rev, hgrn_norm))
    ycat = jnp.concatenate([jnp.concatenate([att_c, att], axis=0), hg], axis=-1)
    return matmul(ycat, w_out, 1280, 1024)


def _token_shift(p, L, mu_prev, mu_next):
    def one(s):
        zero = jnp.zeros_like(s[:1])
        prev = jnp.concatenate([zero, s[:-1]], axis=0)
        nxt = jnp.concatenate([s[1:], zero], axis=0)
        return s + mu_prev * (prev - s) + mu_next * (nxt - s)
    return jnp.concatenate([one(p[:L]), one(p[L:])], axis=0)


def _odd_mixer(h, L, w_in, w_out, gla_gate_up_f, gla_gate_up_b, gla_gate_bias_f, gla_gate_bias_b, gla_norm,
               mu_prev, mu_next, w0_f, w0_b, w2_f, w2_b, a0, a2, g2, k_k, k_a, r_k, ln_w, ln_b, to_bf16):
    T = h.shape[0]
    GO = 3104
    cols = lambda a, b: w_in[:, a:b]
    w_big = jnp.concatenate([cols(0, 2048), cols(2080, 3104), cols(GO, GO + 3072), cols(GO + 3360, GO + 3616)], axis=1)
    w_small = jnp.concatenate([cols(2048, 2080), cols(GO + 3072, GO + 3360),
                               jnp.zeros((w_in.shape[0], 64), F32)], axis=1)
    pb = matmul(h, w_big, 1280, 640)
    ps = matmul(h, w_small, 1280, 384)
    def gla_inputs(up, row0, bias):
        up_rows = jnp.zeros((ps.shape[1], up.shape[1]), F32).at[row0:row0 + up.shape[0]].set(up)
        return [(pb, 512, 0), (pb, 512, 1), (pb, 1024, 1), (ps, ps.shape[1], 0), (up_rows,), (bias.reshape(1, -1),)]
    o_rev = chunk_mixer(_gla_heads, gla_inputs(gla_gate_up_b, 16, gla_gate_bias_b), 4, 128, 256, L, True)
    gla = chunk_mixer(_gla_heads, gla_inputs(gla_gate_up_f, 0, gla_gate_bias_f), 4, 128, 256, L, False,
                      final=((pb, 1024, 2), o_rev, gla_norm))
    rw, cast_done = rwkv7_branch(pb, ps, 1, 24, L, mu_prev, mu_next, (32, 128, 224), w0_f, w0_b, w2_f, w2_b,
                                 a0, a2, g2, k_k, k_a, r_k, ln_w, ln_b, to_bf16)
    ycat = jnp.concatenate([gla[L:], rw[L:]], axis=-1)
    return matmul(ycat, w_out, 1024, 1024), cast_done


def kernel(x, c, ctx, c_ctx, hgrn_lb_logits, l0_ada_w, l0_ada_b, l0_norm_mix_pre, l0_norm_mix_post, l0_norm_ffn_pre, l0_norm_ffn_post, l0_w_in, l0_w_out, l0_attn_sink, l0_hgrn_norm, l0_ffn_w_gate, l0_ffn_w_up, l0_ffn_w_down, l1_ada_w, l1_ada_b, l1_norm_mix_pre, l1_norm_mix_post, l1_norm_ffn_pre, l1_norm_ffn_post, l1_w_in, l1_w_out, l1_gla_gate_up_f, l1_gla_gate_up_b, l1_gla_gate_bias_f, l1_gla_gate_bias_b, l1_gla_norm, l1_rwkv_mu_prev, l1_rwkv_mu_next, l1_rwkv_w0_f, l1_rwkv_w0_b, l1_rwkv_w2_f, l1_rwkv_w2_b, l1_rwkv_a0, l1_rwkv_a2, l1_rwkv_g2, l1_rwkv_k_k, l1_rwkv_k_a, l1_rwkv_r_k, l1_rwkv_ln_w, l1_rwkv_ln_b, l1_moe_router, l1_moe_w_gate, l1_moe_w_up, l1_moe_w_down):
    B, S, D = x.shape
    L = ctx.shape[1]
    assert B == 1
    T = L + S
    SH1, SC1, G1, SH2, SC2, G2 = range(6)
    xa = jnp.concatenate([ctx[0], x[0]], axis=0)
    hgrn_lb = jnp.cumsum(jax.nn.softmax(hgrn_lb_logits.astype(F32), axis=0), axis=0)
    m0 = _adaln(c, c_ctx, l0_ada_w, l0_ada_b)
    m1 = _adaln(c, c_ctx, l1_ada_w, l1_ada_b)

    h = normmod(xa, l0_norm_mix_pre, m0, SH1, SC1, L)
    y = _even_mixer(h, L, l0_w_in, l0_w_out, l0_attn_sink, l0_hgrn_norm, hgrn_lb[0])
    xa, h = resid_norm(xa, y, l0_norm_mix_post, m0, G1, L, nxt=(l0_norm_ffn_pre, m0, SH2, SC2))
    n_t = T // DENSE_TILE
    f = swiglu_ffn(h, l0_ffn_w_gate[None].astype(BF16), l0_ffn_w_up[None].astype(BF16),
                   l0_ffn_w_down[None].astype(BF16),
                   jnp.zeros((n_t,), jnp.int32), jnp.full((1,), n_t, jnp.int32), DENSE_TILE, DENSE_FF_BLOCK)
    xa, h = resid_norm(xa, f, l0_norm_ffn_post, m0, G2, L, nxt=(l1_norm_mix_pre, m1, SH1, SC1))

    E, _, FE = l1_moe_w_gate.shape
    y, (wg16, wu16) = _odd_mixer(
        h, L, l1_w_in, l1_w_out, l1_gla_gate_up_f, l1_gla_gate_up_b, l1_gla_gate_bias_f,
        l1_gla_gate_bias_b, l1_gla_norm, l1_rwkv_mu_prev, l1_rwkv_mu_next, l1_rwkv_w0_f, l1_rwkv_w0_b,
        l1_rwkv_w2_f, l1_rwkv_w2_b, l1_rwkv_a0, l1_rwkv_a2, l1_rwkv_g2, l1_rwkv_k_k, l1_rwkv_k_a,
        l1_rwkv_r_k, l1_rwkv_ln_w, l1_rwkv_ln_b,
        to_bf16=(l1_moe_w_gate.reshape(E * D, FE), l1_moe_w_up.reshape(E * D, FE)))
    xl, h = resid_norm(xa, y, l1_norm_mix_post, m1, G1, 0, nxt=(l1_norm_ffn_pre, m1, SH2, SC2), x_row0=L)
    router = jnp.concatenate([l1_moe_router, jnp.zeros((D, 128 - N_EXPERTS), F32)], axis=1)
    logits = matmul(h, router, 1024, 128)[:, :N_EXPERTS]
    src, gate_w, dest, tile_expert, n_used = _route(logits, MOE_TILE)
    hs = gather_rows(h, src, TOK_BLOCK)
    ys = swiglu_ffn(hs, wg16.reshape(E, D, FE), wu16.reshape(E, D, FE), l1_moe_w_down.astype(BF16),
                    tile_expert, n_used, MOE_TILE, 1024)
    out = combine_resid(xl, ys, dest, gate_w, l1_norm_ffn_post, m1, G2)
    return out[None]
```

```python
import functools

import jax
import jax.numpy as jnp
from jax import lax
from jax.experimental import pallas as pl
from jax.experimental.pallas import tpu as pltpu

F32 = jnp.float32
BF16 = jnp.bfloat16

NORM_EPS = 1e-6
CHUNK = 32
TOK_BLOCK = 256
ATT_BLOCK = 128
WINDOW = 128
HEAD_DIM = 128
ATT_HEADS = 8
ATT_KV_HEADS = 2
ROPE_THETA = 10000.0
GRID_W = 64
RWKV_N = 64
RWKV_HEADS = 16
RWKV_LN_EPS = 64e-5
RWKV_HEAD_BLOCK = 4
GLA_GATE_NORMALIZER = 16.0
N_EXPERTS = 8
MOE_TILE = 512
DENSE_TILE = 416
DENSE_FF_BLOCK = 1408
VMEM_LIMIT_BYTES = 56 * 1024 * 1024
LANES = 128
BF16_SUBLANES = 16
ROW_COPY_UNROLL = 8


def _params(*sem):
    return pltpu.CompilerParams(dimension_semantics=sem, vmem_limit_bytes=VMEM_LIMIT_BYTES)


def _mm_kernel(x_ref, w_ref, o_ref, wbf_ref):
    @pl.when(pl.program_id(1) == 0)
    def _():
        wbf_ref[...] = w_ref[...].astype(BF16)

    o_ref[...] = jnp.dot(x_ref[...].astype(BF16), wbf_ref[...],
                         preferred_element_type=F32).astype(o_ref.dtype)


def matmul(x, w, tm, tn, out_dtype=F32, col_roll=0):
    M, K = x.shape
    N = w.shape[1]
    assert M % tm == 0 and N % tn == 0, (M, tm, N, tn)
    nj = N // tn
    return pl.pallas_call(
        _mm_kernel,
        grid=(nj, M // tm),
        in_specs=[pl.BlockSpec((tm, K), lambda j, i: (i, 0)),
                  pl.BlockSpec((K, tn), lambda j, i: (0, jnp.where(j + col_roll >= nj, j + col_roll - nj,
                                                                    j + col_roll)))],
        out_specs=pl.BlockSpec((tm, tn), lambda j, i: (i, j)),
        out_shape=jax.ShapeDtypeStruct((M, N), out_dtype),
        scratch_shapes=[pltpu.VMEM((K, tn), BF16)],
        compiler_params=_params("arbitrary", "arbitrary"),
        name="matmul",
    )(x, w)


def _rms(x, gain):
    ms = jnp.mean(x * x, axis=-1, keepdims=True)
    return x * lax.rsqrt(ms + NORM_EPS) * gain


def _normmod_kernel(x_ref, gain_ref, sc_ref, sh_ref, h_ref):
    y = _rms(x_ref[...], gain_ref[...])
    h_ref[...] = (y * (1.0 + sc_ref[0]) + sh_ref[0]).astype(h_ref.dtype)


def _mod_spec(D, slot, n_ctx_tiles):
    return pl.BlockSpec((1, 1, D), lambda i: (jnp.where(i < n_ctx_tiles, 6, 0) + slot, 0, 0))


def normmod(x, gain, mods, sh_slot, sc_slot, n_ctx_rows):
    R, D = x.shape
    tr = TOK_BLOCK
    assert R % tr == 0 and n_ctx_rows % tr == 0
    nct = n_ctx_rows // tr
    return pl.pallas_call(
        _normmod_kernel,
        grid=(R // tr,),
        in_specs=[pl.BlockSpec((tr, D), lambda i: (i, 0)),
                  pl.BlockSpec((1, D), lambda i: (0, 0)),
                  _mod_spec(D, sc_slot, nct), _mod_spec(D, sh_slot, nct)],
        out_specs=pl.BlockSpec((tr, D), lambda i: (i, 0)),
        out_shape=jax.ShapeDtypeStruct((R, D), BF16),
        compiler_params=_params("arbitrary"),
        name="normmod",
    )(x, gain.reshape(1, D), mods, mods)


def _resid_kernel(x_ref, y_ref, gpost_ref, g_ref, *rest, with_next):
    xn = x_ref[...] + g_ref[0] * _rms(y_ref[...], gpost_ref[...])
    if with_next:
        gpre_ref, sc_ref, sh_ref, xo_ref, h_ref = rest
        xo_ref[...] = xn
        h_ref[...] = (_rms(xn, gpre_ref[...]) * (1.0 + sc_ref[0]) + sh_ref[0]).astype(h_ref.dtype)
    else:
        (xo_ref,) = rest
        xo_ref[...] = xn


def resid_norm(x, y, gain_post, mods, g_slot, n_ctx_rows, nxt=None, x_row0=0):
    R, D = y.shape
    tr = TOK_BLOCK
    assert R % tr == 0 and n_ctx_rows % tr == 0 and x_row0 % tr == 0 and x.shape[0] == R + x_row0
    nct = n_ctx_rows // tr
    row = pl.BlockSpec((tr, D), lambda i: (i, 0))
    vec = pl.BlockSpec((1, D), lambda i: (0, 0))
    in_specs = [pl.BlockSpec((tr, D), lambda i: (i + x_row0 // tr, 0)), row, vec, _mod_spec(D, g_slot, nct)]
    args = [x, y, gain_post.reshape(1, D), mods]
    out_specs = [row]
    out_shape = [jax.ShapeDtypeStruct((R, D), F32)]
    if nxt is not None:
        gain_pre, mods_n, sh_slot, sc_slot = nxt
        in_specs += [vec, _mod_spec(D, sc_slot, nct), _mod_spec(D, sh_slot, nct)]
        args += [gain_pre.reshape(1, D), mods_n, mods_n]
        out_specs.append(row)
        out_shape.append(jax.ShapeDtypeStruct((R, D), BF16))
    out = pl.pallas_call(
        functools.partial(_resid_kernel, with_next=nxt is not None),
        grid=(R // tr,),
        in_specs=in_specs, out_specs=out_specs, out_shape=out_shape,
        compiler_params=_params("arbitrary"),
        name="resid_norm",
    )(*args)
    return out if nxt is not None else out[0]


def _ffn_kernel(te_ref, nu_ref, x_ref, wg_ref, wu_ref, *rest, experts_per_part):
    *wd_refs, o_ref = rest
    i, j = pl.program_id(0), pl.program_id(1)

    @pl.when(i < nu_ref[0])
    def _():
        x = x_ref[...]
        g = jnp.dot(x, wg_ref[0], preferred_element_type=F32)
        u = jnp.dot(x, wu_ref[0], preferred_element_type=F32)
        hid = (g * jax.nn.sigmoid(g) * u).astype(BF16)

        wd = wd_refs[0][0]
        part_id = te_ref[i] // experts_per_part
        for p in range(1, len(wd_refs)):
            wd = jnp.where(part_id == p, wd_refs[p][0], wd)
        part = jnp.dot(hid, wd, preferred_element_type=F32)

        @pl.when(j == 0)
        def _():
            o_ref[...] = part

        @pl.when(j > 0)
        def _():
            o_ref[...] += part

    @pl.when(jnp.logical_and(i >= nu_ref[0], j == 0))
    def _():
        o_ref[...] = jnp.zeros_like(o_ref)


def swiglu_ffn(x, w_gate, w_up, w_down_parts, tile_expert, n_used, tm, tf):
    R, D = x.shape
    E, _, F = w_gate.shape
    n_parts = len(w_down_parts)
    epp = E // n_parts
    assert R % tm == 0 and F % tf == 0 and w_gate.dtype == BF16 and x.dtype == BF16 and E % n_parts == 0
    nf = F // tf

    def fblk(i, j, te, nu):
        return jnp.where(i < nu[0], j, nf - 1)

    def down_spec(p):
        def index(i, j, te, nu):
            mine = te[i] // epp == p
            return (jnp.where(mine, te[i] - p * epp, 0), jnp.where(mine, fblk(i, j, te, nu), 0), 0)
        return pl.BlockSpec((1, tf, D), index)

    in_specs = [pl.BlockSpec((tm, D), lambda i, j, te, nu: (i, 0)),
                pl.BlockSpec((1, D, tf), lambda i, j, te, nu: (te[i], 0, fblk(i, j, te, nu))),
                pl.BlockSpec((1, D, tf), lambda i, j, te, nu: (te[i], 0, fblk(i, j, te, nu))),
                *[down_spec(p) for p in range(n_parts)]]
    args = [x, w_gate, w_up, *w_down_parts]
    return pl.pallas_call(
        functools.partial(_ffn_kernel, experts_per_part=epp),
        grid_spec=pltpu.PrefetchScalarGridSpec(
            num_scalar_prefetch=2,
            grid=(R // tm, nf),
            in_specs=in_specs,
            out_specs=pl.BlockSpec((tm, D), lambda i, j, te, nu: (i, 0)),
        ),
        out_shape=jax.ShapeDtypeStruct((R, D), F32),
        compiler_params=_params("arbitrary", "arbitrary"),
        name="swiglu_ffn",
    )(tile_expert, n_used, *args)


def _rope(x, c, sg):
    lane = lax.broadcasted_iota(jnp.int32, x.shape, 1)
    first_half = jnp.bitwise_and(lane, HEAD_DIM // 2 - 1) < HEAD_DIM // 4
    partner = jnp.where(first_half, pltpu.roll(x, HEAD_DIM - HEAD_DIM // 4, 1), pltpu.roll(x, HEAD_DIM // 4, 1))
    return x * c + partner * sg


def _attn_kernel(sink_ref, q_ref, *rest, has_window, seq_len):
    if has_window:
        (kp_ref, ko_ref, kn_ref, vp_ref, vo_ref, vn_ref, cp_ref, co_ref, cn_ref, sp_ref, so_ref, sn_ref,
         kc_ref, vc_ref, o_ref) = rest
    else:
        kc_ref, vc_ref, o_ref = rest
    i = pl.program_id(0)
    BQ = q_ref.shape[0]
    L = kc_ref.shape[0]
    G = ATT_KV_HEADS
    R = ATT_HEADS // G
    n_win = 3 * BQ if has_window else 0
    nk = n_win + L
    if has_window:
        row = lax.broadcasted_iota(jnp.int32, (R * BQ, nk), 0)
        col = lax.broadcasted_iota(jnp.int32, (R * BQ, nk), 1)
        qi = jnp.bitwise_and(row, BQ - 1)
        kpos = i * BQ + col - BQ
        in_band = jnp.abs(col - BQ - qi) <= WINDOW
        in_seq = jnp.logical_and(kpos >= 0, kpos < seq_len)
        valid = jnp.logical_or(col >= n_win, jnp.logical_and(in_band, in_seq))
    for g in range(G):
        cs = slice(g * HEAD_DIM, (g + 1) * HEAD_DIM)
        q_heads = [q_ref[:, (g * R + r) * HEAD_DIM:(g * R + r + 1) * HEAD_DIM] for r in range(R)]
        if has_window:
            q_heads = [_rope(q, co_ref[...], so_ref[...]) for q in q_heads]
            k_all = jnp.concatenate([_rope(kp_ref[:, cs], cp_ref[...], sp_ref[...]),
                                     _rope(ko_ref[:, cs], co_ref[...], so_ref[...]),
                                     _rope(kn_ref[:, cs], cn_ref[...], sn_ref[...]), kc_ref[:, cs]], axis=0)
            v_all = jnp.concatenate([vp_ref[:, cs], vo_ref[:, cs], vn_ref[:, cs], vc_ref[:, cs]], axis=0)
        else:
            k_all, v_all = kc_ref[:, cs], vc_ref[:, cs]
        q_g = (jnp.concatenate(q_heads, axis=0) * (HEAD_DIM ** -0.5)).astype(BF16)
        s = lax.dot_general(q_g, k_all.astype(BF16), (((1,), (1,)), ((), ())),
                            preferred_element_type=F32)
        if has_window:
            s = jnp.where(valid, s, -jnp.inf)
        sink = jnp.concatenate([jnp.full((BQ, 1), sink_ref[g * R + r], F32) for r in range(R)], axis=0)
        m = jnp.maximum(sink, jnp.max(s, axis=-1, keepdims=True))
        p = jnp.exp(s - m)
        denom = jnp.exp(sink - m) + jnp.sum(p, axis=-1, keepdims=True)
        o = jnp.dot(p.astype(BF16), v_all.astype(BF16), preferred_element_type=F32) / denom
        for r in range(R):
            h = g * R + r
            o_ref[:, h * HEAD_DIM:(h + 1) * HEAD_DIM] = o[r * BQ:(r + 1) * BQ].astype(o_ref.dtype)


def sink_attention(proj, q_col, k_col, v_col, L, sink, rope):
    QW, KW = ATT_HEADS * HEAD_DIM, ATT_KV_HEADS * HEAD_DIM
    has_window = rope is not None
    S = proj.shape[0] - L if has_window else L
    BQ = ATT_BLOCK if has_window else L
    nb = S // BQ
    assert S % BQ == 0 and L % BQ == 0
    r0 = L // BQ if has_window else 0
    in_specs = [pl.BlockSpec(memory_space=pltpu.SMEM), pl.BlockSpec((BQ, QW), lambda i: (r0 + i, q_col))]
    args = [sink, proj]
    if has_window:
        prev = lambda i: jnp.maximum(i - 1, 0)
        nxt = lambda i: jnp.minimum(i + 1, nb - 1)
        for col in (k_col, v_col):
            in_specs += [pl.BlockSpec((BQ, KW), lambda i, col=col: (r0 + prev(i), col)),
                         pl.BlockSpec((BQ, KW), lambda i, col=col: (r0 + i, col)),
                         pl.BlockSpec((BQ, KW), lambda i, col=col: (r0 + nxt(i), col))]
            args += [proj] * 3
        for tab in rope:
            in_specs += [pl.BlockSpec((BQ, HEAD_DIM), lambda i: (prev(i), 0)),
                         pl.BlockSpec((BQ, HEAD_DIM), lambda i: (i, 0)),
                         pl.BlockSpec((BQ, HEAD_DIM), lambda i: (nxt(i), 0))]
            args += [tab] * 3
    in_specs += [pl.BlockSpec((L, KW), lambda i: (0, k_col)), pl.BlockSpec((L, KW), lambda i: (0, v_col))]
    args += [proj, proj]
    return pl.pallas_call(
        functools.partial(_attn_kernel, has_window=has_window, seq_len=S),
        grid=(nb,),
        in_specs=in_specs,
        out_specs=pl.BlockSpec((BQ, QW), lambda i: (i, 0)),
        out_shape=jax.ShapeDtypeStruct((S, QW), BF16),
        compiler_params=_params("arbitrary"),
        name="sink_attention",
    )(*args)


def _rope_tables(n_tokens):
    n_freq = HEAD_DIM // 4
    t = jnp.arange(n_tokens)
    row = (t // GRID_W).astype(F32)
    col = (t % GRID_W).astype(F32)
    inv = ROPE_THETA ** (-jnp.arange(n_freq, dtype=F32) / n_freq)
    ar, ac = row[:, None] * inv, col[:, None] * inv
    c = jnp.concatenate([jnp.cos(ar), jnp.cos(ar), jnp.cos(ac), jnp.cos(ac)], axis=1)
    sg = jnp.concatenate([-jnp.sin(ar), jnp.sin(ar), -jnp.sin(ac), jnp.sin(ac)], axis=1)
    return c, sg


CHUNK_SHIFT = CHUNK.bit_length() - 1
assert 1 << CHUNK_SHIFT == CHUNK


def _chunk_masks(tb, reverse, strict):
    r = lax.broadcasted_iota(jnp.int32, (tb, tb), 0)
    c = lax.broadcasted_iota(jnp.int32, (tb, tb), 1)
    same = jnp.right_shift(r, CHUNK_SHIFT) == jnp.right_shift(c, CHUNK_SHIFT)
    if reverse:
        tri = (c > r) if strict else (c >= r)
    else:
        tri = (c < r) if strict else (c <= r)
    return same, jnp.logical_and(same, tri)


def _chunk_sums(ld, same, tri):
    tb = ld.shape[0]
    sel = jnp.concatenate([jnp.where(tri, 1.0, 0.0), jnp.where(same, 1.0, 0.0)], axis=0).astype(BF16)
    hi = ld.astype(BF16)
    rest = ld - hi.astype(F32)
    mid = rest.astype(BF16)
    lo = (rest - mid.astype(F32)).astype(BF16)
    dot = lambda p: jnp.dot(sel, p, preferred_element_type=F32)
    g = (dot(lo) + dot(mid)) + dot(hi)
    return g[:tb], g[tb:]


def _dot_tn(a, b):
    return lax.dot_general(a, b, (((0,), (0,)), ((), ())), preferred_element_type=F32)


def _dot_nt(a, b):
    return lax.dot_general(a, b, (((1,), (1,)), ((), ())), preferred_element_type=F32)


def _chunkrec_core(load_head, n_heads, tb, V, reverse, st_ref, dst_ref):
    @pl.when(pl.program_id(0) == 0)
    def _():
        st_ref[...] = jnp.zeros_like(st_ref)

    same, tri = _chunk_masks(tb, reverse, strict=False)
    loaded = [load_head(h) for h in range(n_heads)]
    K = loaded[0][3].shape[1]
    g_cum_all, g_tot_all = _chunk_sums(jnp.concatenate([ld for _, _, _, ld in loaded], axis=1), same, tri)
    per_head = []
    for h in range(n_heads):
        q, k, v, _ = loaded[h]
        g_cum, g_tot = g_cum_all[:, h * K:(h + 1) * K], g_tot_all[:, h * K:(h + 1) * K]
        q_dec = (q * jnp.exp(g_cum)).astype(BF16)
        k_inv = (k * jnp.exp(-g_cum)).astype(BF16)
        k_tail = k * jnp.exp(g_tot - g_cum)
        a = jnp.where(tri, _dot_nt(q_dec, k_inv), 0.0)
        o_intra = jnp.dot(a.astype(BF16), v.astype(BF16), preferred_element_type=F32)
        per_head.append((q_dec, k_tail, v, g_tot, o_intra))
    n_chunks = tb // CHUNK
    order = range(n_chunks - 1, -1, -1) if reverse else range(n_chunks)
    sts = [st_ref[h] for h in range(n_heads)]
    for c in order:
        sl = slice(c * CHUNK, (c + 1) * CHUNK)
        for h in range(n_heads):
            q_dec, k_tail, v, g_tot, o_intra = per_head[h]
            st = sts[h]
            dst_ref[sl, h * V:(h + 1) * V] = o_intra[sl] + _dot_nt(q_dec[sl], st.astype(BF16))
            d = jnp.exp(g_tot[c * CHUNK:c * CHUNK + 1, :])
            sts[h] = st * d + _dot_tn(v[sl], k_tail[sl])
    for h in range(n_heads):
        st_ref[h] = sts[h]


def _gated_head_norm(o, gain, gate_raw):
    y = o * lax.rsqrt(jnp.mean(o * o, axis=-1, keepdims=True) + NORM_EPS) * gain
    return y * (gate_raw * jax.nn.sigmoid(gate_raw))


def _chunk_mixer_kernel(*refs, reverse, n_heads, K, V, load_heads):
    if reverse:
        *in_refs, o_ref, st_ref = refs
        dst_ref = o_ref
    else:
        *in_refs, gate_ref, orev_ref, gain_ref, o_ref, st_ref, dst_ref = refs
    tb = o_ref.shape[0]
    _chunkrec_core(load_heads(*in_refs), n_heads, tb, V, reverse, st_ref, dst_ref)
    if not reverse:
        for h in range(n_heads):
            vs = slice(h * V, (h + 1) * V)
            o_ref[:, vs] = _gated_head_norm(dst_ref[:, vs] + orev_ref[:, vs], gain_ref[...],
                                            gate_ref[:, vs]).astype(o_ref.dtype)


def _hgrn_heads(qh_ref, ih_ref, fr_ref, lb_ref):
    def load(h):
        cs = slice(h * 128, (h + 1) * 128)
        lb = lb_ref[:, cs]
        f = lb + (1.0 - lb) * jax.nn.sigmoid(fr_ref[:, cs])
        qh = qh_ref[:, cs]
        return qh * jax.nn.sigmoid(qh), 1.0 - f, ih_ref[:, cs], jnp.log(f)
    return load


def _gla_heads(gq_ref, gk_ref, gv_ref, gd_ref, up_ref, bias_ref):
    x = jnp.dot(gd_ref[...].astype(BF16), up_ref[...].astype(BF16), preferred_element_type=F32) + bias_ref[...]
    lg = (jnp.minimum(x, 0.0) - jnp.log(1.0 + jnp.exp(-jnp.abs(x)))) * (1.0 / GLA_GATE_NORMALIZER)

    def load(h):
        ks, vs = slice(h * 128, (h + 1) * 128), slice(h * 256, (h + 1) * 256)
        return gq_ref[:, ks] * (128 ** -0.5), gk_ref[:, ks], gv_ref[:, vs], lg[:, ks]
    return load


def _seq_block_index(nblk, nctx, reverse):
    if not reverse:
        return lambda t: t
    return lambda t: jnp.where(t < nctx, nctx - 1 - t, nblk - 1 - (t - nctx))


def chunk_mixer(load_heads, inputs, n_heads, K, V, n_ctx_rows, reverse, final=None):
    T = inputs[0][0].shape[0]
    tb = TOK_BLOCK
    assert T % tb == 0 and n_ctx_rows % tb == 0
    nblk, nctx = T // tb, n_ctx_rows // tb
    blk = _seq_block_index(nblk, nctx, reverse)

    def spec(item):
        if len(item) == 1:
            return pl.BlockSpec(item[0].shape, lambda t: (0, 0))
        _, width, cb = item
        return pl.BlockSpec((tb, width), lambda t: (blk(t), cb))

    HV = n_heads * V
    ospec = pl.BlockSpec((tb, HV), lambda t: (blk(t), 0))
    scratch = [pltpu.VMEM((n_heads, V, K), F32)]
    if not reverse:
        gate, o_rev, gain = final
        inputs = list(inputs) + [gate, (o_rev, HV, 0), (gain.reshape(1, V),)]
        scratch.append(pltpu.VMEM((tb, HV), F32))
    return pl.pallas_call(
        functools.partial(_chunk_mixer_kernel, reverse=reverse, n_heads=n_heads, K=K, V=V, load_heads=load_heads),
        grid=(nblk,),
        in_specs=[spec(it) for it in inputs],
        out_specs=ospec,
        out_shape=jax.ShapeDtypeStruct((T, HV), F32 if reverse else BF16),
        scratch_shapes=scratch,
        compiler_params=_params("arbitrary"),
        name="chunk_mixer",
    )(*[it[0] for it in inputs])


def _mm_bf(a, b):
    return jnp.dot(a.astype(BF16), b.astype(BF16), preferred_element_type=F32)


def _softplus(x):
    return jnp.maximum(x, 0.0) + jnp.log(1.0 + jnp.exp(-jnp.abs(x)))


def _head_sums(x, bd_ref):
    hi = x.astype(BF16)
    lo = (x - hi.astype(F32)).astype(BF16)
    bd = bd_ref[...]
    W = bd.shape[0]
    groups = []
    for g in range(x.shape[1] // W):
        gs = slice(g * W, (g + 1) * W)
        groups.append(jnp.dot(lo[:, gs], bd, preferred_element_type=F32)
                      + jnp.dot(hi[:, gs], bd, preferred_element_type=F32))
    return jnp.concatenate(groups, axis=1)


def _rwkv_prep_kernel(p_ref, pp_ref, pn_ref, g_ref, gp_ref, gn_ref, s_ref, sp_ref, sn_ref,
                      mup_ref, mun_ref, mugp_ref, mugn_ref, musp_ref, musn_ref,
                      w2f_ref, w2b_ref, a2_ref, g2_ref, w0f_ref, w0b_ref, a0_ref, kk_ref, ka_ref, rk_ref, bd_ref,
                      r_out, k_out, v_out, kk_out, b_out, lwf_out, lwb_out, go_out, bon_out,
                      gcf_out, gtf_out, gcb_out, gtb_out, *, n_ctx_blocks):
    t, nblk = pl.program_id(0), pl.num_programs(0)
    tb = p_ref.shape[0]
    C = RWKV_HEADS * RWKV_N
    keep_prev = jnp.where(jnp.logical_or(t == 0, t == n_ctx_blocks), 0.0, 1.0)
    keep_next = jnp.where(jnp.logical_or(t == n_ctx_blocks - 1, t == nblk - 1), 0.0, 1.0)

    def shifted(x_ref, xp_ref, xn_ref, mu_p_ref, mu_n_ref):
        x = x_ref[...]
        rows = lax.broadcasted_iota(jnp.int32, x.shape, 0)
        prev = jnp.where(rows == 0, xp_ref[7:8, :] * keep_prev, pltpu.roll(x, 1, 0))
        nxt = jnp.where(rows == tb - 1, xn_ref[0:1, :] * keep_next, pltpu.roll(x, tb - 1, 0))
        return x + mu_p_ref[...] * (prev - x) + mu_n_ref[...] * (nxt - x)

    rkv = shifted(p_ref, pp_ref, pn_ref, mup_ref, mun_ref)
    rr, rk, rv = rkv[:, :C], rkv[:, C:2 * C], rkv[:, 2 * C:]
    low = shifted(s_ref, sp_ref, sn_ref, musp_ref, musn_ref)
    gd = shifted(g_ref, gp_ref, gn_ref, mugp_ref, mugn_ref)
    tl = jnp.tanh(low).astype(BF16)
    for w0_ref, w2_ref, lw_out, gc_out, gt_out, reverse in ((w0f_ref, w2f_ref, lwf_out, gcf_out, gtf_out, False),
                                                            (w0b_ref, w2b_ref, lwb_out, gcb_out, gtb_out, True)):
        lw = -jnp.exp(-_softplus(-(w0_ref[...] + jnp.dot(tl, w2_ref[...], preferred_element_type=F32))) - 0.5)
        lw_out[...] = lw
        same, tri = _chunk_masks(tb, reverse, strict=False)
        gc_out[...], gt_out[...] = _chunk_sums(lw, same, tri)
    a_sig = jax.nn.sigmoid(a0_ref[...] + jnp.dot(low.astype(BF16), a2_ref[...], preferred_element_type=F32))
    go_out[...] = jnp.dot(jax.nn.sigmoid(gd).astype(BF16), g2_ref[...], preferred_element_type=F32)
    kk = rk * kk_ref[...]
    kk = kk * lax.rsqrt(_head_sums(kk * kk, bd_ref) + 1e-12)
    k_mod = rk * (1.0 + (a_sig - 1.0) * ka_ref[...])
    r_out[...] = rr
    k_out[...] = k_mod
    v_out[...] = rv
    kk_out[...] = kk
    b_out[...] = kk * a_sig
    bon_out[...] = _head_sums(rr * k_mod * rk_ref[...], bd_ref) * rv


def _rwkv_pre_kernel(r_ref, k_ref, v_ref, kk_ref, b_ref, lw_ref, gc_ref, gt_ref, *rest, reverse):
    n_cast = (len(rest) - 4) // 2
    w32_refs, (qp_ref, ol_ref, plt_ref, zt_ref), w16_refs = rest[:n_cast], rest[n_cast:n_cast + 4], rest[n_cast + 4:]
    for w32_ref, w16_ref in zip(w32_refs, w16_refs):
        w16_ref[...] = w32_ref[...].astype(BF16)
    N = RWKV_N
    tb = r_ref.shape[0]
    hb = r_ref.shape[1] // N
    same, tri_incl = _chunk_masks(tb, reverse, strict=False)
    _, tri_strict = _chunk_masks(tb, reverse, strict=True)
    xs, lps, rest = [], [], []
    for h in range(hb):
        hs = slice(h * N, (h + 1) * N)
        r, k, v, b, lw, g_cum, g_tot = (ref[:, hs] for ref in (r_ref, k_ref, v_ref, b_ref, lw_ref, gc_ref, gt_ref))
        a = -kk_ref[:, hs]
        e_neg = jnp.exp(-g_cum)
        e_tail = jnp.exp(g_tot - g_cum)
        a_t = (a * jnp.exp(g_cum - lw)).astype(BF16)
        r_t = (r * jnp.exp(g_cum)).astype(BF16)
        b_t = (b * e_neg).astype(BF16)
        k_t = (k * e_neg).astype(BF16)
        a_ab = jnp.where(tri_strict, _dot_nt(a_t, b_t), 0.0)
        a_ak = jnp.where(tri_strict, _dot_nt(a_t, k_t), 0.0)
        a_rb = jnp.where(tri_incl, _dot_nt(r_t, b_t), 0.0)
        a_rk = jnp.where(tri_incl, _dot_nt(r_t, k_t), 0.0)
        xs.append(jnp.concatenate([a_t.astype(F32), _mm_bf(a_ak, v)], axis=1))
        lps.append(a_ab)
        rest.append((v, b * e_tail, k * e_tail, a_rb,
                     jnp.concatenate([r_t.astype(F32), _mm_bf(a_rk, v)], axis=1)))
    for j in range(CHUNK_SHIFT):
        xs = [x + _mm_bf(lp, x) for x, lp in zip(xs, lps)]
        if j < CHUNK_SHIFT - 1:
            lps = [_mm_bf(lp, lp) for lp in lps]
    for h in range(hb):
        v, b_h, k_h, a_rb, qo0 = rest[h]
        x = xs[h]
        qo = qo0 + _mm_bf(a_rb, x)
        qp_ref[:, h * N:(h + 1) * N] = qo[:, :N]
        ol_ref[:, h * N:(h + 1) * N] = qo[:, N:]
        w, uloc = x[:, :N], x[:, N:]
        for c in range(tb // CHUNK):
            sl = slice(c * CHUNK, (c + 1) * CHUNK)
            plt_ref[h, c] = _dot_tn(w[sl], b_h[sl])
            zt_ref[h, c] = _dot_tn(uloc[sl], b_h[sl]) + _dot_tn(v[sl], k_h[sl])


def _rwkv_scan_kernel(qp_ref, ol_ref, plt_ref, zt_ref, gt_ref, *rest, reverse):
    if reverse:
        o_ref, st_ref = rest
        dst_ref = o_ref
    else:
        orev_ref, go_ref, bon_ref, lnw_ref, lnb_ref, o_ref, st_ref, dst_ref = rest

    @pl.when(pl.program_id(0) == 0)
    def _():
        st_ref[...] = jnp.zeros_like(st_ref)

    N = RWKV_N
    tb = qp_ref.shape[0]
    n_heads = qp_ref.shape[1] // N
    n_chunks = tb // CHUNK
    order = range(n_chunks - 1, -1, -1) if reverse else range(n_chunks)
    sts = [st_ref[h] for h in range(n_heads)]
    for c in order:
        sl = slice(c * CHUNK, (c + 1) * CHUNK)
        for h in range(n_heads):
            hs = slice(h * N, (h + 1) * N)
            st = sts[h]
            dst_ref[sl, hs] = _dot_nt(qp_ref[sl, hs].astype(BF16), st.astype(BF16)) + ol_ref[sl, hs]
            d = jnp.exp(gt_ref[c * CHUNK:c * CHUNK + 1, hs])
            sts[h] = st * d + _mm_bf(st, plt_ref[h, c]) + zt_ref[h, c]
    for h in range(n_heads):
        st_ref[h] = sts[h]
    if not reverse:
        for h in range(n_heads):
            hs = slice(h * N, (h + 1) * N)
            o = dst_ref[:, hs] + orev_ref[:, hs]
            mu = jnp.mean(o, axis=-1, keepdims=True)
            var = jnp.mean(jnp.square(o - mu), axis=-1, keepdims=True)
            on = (o - mu) * lax.rsqrt(var + RWKV_LN_EPS) * lnw_ref[:, hs] + lnb_ref[:, hs]
            o_ref[:, hs] = ((on + bon_ref[:, hs]) * go_ref[:, hs]).astype(o_ref.dtype)


def rwkv7_branch(pb, ps, rkv_col, gd_col, n_ctx_rows, mu_prev, mu_next, low_cols, w0_f, w0_b, w2_f, w2_b,
                 a0, a2, g2, k_k, k_a, r_k, ln_w, ln_b, to_bf16):
    T = pb.shape[0]
    H, N = RWKV_HEADS, RWKV_N
    C = H * N
    tb = TOK_BLOCK
    assert T % tb == 0 and n_ctx_rows % tb == 0
    nblk, nctx = T // tb, n_ctx_rows // tb
    ncb = tb // CHUNK
    GW, SW, R = g2.shape[0], ps.shape[1], w2_f.shape[0]
    wf_col, wb_col, ad_col = low_cols
    s0 = 3 * C
    row = lambda v: v.reshape(1, -1)

    def padded_rows(w, r0):
        return jnp.zeros((SW, C), F32).at[r0:r0 + w.shape[0]].set(w).astype(BF16)

    def mu_low(mu):
        return jnp.zeros((1, SW), F32).at[0, wf_col:wf_col + 3 * R].set(mu[s0:s0 + 3 * R])

    bdw = 2 * LANES
    bd = (jnp.arange(bdw)[:, None] // N == jnp.arange(bdw)[None, :] // N).astype(BF16)
    r8 = tb // 8

    def halo(width, col):
        return (pl.BlockSpec((tb, width), lambda t: (t, col)),
                pl.BlockSpec((8, width), lambda t: (jnp.maximum(t * r8 - 1, 0), col)),
                pl.BlockSpec((8, width), lambda t: (jnp.minimum((t + 1) * r8, T // 8 - 1), col)))

    whole = lambda a: pl.BlockSpec(a.shape, lambda t: (0, 0))
    consts = [row(mu_prev[:s0]), row(mu_next[:s0]), row(mu_prev[s0 + 3 * R:]), row(mu_next[s0 + 3 * R:]),
              mu_low(mu_prev), mu_low(mu_next),
              padded_rows(w2_f, wf_col), padded_rows(w2_b, wb_col), padded_rows(a2, ad_col), g2.astype(BF16),
              row(w0_f), row(w0_b), row(a0), row(k_k), row(k_a), row(r_k), bd]
    tokC = pl.BlockSpec((tb, C), lambda t: (t, 0))
    shapeC = jax.ShapeDtypeStruct((T, C), F32)
    r, k, v, kk, b, lw_f, lw_b, g_out, bonus, gc_f, gt_f, gc_b, gt_b = pl.pallas_call(
        functools.partial(_rwkv_prep_kernel, n_ctx_blocks=nctx),
        grid=(nblk,),
        in_specs=[*halo(3 * C, rkv_col), *halo(GW, gd_col), *halo(SW, 0), *[whole(a) for a in consts]],
        out_specs=[tokC] * 13,
        out_shape=[shapeC] * 13,
        compiler_params=_params("arbitrary"),
        name="rwkv_prep",
    )(pb, pb, pb, pb, pb, pb, ps, ps, ps, *consts)

    hb = RWKV_HEAD_BLOCK
    tok = pl.BlockSpec((tb, hb * N), lambda h, t: (t, h))
    mat = pl.BlockSpec((hb, ncb, N, N), lambda h, t: (h, t, 0, 0))
    mat_shape = jax.ShapeDtypeStruct((H, T // CHUNK, N, N), F32)
    o_rev = None
    cast_done = []
    def cast_specs(job):
        w32, row0, wr = job
        rows = BF16_SUBLANES * pl.cdiv(wr, BF16_SUBLANES * (H // hb) * nblk)
        assert wr % rows == 0 and row0 % rows == 0
        block = lambda h, t: jnp.minimum(h * nblk + t, wr // rows - 1)
        return (pl.BlockSpec((rows, w32.shape[1]), lambda h, t: (row0 // rows + block(h, t), 0)),
                pl.BlockSpec((rows, w32.shape[1]), lambda h, t: (block(h, t), 0)),
                jax.ShapeDtypeStruct((wr, w32.shape[1]), BF16))

    for lw, gc, gt, jobs, reverse in ((lw_b, gc_b, gt_b, to_bf16[0], True), (lw_f, gc_f, gt_f, to_bf16[1], False)):
        cin, cout, cshape = zip(*[cast_specs(job) for job in jobs])
        qp, ol, plt, zt, *w16s = pl.pallas_call(
            functools.partial(_rwkv_pre_kernel, reverse=reverse),
            grid=(H // hb, nblk),
            in_specs=[tok] * 8 + list(cin),
            out_specs=[tok, tok, mat, mat] + list(cout),
            out_shape=[shapeC, shapeC, mat_shape, mat_shape] + list(cshape),
            compiler_params=_params("arbitrary", "arbitrary"),
            name="rwkv_pre",
        )(r, k, v, kk, b, lw, gc, gt, *[job[0] for job in jobs])
        cast_done.append(w16s)
        blk = _seq_block_index(nblk, nctx, reverse)
        tok_all = pl.BlockSpec((tb, C), lambda t: (blk(t), 0))
        mat_all = pl.BlockSpec((H, ncb, N, N), lambda t: (0, blk(t), 0, 0))
        in_specs = [tok_all, tok_all, mat_all, mat_all, tok_all]
        args = [qp, ol, plt, zt, gt]
        scratch = [pltpu.VMEM((H, N, N), F32)]
        if not reverse:
            in_specs += [tok_all, tok_all, tok_all, whole(row(ln_w)), whole(row(ln_b))]
            args += [o_rev, g_out, bonus, row(ln_w), row(ln_b)]
            scratch.append(pltpu.VMEM((tb, C), F32))
        out = pl.pallas_call(
            functools.partial(_rwkv_scan_kernel, reverse=reverse),
            grid=(nblk,),
            in_specs=in_specs,
            out_specs=tok_all,
            out_shape=jax.ShapeDtypeStruct((T, C), F32 if reverse else BF16),
            scratch_shapes=scratch,
            compiler_params=_params("arbitrary"),
            name="rwkv_scan",
        )(*args)
        if reverse:
            o_rev = out
    return out, cast_done


def _row_copy(src_hbm, src_row, dst_ref, dst_row, sem):
    return pltpu.make_async_copy(src_hbm.at[pl.ds(src_row, 1)], dst_ref.at[pl.ds(dst_row, 1)], sem)


def _gather_kernel(src_ref, nxt_ref, x_hbm, o_ref, buf, sems):
    i, n_steps = pl.program_id(0), pl.num_programs(0)
    n = o_ref.shape[0]

    def issue(idx_ref, slot):
        def body(r, carry):
            for u in range(2):
                row = 2 * r + u
                _row_copy(x_hbm, idx_ref[0, 0, row], buf.at[slot], row, sems.at[slot]).start(priority=u)
            return carry
        lax.fori_loop(0, n // 2, body, 0, unroll=ROW_COPY_UNROLL)

    @pl.when(i == 0)
    def _():
        issue(src_ref, 0)

    @pl.when(i + 1 < n_steps)
    def _():
        issue(nxt_ref, (i + 1) % 2)

    slot = i % 2

    pltpu.make_async_copy(x_hbm.at[pl.ds(0, n)], buf.at[slot], sems.at[slot]).wait()
    o_ref[...] = buf[slot].astype(o_ref.dtype)


def gather_rows(x, src, tg):
    S, D = x.shape
    P = src.shape[0]
    sub = D // LANES
    assert P % tg == 0 and x.dtype == BF16 and sub == BF16_SUBLANES
    n_steps = P // tg
    src3 = src.reshape(n_steps, 1, tg)
    out = pl.pallas_call(
        _gather_kernel,
        grid=(n_steps,),
        in_specs=[pl.BlockSpec((1, 1, tg), lambda i: (i, 0, 0), memory_space=pltpu.SMEM),
                  pl.BlockSpec((1, 1, tg), lambda i: (jnp.minimum(i + 1, n_steps - 1), 0, 0),
                               memory_space=pltpu.SMEM),
                  pl.BlockSpec(memory_space=pl.ANY)],
        out_specs=pl.BlockSpec((tg, sub, LANES), lambda i: (i, 0, 0)),
        out_shape=jax.ShapeDtypeStruct((P, sub, LANES), BF16),
        scratch_shapes=[pltpu.VMEM((2, tg, sub, LANES), BF16), pltpu.SemaphoreType.DMA((2,))],
        compiler_params=_params("arbitrary"),
        name="gather_rows",
    )(src3, src3, x.reshape(S, sub, LANES))
    return out.reshape(P, D)


def _combine_kernel(pos_ref, nxt_ref, x_ref, w_ref, ys_hbm, gpost_ref, g_ref, o_ref, buf, sems):
    i, n_steps = pl.program_id(0), pl.num_programs(0)
    n = x_ref.shape[0]

    def issue(idx_ref, slot):
        def body(r, carry):
            for k in range(2):
                _row_copy(ys_hbm, idx_ref[0, 0, 2 * r + k], buf.at[slot, k], r, sems.at[slot, k]).start(priority=k)
            return carry
        lax.fori_loop(0, n, body, 0, unroll=ROW_COPY_UNROLL)

    @pl.when(i == 0)
    def _():
        issue(pos_ref, 0)

    @pl.when(i + 1 < n_steps)
    def _():
        issue(nxt_ref, (i + 1) % 2)

    slot = i % 2

    for k in range(2):
        pltpu.make_async_copy(ys_hbm.at[pl.ds(0, n)], buf.at[slot, k], sems.at[slot, k]).wait()
    w = w_ref[...]
    f = w[:, 0:1] * buf[slot, 0] + w[:, 1:2] * buf[slot, 1]
    o_ref[...] = x_ref[...] + g_ref[0] * _rms(f, gpost_ref[...])


def combine_resid(x, ys, pos, weights, gain_post, mods, g_slot):
    S, D = x.shape
    tc = TOK_BLOCK
    assert S % tc == 0
    n_steps = S // tc
    row = pl.BlockSpec((tc, D), lambda i: (i, 0))
    pos3 = pos.reshape(n_steps, 1, 2 * tc)
    return pl.pallas_call(
        _combine_kernel,
        grid=(n_steps,),
        in_specs=[pl.BlockSpec((1, 1, 2 * tc), lambda i: (i, 0, 0), memory_space=pltpu.SMEM),
                  pl.BlockSpec((1, 1, 2 * tc), lambda i: (jnp.minimum(i + 1, n_steps - 1), 0, 0),
                               memory_space=pltpu.SMEM),
                  row, pl.BlockSpec((tc, 2), lambda i: (i, 0)), pl.BlockSpec(memory_space=pl.ANY),
                  pl.BlockSpec((1, D), lambda i: (0, 0)), _mod_spec(D, g_slot, 0)],
        out_specs=row,
        out_shape=jax.ShapeDtypeStruct((S, D), F32),
        scratch_shapes=[pltpu.VMEM((2, 2, tc, D), F32), pltpu.SemaphoreType.DMA((2, 2))],
        compiler_params=_params("arbitrary"),
        name="combine_resid",
    )(pos3, pos3, x, weights, ys, gain_post.reshape(1, D), mods)


def _route(logits, tm):
    S = logits.shape[0]
    top_val, top_idx = lax.top_k(logits, 2)
    weights = jax.nn.softmax(top_val, axis=-1)
    e_flat = top_idx.reshape(-1)
    onehot = (e_flat[:, None] == jnp.arange(N_EXPERTS)[None, :]).astype(jnp.int32)
    rank = jnp.take_along_axis(jnp.cumsum(onehot, axis=0), e_flat[:, None], axis=1)[:, 0] - 1
    counts = jnp.sum(onehot, axis=0)
    tiles_per = (counts + tm - 1) // tm
    tile_end = jnp.cumsum(tiles_per)
    start = (tile_end - tiles_per) * tm
    dest = start[e_flat] + rank
    n_rows = 2 * S + N_EXPERTS * tm
    n_tiles = n_rows // tm
    token = jnp.arange(2 * S, dtype=jnp.int32) // 2
    src = jnp.zeros((n_rows,), jnp.int32).at[dest].set(token)
    n_used = tile_end[-1].astype(jnp.int32)
    tile_id = jnp.minimum(jnp.arange(n_tiles, dtype=jnp.int32), n_used - 1)
    tile_expert = jnp.sum((tile_end[None, :] <= tile_id[:, None]).astype(jnp.int32), axis=1)
    tile_expert = jnp.minimum(tile_expert, N_EXPERTS - 1)
    return src, weights, dest.astype(jnp.int32), tile_expert, n_used.reshape(1)


def _adaln(c, c_ctx, w, b):
    D = c.shape[-1]
    rows = jnp.zeros((8, D), F32).at[0].set(c[0]).at[1].set(c_ctx)
    m = matmul(jax.nn.silu(rows), w, 8, 2048)[:2] + b[None, :]
    return m.reshape(12, 1, D)


def _head_rms(o, gain, n_heads):
    T = o.shape[0]
    oh = o.reshape(T, n_heads, -1)
    y = oh * lax.rsqrt(jnp.mean(oh * oh, axis=-1, keepdims=True) + NORM_EPS) * gain
    return y.reshape(T, -1)


def _even_mixer(h, L, w_in, w_out, attn_sink, hgrn_norm, hgrn_lb):
    T = h.shape[0]
    S = T - L
    proj = matmul(h, w_in, 1280, 512, col_roll=1536 // 512)
    att = sink_attention(proj, 5, 24, 25, L, attn_sink, _rope_tables(S))
    att_c = sink_attention(proj, 5, 24, 25, L, attn_sink, None)
    lb = (hgrn_lb.reshape(1, -1),)
    qh, ih = (proj, 1024, 0), (proj, 1024, 1)
    o_rev = chunk_mixer(_hgrn_heads, [qh, ih, (proj, 1024, 3), lb], 8, 128, 128, L, True)
    hg = chunk_mixer(_hgrn_heads, [qh, ih, (proj, 1024, 2), lb], 8, 128, 128, L, False,
                     final=((proj, 1024, 4), o_rev, hgrn_norm))
    ycat = jnp.concatenate([jnp.concatenate([att_c, att], axis=0), hg], axis=-1)
    return matmul(ycat, w_out, 1280, 1024)


def _token_shift(p, L, mu_prev, mu_next):
    def one(s):
        zero = jnp.zeros_like(s[:1])
        prev = jnp.concatenate([zero, s[:-1]], axis=0)
        nxt = jnp.concatenate([s[1:], zero], axis=0)
        return s + mu_prev * (prev - s) + mu_next * (nxt - s)
    return jnp.concatenate([one(p[:L]), one(p[L:])], axis=0)


def _odd_mixer(h, L, w_in, w_out, gla_gate_up_f, gla_gate_up_b, gla_gate_bias_f, gla_gate_bias_b, gla_norm,
               mu_prev, mu_next, w0_f, w0_b, w2_f, w2_b, a0, a2, g2, k_k, k_a, r_k, ln_w, ln_b, to_bf16):
    T = h.shape[0]
    GO = 3104
    cols = lambda a, b: w_in[:, a:b]
    w_big = jnp.concatenate([cols(0, 2048), cols(2080, 3104), cols(GO, GO + 3072), cols(GO + 3360, GO + 3616)], axis=1)
    w_small = jnp.concatenate([cols(2048, 2080), cols(GO + 3072, GO + 3360),
                               jnp.zeros((w_in.shape[0], 64), F32)], axis=1)
    pb = matmul(h, w_big, 1280, 640)
    ps = matmul(h, w_small, 1280, 384)
    def gla_inputs(up, row0, bias):
        up_rows = jnp.zeros((ps.shape[1], up.shape[1]), F32).at[row0:row0 + up.shape[0]].set(up)
        return [(pb, 512, 0), (pb, 512, 1), (pb, 1024, 1), (ps, ps.shape[1], 0), (up_rows,), (bias.reshape(1, -1),)]
    o_rev = chunk_mixer(_gla_heads, gla_inputs(gla_gate_up_b, 16, gla_gate_bias_b), 4, 128, 256, L, True)
    gla = chunk_mixer(_gla_heads, gla_inputs(gla_gate_up_f, 0, gla_gate_bias_f), 4, 128, 256, L, False,
                      final=((pb, 1024, 2), o_rev, gla_norm))
    rw, cast_done = rwkv7_branch(pb, ps, 1, 24, L, mu_prev, mu_next, (32, 128, 224), w0_f, w0_b, w2_f, w2_b,
                                 a0, a2, g2, k_k, k_a, r_k, ln_w, ln_b, to_bf16)
    ycat = jnp.concatenate([gla[L:], rw[L:]], axis=-1)
    return matmul(ycat, w_out, 1024, 1024), cast_done


def kernel(x, c, ctx, c_ctx, hgrn_lb_logits, l0_ada_w, l0_ada_b, l0_norm_mix_pre, l0_norm_mix_post, l0_norm_ffn_pre, l0_norm_ffn_post, l0_w_in, l0_w_out, l0_attn_sink, l0_hgrn_norm, l0_ffn_w_gate, l0_ffn_w_up, l0_ffn_w_down, l1_ada_w, l1_ada_b, l1_norm_mix_pre, l1_norm_mix_post, l1_norm_ffn_pre, l1_norm_ffn_post, l1_w_in, l1_w_out, l1_gla_gate_up_f, l1_gla_gate_up_b, l1_gla_gate_bias_f, l1_gla_gate_bias_b, l1_gla_norm, l1_rwkv_mu_prev, l1_rwkv_mu_next, l1_rwkv_w0_f, l1_rwkv_w0_b, l1_rwkv_w2_f, l1_rwkv_w2_b, l1_rwkv_a0, l1_rwkv_a2, l1_rwkv_g2, l1_rwkv_k_k, l1_rwkv_k_a, l1_rwkv_r_k, l1_rwkv_ln_w, l1_rwkv_ln_b, l1_moe_router, l1_moe_w_gate, l1_moe_w_up, l1_moe_w_down):
    B, S, D = x.shape
    L = ctx.shape[1]
    assert B == 1
    T = L + S
    SH1, SC1, G1, SH2, SC2, G2 = range(6)
    xa = jnp.concatenate([ctx[0], x[0]], axis=0)
    hgrn_lb = jnp.cumsum(jax.nn.softmax(hgrn_lb_logits.astype(F32), axis=0), axis=0)
    m0 = _adaln(c, c_ctx, l0_ada_w, l0_ada_b)
    m1 = _adaln(c, c_ctx, l1_ada_w, l1_ada_b)

    h = normmod(xa, l0_norm_mix_pre, m0, SH1, SC1, L)
    y = _even_mixer(h, L, l0_w_in, l0_w_out, l0_attn_sink, l0_hgrn_norm, hgrn_lb[0])
    xa, h = resid_norm(xa, y, l0_norm_mix_post, m0, G1, L, nxt=(l0_norm_ffn_pre, m0, SH2, SC2))
    n_t = T // DENSE_TILE
    f = swiglu_ffn(h, l0_ffn_w_gate[None].astype(BF16), l0_ffn_w_up[None].astype(BF16),
                   [l0_ffn_w_down[None].astype(BF16)],
                   jnp.zeros((n_t,), jnp.int32), jnp.full((1,), n_t, jnp.int32), DENSE_TILE, DENSE_FF_BLOCK)
    xa, h = resid_norm(xa, f, l0_norm_ffn_post, m0, G2, L, nxt=(l1_norm_mix_pre, m1, SH1, SC1))

    E, _, FE = l1_moe_w_gate.shape
    wg2, wu2, wd2 = (l1_moe_w_gate.reshape(E * D, FE), l1_moe_w_up.reshape(E * D, FE),
                     l1_moe_w_down.reshape(E * FE, D))
    half = E * FE // 2
    y, ((wg16, wd16_lo), (wu16, wd16_hi)) = _odd_mixer(
        h, L, l1_w_in, l1_w_out, l1_gla_gate_up_f, l1_gla_gate_up_b, l1_gla_gate_bias_f,
        l1_gla_gate_bias_b, l1_gla_norm, l1_rwkv_mu_prev, l1_rwkv_mu_next, l1_rwkv_w0_f, l1_rwkv_w0_b,
        l1_rwkv_w2_f, l1_rwkv_w2_b, l1_rwkv_a0, l1_rwkv_a2, l1_rwkv_g2, l1_rwkv_k_k, l1_rwkv_k_a,
        l1_rwkv_r_k, l1_rwkv_ln_w, l1_rwkv_ln_b,
        to_bf16=([(wg2, 0, E * D), (wd2, 0, half)], [(wu2, 0, E * D), (wd2, half, half)]))
    xl, h = resid_norm(xa, y, l1_norm_mix_post, m1, G1, 0, nxt=(l1_norm_ffn_pre, m1, SH2, SC2), x_row0=L)
    router = jnp.concatenate([l1_moe_router, jnp.zeros((D, 128 - N_EXPERTS), F32)], axis=1)
    logits = matmul(h, router, 1024, 128)[:, :N_EXPERTS]
    src, gate_w, dest, tile_expert, n_used = _route(logits, MOE_TILE)
    hs = gather_rows(h, src, TOK_BLOCK)
    ys = swiglu_ffn(hs, wg16.reshape(E, D, FE), wu16.reshape(E, D, FE),
                    [wd16_lo.reshape(E // 2, FE, D), wd16_hi.reshape(E // 2, FE, D)],
                    tile_expert, n_used, MOE_TILE, 1024)
    out = combine_resid(xl, ys, dest, gate_w, l1_norm_ffn_post, m1, G2)
    return out[None]
```

```python
import functools

import jax
import jax.numpy as jnp
from jax import lax
from jax.experimental import pallas as pl
from jax.experimental.pallas import tpu as pltpu

F32 = jnp.float32
BF16 = jnp.bfloat16

NORM_EPS = 1e-6
CHUNK = 32
TOK_BLOCK = 256
ATT_BLOCK = 128
WINDOW = 128
HEAD_DIM = 128
ATT_HEADS = 8
ATT_KV_HEADS = 2
ROPE_THETA = 10000.0
GRID_W = 64
RWKV_N = 64
RWKV_HEADS = 16
RWKV_LN_EPS = 64e-5
RWKV_HEAD_BLOCK = 8
GLA_GATE_NORMALIZER = 16.0
N_EXPERTS = 8
MOE_TILE = 512
DENSE_TILE = 416
DENSE_FF_BLOCK = 1408
VMEM_LIMIT_BYTES = 56 * 1024 * 1024
LANES = 128
BF16_SUBLANES = 16
ROW_COPY_UNROLL = 8


def _params(*sem):
    return pltpu.CompilerParams(dimension_semantics=sem, vmem_limit_bytes=VMEM_LIMIT_BYTES)


def _mm_kernel(x_ref, w_ref, o_ref, wbf_ref):
    @pl.when(pl.program_id(1) == 0)
    def _():
        wbf_ref[...] = w_ref[...].astype(BF16)

    o_ref[...] = jnp.dot(x_ref[...].astype(BF16), wbf_ref[...],
                         preferred_element_type=F32).astype(o_ref.dtype)


def matmul(x, w, tm, tn, out_dtype=F32, col_roll=0):
    M, K = x.shape
    N = w.shape[1]
    assert M % tm == 0 and N % tn == 0, (M, tm, N, tn)
    nj = N // tn
    return pl.pallas_call(
        _mm_kernel,
        grid=(nj, M // tm),
        in_specs=[pl.BlockSpec((tm, K), lambda j, i: (i, 0)),
                  pl.BlockSpec((K, tn), lambda j, i: (0, jnp.where(j + col_roll >= nj, j + col_roll - nj,
                                                                    j + col_roll)))],
        out_specs=pl.BlockSpec((tm, tn), lambda j, i: (i, j)),
        out_shape=jax.ShapeDtypeStruct((M, N), out_dtype),
        scratch_shapes=[pltpu.VMEM((K, tn), BF16)],
        compiler_params=_params("arbitrary", "arbitrary"),
        name="matmul",
    )(x, w)


def _rms(x, gain):
    ms = jnp.mean(x * x, axis=-1, keepdims=True)
    return x * lax.rsqrt(ms + NORM_EPS) * gain


def _normmod_kernel(x_ref, gain_ref, sc_ref, sh_ref, h_ref):
    y = _rms(x_ref[...], gain_ref[...])
    h_ref[...] = (y * (1.0 + sc_ref[0]) + sh_ref[0]).astype(h_ref.dtype)


def _mod_spec(D, slot, n_ctx_tiles):
    return pl.BlockSpec((1, 1, D), lambda i: (jnp.where(i < n_ctx_tiles, 6, 0) + slot, 0, 0))


def normmod(x, gain, mods, sh_slot, sc_slot, n_ctx_rows):
    R, D = x.shape
    tr = TOK_BLOCK
    assert R % tr == 0 and n_ctx_rows % tr == 0
    nct = n_ctx_rows // tr
    return pl.pallas_call(
        _normmod_kernel,
        grid=(R // tr,),
        in_specs=[pl.BlockSpec((tr, D), lambda i: (i, 0)),
                  pl.BlockSpec((1, D), lambda i: (0, 0)),
                  _mod_spec(D, sc_slot, nct), _mod_spec(D, sh_slot, nct)],
        out_specs=pl.BlockSpec((tr, D), lambda i: (i, 0)),
        out_shape=jax.ShapeDtypeStruct((R, D), BF16),
        compiler_params=_params("arbitrary"),
        name="normmod",
    )(x, gain.reshape(1, D), mods, mods)


def _resid_kernel(x_ref, y_ref, gpost_ref, g_ref, *rest, with_next):
    xn = x_ref[...] + g_ref[0] * _rms(y_ref[...], gpost_ref[...])
    if with_next:
        gpre_ref, sc_ref, sh_ref, xo_ref, h_ref = rest
        xo_ref[...] = xn
        h_ref[...] = (_rms(xn, gpre_ref[...]) * (1.0 + sc_ref[0]) + sh_ref[0]).astype(h_ref.dtype)
    else:
        (xo_ref,) = rest
        xo_ref[...] = xn


def resid_norm(x, y, gain_post, mods, g_slot, n_ctx_rows, nxt=None, x_row0=0):
    R, D = y.shape
    tr = TOK_BLOCK
    assert R % tr == 0 and n_ctx_rows % tr == 0 and x_row0 % tr == 0 and x.shape[0] == R + x_row0
    nct = n_ctx_rows // tr
    row = pl.BlockSpec((tr, D), lambda i: (i, 0))
    vec = pl.BlockSpec((1, D), lambda i: (0, 0))
    in_specs = [pl.BlockSpec((tr, D), lambda i: (i + x_row0 // tr, 0)), row, vec, _mod_spec(D, g_slot, nct)]
    args = [x, y, gain_post.reshape(1, D), mods]
    out_specs = [row]
    out_shape = [jax.ShapeDtypeStruct((R, D), F32)]
    if nxt is not None:
        gain_pre, mods_n, sh_slot, sc_slot = nxt
        in_specs += [vec, _mod_spec(D, sc_slot, nct), _mod_spec(D, sh_slot, nct)]
        args += [gain_pre.reshape(1, D), mods_n, mods_n]
        out_specs.append(row)
        out_shape.append(jax.ShapeDtypeStruct((R, D), BF16))
    out = pl.pallas_call(
        functools.partial(_resid_kernel, with_next=nxt is not None),
        grid=(R // tr,),
        in_specs=in_specs, out_specs=out_specs, out_shape=out_shape,
        compiler_params=_params("arbitrary"),
        name="resid_norm",
    )(*args)
    return out if nxt is not None else out[0]


def _ffn_kernel(te_ref, nu_ref, x_ref, wg_ref, wu_ref, *rest, experts_per_part):
    *wd_refs, o_ref = rest
    i, j = pl.program_id(0), pl.program_id(1)

    @pl.when(i < nu_ref[0])
    def _():
        x = x_ref[...]
        g = jnp.dot(x, wg_ref[0], preferred_element_type=F32)
        u = jnp.dot(x, wu_ref[0], preferred_element_type=F32)
        hid = (g * jax.nn.sigmoid(g) * u).astype(BF16)

        wd = wd_refs[0][0]
        part_id = te_ref[i] // experts_per_part
        for p in range(1, len(wd_refs)):
            wd = jnp.where(part_id == p, wd_refs[p][0], wd)
        @pl.when(j == 0)
        def _():
            o_ref[...] = jnp.dot(hid, wd, preferred_element_type=F32)

        @pl.when(j > 0)
        def _():
            o_ref[...] = o_ref[...] + jnp.dot(hid, wd, preferred_element_type=F32)

    @pl.when(jnp.logical_and(i >= nu_ref[0], j == 0))
    def _():
        o_ref[...] = jnp.zeros_like(o_ref)


def swiglu_ffn(x, w_gate, w_up, w_down_parts, tile_expert, n_used, tm, tf):
    R, D = x.shape
    E, _, F = w_gate.shape
    n_parts = len(w_down_parts)
    epp = E // n_parts
    assert R % tm == 0 and F % tf == 0 and w_gate.dtype == BF16 and x.dtype == BF16 and E % n_parts == 0
    nf = F // tf

    def fblk(i, j, te, nu):
        return jnp.where(i < nu[0], j, nf - 1)

    def down_spec(p):
        def index(i, j, te, nu):
            mine = te[i] // epp == p
            return (jnp.where(mine, te[i] - p * epp, 0), jnp.where(mine, fblk(i, j, te, nu), 0), 0)
        return pl.BlockSpec((1, tf, D), index)

    in_specs = [pl.BlockSpec((tm, D), lambda i, j, te, nu: (i, 0)),
                pl.BlockSpec((1, D, tf), lambda i, j, te, nu: (te[i], 0, fblk(i, j, te, nu))),
                pl.BlockSpec((1, D, tf), lambda i, j, te, nu: (te[i], 0, fblk(i, j, te, nu))),
                *[down_spec(p) for p in range(n_parts)]]
    args = [x, w_gate, w_up, *w_down_parts]
    return pl.pallas_call(
        functools.partial(_ffn_kernel, experts_per_part=epp),
        grid_spec=pltpu.PrefetchScalarGridSpec(
            num_scalar_prefetch=2,
            grid=(R // tm, nf),
            in_specs=in_specs,
            out_specs=pl.BlockSpec((tm, D), lambda i, j, te, nu: (i, 0)),
        ),
        out_shape=jax.ShapeDtypeStruct((R, D), F32),
        compiler_params=_params("arbitrary", "arbitrary"),
        name="swiglu_ffn",
    )(tile_expert, n_used, *args)


def _rope(x, c, sg):
    lane = lax.broadcasted_iota(jnp.int32, x.shape, 1)
    first_half = jnp.bitwise_and(lane, HEAD_DIM // 2 - 1) < HEAD_DIM // 4
    partner = jnp.where(first_half, pltpu.roll(x, HEAD_DIM - HEAD_DIM // 4, 1), pltpu.roll(x, HEAD_DIM // 4, 1))
    return x * c + partner * sg


def _attn_kernel(sink_ref, q_ref, *rest, has_window, seq_len):
    if has_window:
        (kp_ref, ko_ref, kn_ref, vp_ref, vo_ref, vn_ref, cp_ref, co_ref, cn_ref, sp_ref, so_ref, sn_ref,
         kc_ref, vc_ref, o_ref) = rest
    else:
        kc_ref, vc_ref, o_ref = rest
    i = pl.program_id(0)
    BQ = q_ref.shape[0]
    L = kc_ref.shape[0]
    G = ATT_KV_HEADS
    R = ATT_HEADS // G
    n_win = 3 * BQ if has_window else 0
    nk = n_win + L
    if has_window:
        row = lax.broadcasted_iota(jnp.int32, (R * BQ, nk), 0)
        col = lax.broadcasted_iota(jnp.int32, (R * BQ, nk), 1)
        qi = jnp.bitwise_and(row, BQ - 1)
        kpos = i * BQ + col - BQ
        in_band = jnp.abs(col - BQ - qi) <= WINDOW
        in_seq = jnp.logical_and(kpos >= 0, kpos < seq_len)
        valid = jnp.logical_or(col >= n_win, jnp.logical_and(in_band, in_seq))
    for g in range(G):
        cs = slice(g * HEAD_DIM, (g + 1) * HEAD_DIM)
        q_heads = [q_ref[:, (g * R + r) * HEAD_DIM:(g * R + r + 1) * HEAD_DIM] for r in range(R)]
        if has_window:
            q_heads = [_rope(q, co_ref[...], so_ref[...]) for q in q_heads]
            k_all = jnp.concatenate([_rope(kp_ref[:, cs], cp_ref[...], sp_ref[...]),
                                     _rope(ko_ref[:, cs], co_ref[...], so_ref[...]),
                                     _rope(kn_ref[:, cs], cn_ref[...], sn_ref[...]), kc_ref[:, cs]], axis=0)
            v_all = jnp.concatenate([vp_ref[:, cs], vo_ref[:, cs], vn_ref[:, cs], vc_ref[:, cs]], axis=0)
        else:
            k_all, v_all = kc_ref[:, cs], vc_ref[:, cs]
        q_g = (jnp.concatenate(q_heads, axis=0) * (HEAD_DIM ** -0.5)).astype(BF16)
        s = lax.dot_general(q_g, k_all.astype(BF16), (((1,), (1,)), ((), ())),
                            preferred_element_type=F32)
        if has_window:
            s = jnp.where(valid, s, -jnp.inf)
        sink = jnp.concatenate([jnp.full((BQ, 1), sink_ref[g * R + r], F32) for r in range(R)], axis=0)
        m = jnp.maximum(sink, jnp.max(s, axis=-1, keepdims=True))
        p = jnp.exp(s - m)
        denom = jnp.exp(sink - m) + jnp.sum(p, axis=-1, keepdims=True)
        o = jnp.dot(p.astype(BF16), v_all.astype(BF16), preferred_element_type=F32) / denom
        for r in range(R):
            h = g * R + r
            o_ref[:, h * HEAD_DIM:(h + 1) * HEAD_DIM] = o[r * BQ:(r + 1) * BQ].astype(o_ref.dtype)


def sink_attention(proj, q_col, k_col, v_col, L, sink, rope):
    QW, KW = ATT_HEADS * HEAD_DIM, ATT_KV_HEADS * HEAD_DIM
    has_window = rope is not None
    S = proj.shape[0] - L if has_window else L
    BQ = ATT_BLOCK if has_window else L
    nb = S // BQ
    assert S % BQ == 0 and L % BQ == 0
    r0 = L // BQ if has_window else 0
    in_specs = [pl.BlockSpec(memory_space=pltpu.SMEM), pl.BlockSpec((BQ, QW), lambda i: (r0 + i, q_col))]
    args = [sink, proj]
    if has_window:
        prev = lambda i: jnp.maximum(i - 1, 0)
        nxt = lambda i: jnp.minimum(i + 1, nb - 1)
        for col in (k_col, v_col):
            in_specs += [pl.BlockSpec((BQ, KW), lambda i, col=col: (r0 + prev(i), col)),
                         pl.BlockSpec((BQ, KW), lambda i, col=col: (r0 + i, col)),
                         pl.BlockSpec((BQ, KW), lambda i, col=col: (r0 + nxt(i), col))]
            args += [proj] * 3
        for tab in rope:
            in_specs += [pl.BlockSpec((BQ, HEAD_DIM), lambda i: (prev(i), 0)),
                         pl.BlockSpec((BQ, HEAD_DIM), lambda i: (i, 0)),
                         pl.BlockSpec((BQ, HEAD_DIM), lambda i: (nxt(i), 0))]
            args += [tab] * 3
    in_specs += [pl.BlockSpec((L, KW), lambda i: (0, k_col)), pl.BlockSpec((L, KW), lambda i: (0, v_col))]
    args += [proj, proj]
    return pl.pallas_call(
        functools.partial(_attn_kernel, has_window=has_window, seq_len=S),
        grid=(nb,),
        in_specs=in_specs,
        out_specs=pl.BlockSpec((BQ, QW), lambda i: (i, 0)),
        out_shape=jax.ShapeDtypeStruct((S, QW), BF16),
        compiler_params=_params("arbitrary"),
        name="sink_attention",
    )(*args)


def _rope_tables(n_tokens):
    n_freq = HEAD_DIM // 4
    t = jnp.arange(n_tokens)
    row = (t // GRID_W).astype(F32)
    col = (t % GRID_W).astype(F32)
    inv = ROPE_THETA ** (-jnp.arange(n_freq, dtype=F32) / n_freq)
    ar, ac = row[:, None] * inv, col[:, None] * inv
    c = jnp.concatenate([jnp.cos(ar), jnp.cos(ar), jnp.cos(ac), jnp.cos(ac)], axis=1)
    sg = jnp.concatenate([-jnp.sin(ar), jnp.sin(ar), -jnp.sin(ac), jnp.sin(ac)], axis=1)
    return c, sg


CHUNK_SHIFT = CHUNK.bit_length() - 1
assert 1 << CHUNK_SHIFT == CHUNK


def _chunk_masks(tb, reverse, strict):
    r = lax.broadcasted_iota(jnp.int32, (tb, tb), 0)
    c = lax.broadcasted_iota(jnp.int32, (tb, tb), 1)
    same = jnp.right_shift(r, CHUNK_SHIFT) == jnp.right_shift(c, CHUNK_SHIFT)
    if reverse:
        tri = (c > r) if strict else (c >= r)
    else:
        tri = (c < r) if strict else (c <= r)
    return same, jnp.logical_and(same, tri)


def _chunk_sums(ld, same, tri):
    tb = ld.shape[0]
    sel = jnp.concatenate([jnp.where(tri, 1.0, 0.0), jnp.where(same, 1.0, 0.0)], axis=0).astype(BF16)
    hi = ld.astype(BF16)
    rest = ld - hi.astype(F32)
    mid = rest.astype(BF16)
    lo = (rest - mid.astype(F32)).astype(BF16)
    dot = lambda p: jnp.dot(sel, p, preferred_element_type=F32)
    g = (dot(lo) + dot(mid)) + dot(hi)
    return g[:tb], g[tb:]


def _dot_tn(a, b):
    return lax.dot_general(a, b, (((0,), (0,)), ((), ())), preferred_element_type=F32)


def _dot_nt(a, b):
    return lax.dot_general(a, b, (((1,), (1,)), ((), ())), preferred_element_type=F32)


def _chunkrec_core(load_head, n_heads, tb, V, reverse, st_ref, dst_ref):
    @pl.when(pl.program_id(0) == 0)
    def _():
        st_ref[...] = jnp.zeros_like(st_ref)

    same, tri = _chunk_masks(tb, reverse, strict=False)
    loaded = [load_head(h) for h in range(n_heads)]
    K = loaded[0][3].shape[1]
    g_cum_all, g_tot_all = _chunk_sums(jnp.concatenate([ld for _, _, _, ld in loaded], axis=1), same, tri)
    per_head = []
    for h in range(n_heads):
        q, k, v, _ = loaded[h]
        g_cum, g_tot = g_cum_all[:, h * K:(h + 1) * K], g_tot_all[:, h * K:(h + 1) * K]
        q_dec = (q * jnp.exp(g_cum)).astype(BF16)
        k_inv = (k * jnp.exp(-g_cum)).astype(BF16)
        k_tail = k * jnp.exp(g_tot - g_cum)
        a = jnp.where(tri, _dot_nt(q_dec, k_inv), 0.0)
        o_intra = jnp.dot(a.astype(BF16), v.astype(BF16), preferred_element_type=F32)
        per_head.append((q_dec, k_tail, v, g_tot, o_intra))
    n_chunks = tb // CHUNK
    order = range(n_chunks - 1, -1, -1) if reverse else range(n_chunks)
    sts = [st_ref[h] for h in range(n_heads)]
    for c in order:
        sl = slice(c * CHUNK, (c + 1) * CHUNK)
        for h in range(n_heads):
            q_dec, k_tail, v, g_tot, o_intra = per_head[h]
            st = sts[h]
            dst_ref[sl, h * V:(h + 1) * V] = o_intra[sl] + _dot_nt(q_dec[sl], st.astype(BF16))
            d = jnp.exp(g_tot[c * CHUNK:c * CHUNK + 1, :])
            sts[h] = st * d + _dot_tn(v[sl], k_tail[sl])
    for h in range(n_heads):
        st_ref[h] = sts[h]


def _gated_head_norm(o, gain, gate_raw):
    y = o * lax.rsqrt(jnp.mean(o * o, axis=-1, keepdims=True) + NORM_EPS) * gain
    return y * (gate_raw * jax.nn.sigmoid(gate_raw))


def _chunk_mixer_kernel(*refs, reverse, n_heads, K, V, load_heads):
    if reverse:
        *in_refs, o_ref, st_ref = refs
        dst_ref = o_ref
    else:
        *in_refs, gate_ref, orev_ref, gain_ref, o_ref, st_ref, dst_ref = refs
    tb = o_ref.shape[0]
    _chunkrec_core(load_heads(*in_refs), n_heads, tb, V, reverse, st_ref, dst_ref)
    if not reverse:
        for h in range(n_heads):
            vs = slice(h * V, (h + 1) * V)
            o_ref[:, vs] = _gated_head_norm(dst_ref[:, vs] + orev_ref[:, vs], gain_ref[...],
                                            gate_ref[:, vs]).astype(o_ref.dtype)


def _hgrn_heads(qh_ref, ih_ref, fr_ref, lb_ref):
    def load(h):
        cs = slice(h * 128, (h + 1) * 128)
        lb = lb_ref[:, cs]
        f = lb + (1.0 - lb) * jax.nn.sigmoid(fr_ref[:, cs])
        qh = qh_ref[:, cs]
        return qh * jax.nn.sigmoid(qh), 1.0 - f, ih_ref[:, cs], jnp.log(f)
    return load


def _gla_heads(gq_ref, gk_ref, gv_ref, gd_ref, up_ref, bias_ref):
    x = jnp.dot(gd_ref[...].astype(BF16), up_ref[...].astype(BF16), preferred_element_type=F32) + bias_ref[...]
    lg = (jnp.minimum(x, 0.0) - jnp.log(1.0 + jnp.exp(-jnp.abs(x)))) * (1.0 / GLA_GATE_NORMALIZER)

    def load(h):
        ks, vs = slice(h * 128, (h + 1) * 128), slice(h * 256, (h + 1) * 256)
        return gq_ref[:, ks] * (128 ** -0.5), gk_ref[:, ks], gv_ref[:, vs], lg[:, ks]
    return load


def _seq_block_index(nblk, nctx, reverse):
    if not reverse:
        return lambda t: t
    return lambda t: jnp.where(t < nctx, nctx - 1 - t, nblk - 1 - (t - nctx))


def chunk_mixer(load_heads, inputs, n_heads, K, V, n_ctx_rows, reverse, final=None):
    T = inputs[0][0].shape[0]
    tb = TOK_BLOCK
    assert T % tb == 0 and n_ctx_rows % tb == 0
    nblk, nctx = T // tb, n_ctx_rows // tb
    blk = _seq_block_index(nblk, nctx, reverse)

    def spec(item):
        if len(item) == 1:
            return pl.BlockSpec(item[0].shape, lambda t: (0, 0))
        _, width, cb = item
        return pl.BlockSpec((tb, width), lambda t: (blk(t), cb))

    HV = n_heads * V
    ospec = pl.BlockSpec((tb, HV), lambda t: (blk(t), 0))
    scratch = [pltpu.VMEM((n_heads, V, K), F32)]
    if not reverse:
        gate, o_rev, gain = final
        inputs = list(inputs) + [gate, (o_rev, HV, 0), (gain.reshape(1, V),)]
        scratch.append(pltpu.VMEM((tb, HV), F32))
    return pl.pallas_call(
        functools.partial(_chunk_mixer_kernel, reverse=reverse, n_heads=n_heads, K=K, V=V, load_heads=load_heads),
        grid=(nblk,),
        in_specs=[spec(it) for it in inputs],
        out_specs=ospec,
        out_shape=jax.ShapeDtypeStruct((T, HV), F32 if reverse else BF16),
        scratch_shapes=scratch,
        compiler_params=_params("arbitrary"),
        name="chunk_mixer",
    )(*[it[0] for it in inputs])


def _mm_bf(a, b):
    return jnp.dot(a.astype(BF16), b.astype(BF16), preferred_element_type=F32)


def _softplus(x):
    return jnp.maximum(x, 0.0) + jnp.log(1.0 + jnp.exp(-jnp.abs(x)))


def _head_sums(x, bd_ref):
    hi = x.astype(BF16)
    lo = (x - hi.astype(F32)).astype(BF16)
    bd = bd_ref[...]
    W = bd.shape[0]
    groups = []
    for g in range(x.shape[1] // W):
        gs = slice(g * W, (g + 1) * W)
        groups.append(jnp.dot(lo[:, gs], bd, preferred_element_type=F32)
                      + jnp.dot(hi[:, gs], bd, preferred_element_type=F32))
    return jnp.concatenate(groups, axis=1)


def _rwkv_prep_kernel(p_ref, pp_ref, pn_ref, g_ref, gp_ref, gn_ref, s_ref, sp_ref, sn_ref,
                      mup_ref, mun_ref, mugp_ref, mugn_ref, musp_ref, musn_ref,
                      w2f_ref, w2b_ref, a2_ref, g2_ref, w0f_ref, w0b_ref, a0_ref, kk_ref, ka_ref, rk_ref, bd_ref,
                      r_out, k_out, v_out, kk_out, b_out, lwf_out, lwb_out, go_out, bon_out,
                      gcf_out, gtf_out, gcb_out, gtb_out, *, n_ctx_blocks):
    t, nblk = pl.program_id(0), pl.num_programs(0)
    tb = p_ref.shape[0]
    C = RWKV_HEADS * RWKV_N
    keep_prev = jnp.where(jnp.logical_or(t == 0, t == n_ctx_blocks), 0.0, 1.0)
    keep_next = jnp.where(jnp.logical_or(t == n_ctx_blocks - 1, t == nblk - 1), 0.0, 1.0)

    def shifted(x_ref, xp_ref, xn_ref, mu_p_ref, mu_n_ref):
        x = x_ref[...]
        rows = lax.broadcasted_iota(jnp.int32, x.shape, 0)
        prev = jnp.where(rows == 0, xp_ref[7:8, :] * keep_prev, pltpu.roll(x, 1, 0))
        nxt = jnp.where(rows == tb - 1, xn_ref[0:1, :] * keep_next, pltpu.roll(x, tb - 1, 0))
        return x + mu_p_ref[...] * (prev - x) + mu_n_ref[...] * (nxt - x)

    rkv = shifted(p_ref, pp_ref, pn_ref, mup_ref, mun_ref)
    rr, rk, rv = rkv[:, :C], rkv[:, C:2 * C], rkv[:, 2 * C:]
    low = shifted(s_ref, sp_ref, sn_ref, musp_ref, musn_ref)
    gd = shifted(g_ref, gp_ref, gn_ref, mugp_ref, mugn_ref)
    tl = jnp.tanh(low).astype(BF16)
    for w0_ref, w2_ref, lw_out, gc_out, gt_out, reverse in ((w0f_ref, w2f_ref, lwf_out, gcf_out, gtf_out, False),
                                                            (w0b_ref, w2b_ref, lwb_out, gcb_out, gtb_out, True)):
        lw = -jnp.exp(-_softplus(-(w0_ref[...] + jnp.dot(tl, w2_ref[...], preferred_element_type=F32))) - 0.5)
        lw_out[...] = lw
        same, tri = _chunk_masks(tb, reverse, strict=False)
        gc_out[...], gt_out[...] = _chunk_sums(lw, same, tri)
    a_sig = jax.nn.sigmoid(a0_ref[...] + jnp.dot(low.astype(BF16), a2_ref[...], preferred_element_type=F32))
    go_out[...] = jnp.dot(jax.nn.sigmoid(gd).astype(BF16), g2_ref[...], preferred_element_type=F32)
    kk = rk * kk_ref[...]
    kk = kk * lax.rsqrt(_head_sums(kk * kk, bd_ref) + 1e-12)
    k_mod = rk * (1.0 + (a_sig - 1.0) * ka_ref[...])
    r_out[...] = rr
    k_out[...] = k_mod
    v_out[...] = rv
    kk_out[...] = kk
    b_out[...] = kk * a_sig
    bon_out[...] = _head_sums(rr * k_mod * rk_ref[...], bd_ref) * rv


def _rwkv_pre_kernel(r_ref, k_ref, v_ref, kk_ref, b_ref, lw_ref, gc_ref, gt_ref, *rest, reverse):
    n_cast = (len(rest) - 4) // 2
    w32_refs, (qp_ref, ol_ref, plt_ref, zt_ref), w16_refs = rest[:n_cast], rest[n_cast:n_cast + 4], rest[n_cast + 4:]
    for w32_ref, w16_ref in zip(w32_refs, w16_refs):
        w16_ref[...] = w32_ref[...].astype(BF16)
    N = RWKV_N
    tb = r_ref.shape[0]
    hb = r_ref.shape[1] // N
    same, tri_incl = _chunk_masks(tb, reverse, strict=False)
    _, tri_strict = _chunk_masks(tb, reverse, strict=True)
    xs, lps, rest = [], [], []
    for h in range(hb):
        hs = slice(h * N, (h + 1) * N)
        r, k, v, b, lw, g_cum, g_tot = (ref[:, hs] for ref in (r_ref, k_ref, v_ref, b_ref, lw_ref, gc_ref, gt_ref))
        a = -kk_ref[:, hs]
        e_neg = jnp.exp(-g_cum)
        e_tail = jnp.exp(g_tot - g_cum)
        a_t = (a * jnp.exp(g_cum - lw)).astype(BF16)
        r_t = (r * jnp.exp(g_cum)).astype(BF16)
        b_t = (b * e_neg).astype(BF16)
        k_t = (k * e_neg).astype(BF16)
        a_ab = jnp.where(tri_strict, _dot_nt(a_t, b_t), 0.0)
        a_ak = jnp.where(tri_strict, _dot_nt(a_t, k_t), 0.0)
        a_rb = jnp.where(tri_incl, _dot_nt(r_t, b_t), 0.0)
        a_rk = jnp.where(tri_incl, _dot_nt(r_t, k_t), 0.0)
        xs.append(jnp.concatenate([a_t.astype(F32), _mm_bf(a_ak, v)], axis=1))
        lps.append(a_ab)
        rest.append((v, b * e_tail, k * e_tail, a_rb,
                     jnp.concatenate([r_t.astype(F32), _mm_bf(a_rk, v)], axis=1)))
    for j in range(CHUNK_SHIFT):
        xs = [x + _mm_bf(lp, x) for x, lp in zip(xs, lps)]
        if j < CHUNK_SHIFT - 1:
            lps = [_mm_bf(lp, lp) for lp in lps]
    for h in range(hb):
        v, b_h, k_h, a_rb, qo0 = rest[h]
        x = xs[h]
        qo = qo0 + _mm_bf(a_rb, x)
        qp_ref[:, h * N:(h + 1) * N] = qo[:, :N]
        ol_ref[:, h * N:(h + 1) * N] = qo[:, N:]
        w, uloc = x[:, :N], x[:, N:]
        for c in range(tb // CHUNK):
            sl = slice(c * CHUNK, (c + 1) * CHUNK)
            plt_ref[h, c] = _dot_tn(w[sl], b_h[sl])
            zt_ref[h, c] = _dot_tn(uloc[sl], b_h[sl]) + _dot_tn(v[sl], k_h[sl])


def _rwkv_scan_kernel(qp_ref, ol_ref, plt_ref, zt_ref, gt_ref, *rest, reverse):
    if reverse:
        o_ref, st_ref = rest
        dst_ref = o_ref
    else:
        orev_ref, go_ref, bon_ref, lnw_ref, lnb_ref, bd_ref, o_ref, st_ref, dst_ref = rest

    @pl.when(pl.program_id(0) == 0)
    def _():
        st_ref[...] = jnp.zeros_like(st_ref)

    N = RWKV_N
    tb = qp_ref.shape[0]
    n_heads = qp_ref.shape[1] // N
    n_chunks = tb // CHUNK
    order = range(n_chunks - 1, -1, -1) if reverse else range(n_chunks)
    sts = [st_ref[h] for h in range(n_heads)]
    for c in order:
        sl = slice(c * CHUNK, (c + 1) * CHUNK)
        for h in range(n_heads):
            hs = slice(h * N, (h + 1) * N)
            st = sts[h]
            dst_ref[sl, hs] = _dot_nt(qp_ref[sl, hs].astype(BF16), st.astype(BF16)) + ol_ref[sl, hs]
            d = jnp.exp(gt_ref[c * CHUNK:c * CHUNK + 1, hs])
            sts[h] = st * d + _mm_bf(st, plt_ref[h, c]) + zt_ref[h, c]
    for h in range(n_heads):
        st_ref[h] = sts[h]
    if not reverse:
        o = dst_ref[...] + orev_ref[...]
        dev = o - _head_sums(o, bd_ref) * (1.0 / N)
        var = _head_sums(dev * dev, bd_ref) * (1.0 / N)
        on = dev * lax.rsqrt(var + RWKV_LN_EPS) * lnw_ref[...] + lnb_ref[...]
        o_ref[...] = ((on + bon_ref[...]) * go_ref[...]).astype(o_ref.dtype)


def rwkv7_branch(pb, ps, rkv_col, gd_col, n_ctx_rows, mu_prev, mu_next, low_cols, w0_f, w0_b, w2_f, w2_b,
                 a0, a2, g2, k_k, k_a, r_k, ln_w, ln_b, to_bf16):
    T = pb.shape[0]
    H, N = RWKV_HEADS, RWKV_N
    C = H * N
    tb = TOK_BLOCK
    assert T % tb == 0 and n_ctx_rows % tb == 0
    nblk, nctx = T // tb, n_ctx_rows // tb
    ncb = tb // CHUNK
    GW, SW, R = g2.shape[0], ps.shape[1], w2_f.shape[0]
    wf_col, wb_col, ad_col = low_cols
    s0 = 3 * C
    row = lambda v: v.reshape(1, -1)

    def padded_rows(w, r0):
        return jnp.zeros((SW, C), F32).at[r0:r0 + w.shape[0]].set(w).astype(BF16)

    def mu_low(mu):
        return jnp.zeros((1, SW), F32).at[0, wf_col:wf_col + 3 * R].set(mu[s0:s0 + 3 * R])

    bdw = 2 * LANES
    bd = (jnp.arange(bdw)[:, None] // N == jnp.arange(bdw)[None, :] // N).astype(BF16)
    r8 = tb // 8

    def halo(width, col):
        return (pl.BlockSpec((tb, width), lambda t: (t, col)),
                pl.BlockSpec((8, width), lambda t: (jnp.maximum(t * r8 - 1, 0), col)),
                pl.BlockSpec((8, width), lambda t: (jnp.minimum((t + 1) * r8, T // 8 - 1), col)))

    whole = lambda a: pl.BlockSpec(a.shape, lambda t: (0, 0))
    consts = [row(mu_prev[:s0]), row(mu_next[:s0]), row(mu_prev[s0 + 3 * R:]), row(mu_next[s0 + 3 * R:]),
              mu_low(mu_prev), mu_low(mu_next),
              padded_rows(w2_f, wf_col), padded_rows(w2_b, wb_col), padded_rows(a2, ad_col), g2.astype(BF16),
              row(w0_f), row(w0_b), row(a0), row(k_k), row(k_a), row(r_k), bd]
    tokC = pl.BlockSpec((tb, C), lambda t: (t, 0))
    shapeC = jax.ShapeDtypeStruct((T, C), F32)
    r, k, v, kk, b, lw_f, lw_b, g_out, bonus, gc_f, gt_f, gc_b, gt_b = pl.pallas_call(
        functools.partial(_rwkv_prep_kernel, n_ctx_blocks=nctx),
        grid=(nblk,),
        in_specs=[*halo(3 * C, rkv_col), *halo(GW, gd_col), *halo(SW, 0), *[whole(a) for a in consts]],
        out_specs=[tokC] * 13,
        out_shape=[shapeC] * 13,
        compiler_params=_params("arbitrary"),
        name="rwkv_prep",
    )(pb, pb, pb, pb, pb, pb, ps, ps, ps, *consts)

    hb = RWKV_HEAD_BLOCK
    tok = pl.BlockSpec((tb, hb * N), lambda h, t: (t, h))
    mat = pl.BlockSpec((hb, ncb, N, N), lambda h, t: (h, t, 0, 0))
    mat_shape = jax.ShapeDtypeStruct((H, T // CHUNK, N, N), F32)
    o_rev = None
    cast_done = []
    def cast_specs(job):
        w32, row0, wr = job
        rows = BF16_SUBLANES * pl.cdiv(wr, BF16_SUBLANES * (H // hb) * nblk)
        assert wr % rows == 0 and row0 % rows == 0
        block = lambda h, t: jnp.minimum(h * nblk + t, wr // rows - 1)
        return (pl.BlockSpec((rows, w32.shape[1]), lambda h, t: (row0 // rows + block(h, t), 0)),
                pl.BlockSpec((rows, w32.shape[1]), lambda h, t: (block(h, t), 0)),
                jax.ShapeDtypeStruct((wr, w32.shape[1]), BF16))

    for lw, gc, gt, jobs, reverse in ((lw_b, gc_b, gt_b, to_bf16[0], True), (lw_f, gc_f, gt_f, to_bf16[1], False)):
        cin, cout, cshape = zip(*[cast_specs(job) for job in jobs])
        qp, ol, plt, zt, *w16s = pl.pallas_call(
            functools.partial(_rwkv_pre_kernel, reverse=reverse),
            grid=(H // hb, nblk),
            in_specs=[tok] * 8 + list(cin),
            out_specs=[tok, tok, mat, mat] + list(cout),
            out_shape=[shapeC, shapeC, mat_shape, mat_shape] + list(cshape),
            compiler_params=_params("arbitrary", "arbitrary"),
            name="rwkv_pre",
        )(r, k, v, kk, b, lw, gc, gt, *[job[0] for job in jobs])
        cast_done.append(w16s)
        blk = _seq_block_index(nblk, nctx, reverse)
        tok_all = pl.BlockSpec((tb, C), lambda t: (blk(t), 0))
        mat_all = pl.BlockSpec((H, ncb, N, N), lambda t: (0, blk(t), 0, 0))
        in_specs = [tok_all, tok_all, mat_all, mat_all, tok_all]
        args = [qp, ol, plt, zt, gt]
        scratch = [pltpu.VMEM((H, N, N), F32)]
        if not reverse:
            in_specs += [tok_all, tok_all, tok_all, whole(row(ln_w)), whole(row(ln_b)), whole(bd)]
            args += [o_rev, g_out, bonus, row(ln_w), row(ln_b), bd]
            scratch.append(pltpu.VMEM((tb, C), F32))
        out = pl.pallas_call(
            functools.partial(_rwkv_scan_kernel, reverse=reverse),
            grid=(nblk,),
            in_specs=in_specs,
            out_specs=tok_all,
            out_shape=jax.ShapeDtypeStruct((T, C), F32 if reverse else BF16),
            scratch_shapes=scratch,
            compiler_params=_params("arbitrary"),
            name="rwkv_scan",
        )(*args)
        if reverse:
            o_rev = out
    return out, cast_done


def _row_copy(src_hbm, src_row, dst_ref, dst_row, sem):
    return pltpu.make_async_copy(src_hbm.at[pl.ds(src_row, 1)], dst_ref.at[pl.ds(dst_row, 1)], sem)


def _gather_kernel(src_ref, nxt_ref, x_hbm, o_ref, buf, sems):
    i, n_steps = pl.program_id(0), pl.num_programs(0)
    n = o_ref.shape[0]

    def issue(idx_ref, slot):
        def body(r, carry):
            for u in range(2):
                row = 2 * r + u
                _row_copy(x_hbm, idx_ref[0, 0, row], buf.at[slot], row, sems.at[slot]).start(priority=u)
            return carry
        lax.fori_loop(0, n // 2, body, 0, unroll=ROW_COPY_UNROLL)

    @pl.when(i == 0)
    def _():
        issue(src_ref, 0)

    @pl.when(i + 1 < n_steps)
    def _():
        issue(nxt_ref, (i + 1) % 2)

    slot = i % 2

    pltpu.make_async_copy(x_hbm.at[pl.ds(0, n)], buf.at[slot], sems.at[slot]).wait()
    o_ref[...] = buf[slot].astype(o_ref.dtype)


def gather_rows(x, src, tg):
    S, D = x.shape
    P = src.shape[0]
    sub = D // LANES
    assert P % tg == 0 and x.dtype == BF16 and sub == BF16_SUBLANES
    n_steps = P // tg
    src3 = src.reshape(n_steps, 1, tg)
    out = pl.pallas_call(
        _gather_kernel,
        grid=(n_steps,),
        in_specs=[pl.BlockSpec((1, 1, tg), lambda i: (i, 0, 0), memory_space=pltpu.SMEM),
                  pl.BlockSpec((1, 1, tg), lambda i: (jnp.minimum(i + 1, n_steps - 1), 0, 0),
                               memory_space=pltpu.SMEM),
                  pl.BlockSpec(memory_space=pl.ANY)],
        out_specs=pl.BlockSpec((tg, sub, LANES), lambda i: (i, 0, 0)),
        out_shape=jax.ShapeDtypeStruct((P, sub, LANES), BF16),
        scratch_shapes=[pltpu.VMEM((2, tg, sub, LANES), BF16), pltpu.SemaphoreType.DMA((2,))],
        compiler_params=_params("arbitrary"),
        name="gather_rows",
    )(src3, src3, x.reshape(S, sub, LANES))
    return out.reshape(P, D)


def _combine_kernel(pos_ref, nxt_ref, x_ref, w_ref, ys_hbm, gpost_ref, g_ref, o_ref, buf, sems):
    i, n_steps = pl.program_id(0), pl.num_programs(0)
    n = x_ref.shape[0]

    def issue(idx_ref, slot):
        def body(r, carry):
            for k in range(2):
                _row_copy(ys_hbm, idx_ref[0, 0, 2 * r + k], buf.at[slot, k], r, sems.at[slot, k]).start(priority=k)
            return carry
        lax.fori_loop(0, n, body, 0, unroll=ROW_COPY_UNROLL)

    @pl.when(i == 0)
    def _():
        issue(pos_ref, 0)

    @pl.when(i + 1 < n_steps)
    def _():
        issue(nxt_ref, (i + 1) % 2)

    slot = i % 2

    for k in range(2):
        pltpu.make_async_copy(ys_hbm.at[pl.ds(0, n)], buf.at[slot, k], sems.at[slot, k]).wait()
    w = w_ref[...]
    f = w[:, 0:1] * buf[slot, 0] + w[:, 1:2] * buf[slot, 1]
    o_ref[...] = x_ref[...] + g_ref[0] * _rms(f, gpost_ref[...])


def combine_resid(x, ys, pos, weights, gain_post, mods, g_slot):
    S, D = x.shape
    tc = TOK_BLOCK
    assert S % tc == 0
    n_steps = S // tc
    row = pl.BlockSpec((tc, D), lambda i: (i, 0))
    pos3 = pos.reshape(n_steps, 1, 2 * tc)
    return pl.pallas_call(
        _combine_kernel,
        grid=(n_steps,),
        in_specs=[pl.BlockSpec((1, 1, 2 * tc), lambda i: (i, 0, 0), memory_space=pltpu.SMEM),
                  pl.BlockSpec((1, 1, 2 * tc), lambda i: (jnp.minimum(i + 1, n_steps - 1), 0, 0),
                               memory_space=pltpu.SMEM),
                  row, pl.BlockSpec((tc, 2), lambda i: (i, 0)), pl.BlockSpec(memory_space=pl.ANY),
                  pl.BlockSpec((1, D), lambda i: (0, 0)), _mod_spec(D, g_slot, 0)],
        out_specs=row,
        out_shape=jax.ShapeDtypeStruct((S, D), F32),
        scratch_shapes=[pltpu.VMEM((2, 2, tc, D), F32), pltpu.SemaphoreType.DMA((2, 2))],
        compiler_params=_params("arbitrary"),
        name="combine_resid",
    )(pos3, pos3, x, weights, ys, gain_post.reshape(1, D), mods)


def _route(logits, tm):
    S = logits.shape[0]
    top_val, top_idx = lax.top_k(logits, 2)
    weights = jax.nn.softmax(top_val, axis=-1)
    e_flat = top_idx.reshape(-1)
    onehot = (e_flat[:, None] == jnp.arange(N_EXPERTS)[None, :]).astype(jnp.int32)
    rank = jnp.take_along_axis(jnp.cumsum(onehot, axis=0), e_flat[:, None], axis=1)[:, 0] - 1
    counts = jnp.sum(onehot, axis=0)
    tiles_per = (counts + tm - 1) // tm
    tile_end = jnp.cumsum(tiles_per)
    start = (tile_end - tiles_per) * tm
    dest = start[e_flat] + rank
    n_rows = 2 * S + N_EXPERTS * tm
    n_tiles = n_rows // tm
    token = jnp.arange(2 * S, dtype=jnp.int32) // 2
    src = jnp.zeros((n_rows,), jnp.int32).at[dest].set(token)
    n_used = tile_end[-1].astype(jnp.int32)
    tile_id = jnp.minimum(jnp.arange(n_tiles, dtype=jnp.int32), n_used - 1)
    tile_expert = jnp.sum((tile_end[None, :] <= tile_id[:, None]).astype(jnp.int32), axis=1)
    tile_expert = jnp.minimum(tile_expert, N_EXPERTS - 1)
    return src, weights, dest.astype(jnp.int32), tile_expert, n_used.reshape(1)


def _adaln(c, c_ctx, w, b):
    D = c.shape[-1]
    rows = jnp.zeros((8, D), F32).at[0].set(c[0]).at[1].set(c_ctx)
    m = matmul(jax.nn.silu(rows), w, 8, 2048)[:2] + b[None, :]
    return m.reshape(12, 1, D)


def _head_rms(o, gain, n_heads):
    T = o.shape[0]
    oh = o.reshape(T, n_heads, -1)
    y = oh * lax.rsqrt(jnp.mean(oh * oh, axis=-1, keepdims=True) + NORM_EPS) * gain
    return y.reshape(T, -1)


def _even_mixer(h, L, w_in, w_out, attn_sink, hgrn_norm, hgrn_lb):
    T = h.shape[0]
    S = T - L
    proj = matmul(h, w_in, 1280, 512, col_roll=1536 // 512)
    att = sink_attention(proj, 5, 24, 25, L, attn_sink, _rope_tables(S))
    att_c = sink_attention(proj, 5, 24, 25, L, attn_sink, None)
    lb = (hgrn_lb.reshape(1, -1),)
    qh, ih = (proj, 1024, 0), (proj, 1024, 1)
    o_rev = chunk_mixer(_hgrn_heads, [qh, ih, (proj, 1024, 3), lb], 8, 128, 128, L, True)
    hg = chunk_mixer(_hgrn_heads, [qh, ih, (proj, 1024, 2), lb], 8, 128, 128, L, False,
                     final=((proj, 1024, 4), o_rev, hgrn_norm))
    ycat = jnp.concatenate([jnp.concatenate([att_c, att], axis=0), hg], axis=-1)
    return matmul(ycat, w_out, 1280, 1024)


def _token_shift(p, L, mu_prev, mu_next):
    def one(s):
        zero = jnp.zeros_like(s[:1])
        prev = jnp.concatenate([zero, s[:-1]], axis=0)
        nxt = jnp.concatenate([s[1:], zero], axis=0)
        return s + mu_prev * (prev - s) + mu_next * (nxt - s)
    return jnp.concatenate([one(p[:L]), one(p[L:])], axis=0)


def _odd_mixer(h, L, w_in, w_out, gla_gate_up_f, gla_gate_up_b, gla_gate_bias_f, gla_gate_bias_b, gla_norm,
               mu_prev, mu_next, w0_f, w0_b, w2_f, w2_b, a0, a2, g2, k_k, k_a, r_k, ln_w, ln_b, to_bf16):
    T = h.shape[0]
    GO = 3104
    cols = lambda a, b: w_in[:, a:b]
    w_big = jnp.concatenate([cols(0, 2048), cols(2080, 3104), cols(GO, GO + 3072), cols(GO + 3360, GO + 3616)], axis=1)
    w_small = jnp.concatenate([cols(2048, 2080), cols(GO + 3072, GO + 3360),
                               jnp.zeros((w_in.shape[0], 64), F32)], axis=1)
    pb = matmul(h, w_big, 1280, 640)
    ps = matmul(h, w_small, 1280, 384)
    def gla_inputs(up, row0, bias):
        up_rows = jnp.zeros((ps.shape[1], up.shape[1]), F32).at[row0:row0 + up.shape[0]].set(up)
        return [(pb, 512, 0), (pb, 512, 1), (pb, 1024, 1), (ps, ps.shape[1], 0), (up_rows,), (bias.reshape(1, -1),)]
    o_rev = chunk_mixer(_gla_heads, gla_inputs(gla_gate_up_b, 16, gla_gate_bias_b), 4, 128, 256, L, True)
    gla = chunk_mixer(_gla_heads, gla_inputs(gla_gate_up_f, 0, gla_gate_bias_f), 4, 128, 256, L, False,
                      final=((pb, 1024, 2), o_rev, gla_norm))
    rw, cast_done = rwkv7_branch(pb, ps, 1, 24, L, mu_prev, mu_next, (32, 128, 224), w0_f, w0_b, w2_f, w2_b,
                                 a0, a2, g2, k_k, k_a, r_k, ln_w, ln_b, to_bf16)
    ycat = jnp.concatenate([gla[L:], rw[L:]], axis=-1)
    return matmul(ycat, w_out, 1024, 1024), cast_done


def kernel(x, c, ctx, c_ctx, hgrn_lb_logits, l0_ada_w, l0_ada_b, l0_norm_mix_pre, l0_norm_mix_post, l0_norm_ffn_pre, l0_norm_ffn_post, l0_w_in, l0_w_out, l0_attn_sink, l0_hgrn_norm, l0_ffn_w_gate, l0_ffn_w_up, l0_ffn_w_down, l1_ada_w, l1_ada_b, l1_norm_mix_pre, l1_norm_mix_post, l1_norm_ffn_pre, l1_norm_ffn_post, l1_w_in, l1_w_out, l1_gla_gate_up_f, l1_gla_gate_up_b, l1_gla_gate_bias_f, l1_gla_gate_bias_b, l1_gla_norm, l1_rwkv_mu_prev, l1_rwkv_mu_next, l1_rwkv_w0_f, l1_rwkv_w0_b, l1_rwkv_w2_f, l1_rwkv_w2_b, l1_rwkv_a0, l1_rwkv_a2, l1_rwkv_g2, l1_rwkv_k_k, l1_rwkv_k_a, l1_rwkv_r_k, l1_rwkv_ln_w, l1_rwkv_ln_b, l1_moe_router, l1_moe_w_gate, l1_moe_w_up, l1_moe_w_down):
    B, S, D = x.shape
    L = ctx.shape[1]
    assert B == 1
    T = L + S
    SH1, SC1, G1, SH2, SC2, G2 = range(6)
    xa = jnp.concatenate([ctx[0], x[0]], axis=0)
    hgrn_lb = jnp.cumsum(jax.nn.softmax(hgrn_lb_logits.astype(F32), axis=0), axis=0)
    m0 = _adaln(c, c_ctx, l0_ada_w, l0_ada_b)
    m1 = _adaln(c, c_ctx, l1_ada_w, l1_ada_b)

    h = normmod(xa, l0_norm_mix_pre, m0, SH1, SC1, L)
    y = _even_mixer(h, L, l0_w_in, l0_w_out, l0_attn_sink, l0_hgrn_norm, hgrn_lb[0])
    xa, h = resid_norm(xa, y, l0_norm_mix_post, m0, G1, L, nxt=(l0_norm_ffn_pre, m0, SH2, SC2))
    n_t = T // DENSE_TILE
    f = swiglu_ffn(h, l0_ffn_w_gate[None].astype(BF16), l0_ffn_w_up[None].astype(BF16),
                   [l0_ffn_w_down[None].astype(BF16)],
                   jnp.zeros((n_t,), jnp.int32), jnp.full((1,), n_t, jnp.int32), DENSE_TILE, DENSE_FF_BLOCK)
    xa, h = resid_norm(xa, f, l0_norm_ffn_post, m0, G2, L, nxt=(l1_norm_mix_pre, m1, SH1, SC1))

    E, _, FE = l1_moe_w_gate.shape
    wg2, wu2, wd2 = (l1_moe_w_gate.reshape(E * D, FE), l1_moe_w_up.reshape(E * D, FE),
                     l1_moe_w_down.reshape(E * FE, D))
    half = E * FE // 2
    y, ((wg16, wd16_lo), (wu16, wd16_hi)) = _odd_mixer(
        h, L, l1_w_in, l1_w_out, l1_gla_gate_up_f, l1_gla_gate_up_b, l1_gla_gate_bias_f,
        l1_gla_gate_bias_b, l1_gla_norm, l1_rwkv_mu_prev, l1_rwkv_mu_next, l1_rwkv_w0_f, l1_rwkv_w0_b,
        l1_rwkv_w2_f, l1_rwkv_w2_b, l1_rwkv_a0, l1_rwkv_a2, l1_rwkv_g2, l1_rwkv_k_k, l1_rwkv_k_a,
        l1_rwkv_r_k, l1_rwkv_ln_w, l1_rwkv_ln_b,
        to_bf16=([(wg2, 0, E * D), (wd2, 0, half)], [(wu2, 0, E * D), (wd2, half, half)]))
    xl, h = resid_norm(xa, y, l1_norm_mix_post, m1, G1, 0, nxt=(l1_norm_ffn_pre, m1, SH2, SC2), x_row0=L)
    router = jnp.concatenate([l1_moe_router, jnp.zeros((D, 128 - N_EXPERTS), F32)], axis=1)
    logits = matmul(h, router, 1024, 128)[:, :N_EXPERTS]
    src, gate_w, dest, tile_expert, n_used = _route(logits, MOE_TILE)
    hs = gather_rows(h, src, TOK_BLOCK)
    ys = swiglu_ffn(hs, wg16.reshape(E, D, FE), wu16.reshape(E, D, FE),
                    [wd16_lo.reshape(E // 2, FE, D), wd16_hi.reshape(E // 2, FE, D)],
                    tile_expert, n_used, MOE_TILE, 1024)
    out = combine_resid(xl, ys, dest, gate_w, l1_norm_ffn_post, m1, G2)
    return out[None]
```

```python
import functools

import jax
import jax.numpy as jnp
from jax import lax
from jax.experimental import pallas as pl
from jax.experimental.pallas import tpu as pltpu

F32 = jnp.float32
BF16 = jnp.bfloat16

NORM_EPS = 1e-6
CHUNK = 32
TOK_BLOCK = 256
ATT_BLOCK = 128
WINDOW = 128
HEAD_DIM = 128
ATT_HEADS = 8
ATT_KV_HEADS = 2
ROPE_THETA = 10000.0
GRID_W = 64
RWKV_N = 64
RWKV_HEADS = 16
RWKV_LN_EPS = 64e-5
RWKV_HEAD_BLOCK = 8
GLA_GATE_NORMALIZER = 16.0
N_EXPERTS = 8
MOE_TILE = 512
DENSE_TILE = 416
DENSE_FF_BLOCK = 1408
MOE_FF_BLOCK = 1024
STREAM_ROWS = 1280
LATENT_ROWS = 1024
VMEM_LIMIT_BYTES = 56 * 1024 * 1024
LANES = 128
BF16_SUBLANES = 16
ROW_COPY_UNROLL = 8


def _params(*sem):
    return pltpu.CompilerParams(dimension_semantics=sem, vmem_limit_bytes=VMEM_LIMIT_BYTES)


def _mm_kernel(x_ref, w_ref, o_ref, wbf_ref):
    @pl.when(pl.program_id(1) == 0)
    def _():
        wbf_ref[...] = w_ref[...].astype(BF16)

    o_ref[...] = jnp.dot(x_ref[...].astype(BF16), wbf_ref[...],
                         preferred_element_type=F32).astype(o_ref.dtype)


def matmul(x, w, tm, tn, out_dtype=F32, col_roll=0):
    M, K = x.shape
    N = w.shape[1]
    assert M % tm == 0 and N % tn == 0, (M, tm, N, tn)
    nj = N // tn
    return pl.pallas_call(
        _mm_kernel,
        grid=(nj, M // tm),
        in_specs=[pl.BlockSpec((tm, K), lambda j, i: (i, 0)),
                  pl.BlockSpec((K, tn), lambda j, i: (0, jnp.where(j + col_roll >= nj, j + col_roll - nj,
                                                                    j + col_roll)))],
        out_specs=pl.BlockSpec((tm, tn), lambda j, i: (i, j)),
        out_shape=jax.ShapeDtypeStruct((M, N), out_dtype),
        scratch_shapes=[pltpu.VMEM((K, tn), BF16)],
        compiler_params=_params("arbitrary", "arbitrary"),
        name="matmul",
    )(x, w)


def _rms(x, gain):
    ms = jnp.mean(x * x, axis=-1, keepdims=True)
    return x * lax.rsqrt(ms + NORM_EPS) * gain


def _normmod_kernel(x_ref, gain_ref, sc_ref, sh_ref, h_ref):
    y = _rms(x_ref[...], gain_ref[...])
    h_ref[...] = (y * (1.0 + sc_ref[0]) + sh_ref[0]).astype(h_ref.dtype)


def _mod_spec(D, slot, n_ctx_tiles):
    return pl.BlockSpec((1, 1, D), lambda i: (jnp.where(i < n_ctx_tiles, 6, 0) + slot, 0, 0))


def normmod(x, gain, mods, sh_slot, sc_slot, n_ctx_rows):
    R, D = x.shape
    tr = TOK_BLOCK
    assert R % tr == 0 and n_ctx_rows % tr == 0
    nct = n_ctx_rows // tr
    return pl.pallas_call(
        _normmod_kernel,
        grid=(R // tr,),
        in_specs=[pl.BlockSpec((tr, D), lambda i: (i, 0)),
                  pl.BlockSpec((1, D), lambda i: (0, 0)),
                  _mod_spec(D, sc_slot, nct), _mod_spec(D, sh_slot, nct)],
        out_specs=pl.BlockSpec((tr, D), lambda i: (i, 0)),
        out_shape=jax.ShapeDtypeStruct((R, D), BF16),
        compiler_params=_params("arbitrary"),
        name="normmod",
    )(x, gain.reshape(1, D), mods, mods)


def _resid_kernel(x_ref, y_ref, gpost_ref, g_ref, *rest, with_next):
    xn = x_ref[...] + g_ref[0] * _rms(y_ref[...], gpost_ref[...])
    if with_next:
        gpre_ref, sc_ref, sh_ref, xo_ref, h_ref = rest
        xo_ref[...] = xn
        h_ref[...] = (_rms(xn, gpre_ref[...]) * (1.0 + sc_ref[0]) + sh_ref[0]).astype(h_ref.dtype)
    else:
        (xo_ref,) = rest
        xo_ref[...] = xn


def resid_norm(x, y, gain_post, mods, g_slot, n_ctx_rows, nxt=None, x_row0=0):
    R, D = y.shape
    tr = TOK_BLOCK
    assert R % tr == 0 and n_ctx_rows % tr == 0 and x_row0 % tr == 0 and x.shape[0] == R + x_row0
    nct = n_ctx_rows // tr
    row = pl.BlockSpec((tr, D), lambda i: (i, 0))
    vec = pl.BlockSpec((1, D), lambda i: (0, 0))
    in_specs = [pl.BlockSpec((tr, D), lambda i: (i + x_row0 // tr, 0)), row, vec, _mod_spec(D, g_slot, nct)]
    args = [x, y, gain_post.reshape(1, D), mods]
    out_specs = [row]
    out_shape = [jax.ShapeDtypeStruct((R, D), F32)]
    if nxt is not None:
        gain_pre, mods_n, sh_slot, sc_slot = nxt
        in_specs += [vec, _mod_spec(D, sc_slot, nct), _mod_spec(D, sh_slot, nct)]
        args += [gain_pre.reshape(1, D), mods_n, mods_n]
        out_specs.append(row)
        out_shape.append(jax.ShapeDtypeStruct((R, D), BF16))
    out = pl.pallas_call(
        functools.partial(_resid_kernel, with_next=nxt is not None),
        grid=(R // tr,),
        in_specs=in_specs, out_specs=out_specs, out_shape=out_shape,
        compiler_params=_params("arbitrary"),
        name="resid_norm",
    )(*args)
    return out if nxt is not None else out[0]


def _ffn_kernel(te_ref, nu_ref, x_ref, wg_ref, wu_ref, *rest, experts_per_part):
    *wd_refs, o_ref = rest
    i, j = pl.program_id(0), pl.program_id(1)

    @pl.when(i < nu_ref[0])
    def _():
        x = x_ref[...]
        g = jnp.dot(x, wg_ref[0], preferred_element_type=F32)
        u = jnp.dot(x, wu_ref[0], preferred_element_type=F32)
        hid = (g * jax.nn.sigmoid(g) * u).astype(BF16)

        wd = wd_refs[0][0]
        part_id = te_ref[i] // experts_per_part
        for p in range(1, len(wd_refs)):
            wd = jnp.where(part_id == p, wd_refs[p][0], wd)
        part = jnp.dot(hid, wd, preferred_element_type=F32)

        @pl.when(j == 0)
        def _():
            o_ref[...] = part

        @pl.when(j > 0)
        def _():
            o_ref[...] += part

    @pl.when(jnp.logical_and(i >= nu_ref[0], j == 0))
    def _():
        o_ref[...] = jnp.zeros_like(o_ref)


def swiglu_ffn(x, w_gate, w_up, w_down_parts, tile_expert, n_used, tm, tf):
    R, D = x.shape
    E, _, F = w_gate.shape
    n_parts = len(w_down_parts)
    epp = E // n_parts
    assert R % tm == 0 and F % tf == 0 and w_gate.dtype == BF16 and x.dtype == BF16 and E % n_parts == 0
    nf = F // tf

    def fblk(i, j, te, nu):
        return jnp.where(i < nu[0], j, nf - 1)

    def down_spec(p):
        def index(i, j, te, nu):
            mine = te[i] // epp == p
            return (jnp.where(mine, te[i] - p * epp, 0), jnp.where(mine, fblk(i, j, te, nu), 0), 0)
        return pl.BlockSpec((1, tf, D), index)

    in_specs = [pl.BlockSpec((tm, D), lambda i, j, te, nu: (i, 0)),
                pl.BlockSpec((1, D, tf), lambda i, j, te, nu: (te[i], 0, fblk(i, j, te, nu))),
                pl.BlockSpec((1, D, tf), lambda i, j, te, nu: (te[i], 0, fblk(i, j, te, nu))),
                *[down_spec(p) for p in range(n_parts)]]
    args = [x, w_gate, w_up, *w_down_parts]
    return pl.pallas_call(
        functools.partial(_ffn_kernel, experts_per_part=epp),
        grid_spec=pltpu.PrefetchScalarGridSpec(
            num_scalar_prefetch=2,
            grid=(R // tm, nf),
            in_specs=in_specs,
            out_specs=pl.BlockSpec((tm, D), lambda i, j, te, nu: (i, 0)),
        ),
        out_shape=jax.ShapeDtypeStruct((R, D), F32),
        compiler_params=_params("arbitrary", "arbitrary"),
        name="swiglu_ffn",
    )(tile_expert, n_used, *args)


def _rope(x, c, sg):
    lane = lax.broadcasted_iota(jnp.int32, x.shape, 1)
    first_half = jnp.bitwise_and(lane, HEAD_DIM // 2 - 1) < HEAD_DIM // 4
    partner = jnp.where(first_half, pltpu.roll(x, HEAD_DIM - HEAD_DIM // 4, 1), pltpu.roll(x, HEAD_DIM // 4, 1))
    return x * c + partner * sg


def _attn_kernel(sink_ref, q_ref, *rest, has_window, seq_len):
    if has_window:
        (kp_ref, ko_ref, kn_ref, vp_ref, vo_ref, vn_ref, cp_ref, co_ref, cn_ref, sp_ref, so_ref, sn_ref,
         kc_ref, vc_ref, o_ref) = rest
    else:
        kc_ref, vc_ref, o_ref = rest
    i = pl.program_id(0)
    BQ = q_ref.shape[0]
    L = kc_ref.shape[0]
    G = ATT_KV_HEADS
    R = ATT_HEADS // G
    n_win = 3 * BQ if has_window else 0
    nk = n_win + L
    if has_window:
        row = lax.broadcasted_iota(jnp.int32, (R * BQ, nk), 0)
        col = lax.broadcasted_iota(jnp.int32, (R * BQ, nk), 1)
        qi = jnp.bitwise_and(row, BQ - 1)
        kpos = i * BQ + col - BQ
        in_band = jnp.abs(col - BQ - qi) <= WINDOW
        in_seq = jnp.logical_and(kpos >= 0, kpos < seq_len)
        valid = jnp.logical_or(col >= n_win, jnp.logical_and(in_band, in_seq))
    for g in range(G):
        cs = slice(g * HEAD_DIM, (g + 1) * HEAD_DIM)
        q_heads = [q_ref[:, (g * R + r) * HEAD_DIM:(g * R + r + 1) * HEAD_DIM] for r in range(R)]
        if has_window:
            q_heads = [_rope(q, co_ref[...], so_ref[...]) for q in q_heads]
            k_all = jnp.concatenate([_rope(kp_ref[:, cs], cp_ref[...], sp_ref[...]),
                                     _rope(ko_ref[:, cs], co_ref[...], so_ref[...]),
                                     _rope(kn_ref[:, cs], cn_ref[...], sn_ref[...]), kc_ref[:, cs]], axis=0)
            v_all = jnp.concatenate([vp_ref[:, cs], vo_ref[:, cs], vn_ref[:, cs], vc_ref[:, cs]], axis=0)
        else:
            k_all, v_all = kc_ref[:, cs], vc_ref[:, cs]
        q_g = (jnp.concatenate(q_heads, axis=0) * (HEAD_DIM ** -0.5)).astype(BF16)
        s = lax.dot_general(q_g, k_all.astype(BF16), (((1,), (1,)), ((), ())),
                            preferred_element_type=F32)
        if has_window:
            s = jnp.where(valid, s, -jnp.inf)
        sink = jnp.concatenate([jnp.full((BQ, 1), sink_ref[g * R + r], F32) for r in range(R)], axis=0)
        m = jnp.maximum(sink, jnp.max(s, axis=-1, keepdims=True))
        p = jnp.exp(s - m)
        denom = jnp.exp(sink - m) + jnp.sum(p, axis=-1, keepdims=True)
        o = jnp.dot(p.astype(BF16), v_all.astype(BF16), preferred_element_type=F32) / denom
        for r in range(R):
            h = g * R + r
            o_ref[:, h * HEAD_DIM:(h + 1) * HEAD_DIM] = o[r * BQ:(r + 1) * BQ].astype(o_ref.dtype)


def sink_attention(proj, q_col, k_col, v_col, L, sink, rope):
    QW, KW = ATT_HEADS * HEAD_DIM, ATT_KV_HEADS * HEAD_DIM
    has_window = rope is not None
    S = proj.shape[0] - L if has_window else L
    BQ = ATT_BLOCK if has_window else L
    nb = S // BQ
    assert S % BQ == 0 and L % BQ == 0
    r0 = L // BQ if has_window else 0
    in_specs = [pl.BlockSpec(memory_space=pltpu.SMEM), pl.BlockSpec((BQ, QW), lambda i: (r0 + i, q_col))]
    args = [sink, proj]
    if has_window:
        prev = lambda i: jnp.maximum(i - 1, 0)
        nxt = lambda i: jnp.minimum(i + 1, nb - 1)
        for col in (k_col, v_col):
            in_specs += [pl.BlockSpec((BQ, KW), lambda i, col=col: (r0 + prev(i), col)),
                         pl.BlockSpec((BQ, KW), lambda i, col=col: (r0 + i, col)),
                         pl.BlockSpec((BQ, KW), lambda i, col=col: (r0 + nxt(i), col))]
            args += [proj] * 3
        for tab in rope:
            in_specs += [pl.BlockSpec((BQ, HEAD_DIM), lambda i: (prev(i), 0)),
                         pl.BlockSpec((BQ, HEAD_DIM), lambda i: (i, 0)),
                         pl.BlockSpec((BQ, HEAD_DIM), lambda i: (nxt(i), 0))]
            args += [tab] * 3
    in_specs += [pl.BlockSpec((L, KW), lambda i: (0, k_col)), pl.BlockSpec((L, KW), lambda i: (0, v_col))]
    args += [proj, proj]
    return pl.pallas_call(
        functools.partial(_attn_kernel, has_window=has_window, seq_len=S),
        grid=(nb,),
        in_specs=in_specs,
        out_specs=pl.BlockSpec((BQ, QW), lambda i: (i, 0)),
        out_shape=jax.ShapeDtypeStruct((S, QW), BF16),
        compiler_params=_params("arbitrary"),
        name="sink_attention",
    )(*args)


def _rope_tables(n_tokens):
    n_freq = HEAD_DIM // 4
    t = jnp.arange(n_tokens)
    row = (t // GRID_W).astype(F32)
    col = (t % GRID_W).astype(F32)
    inv = ROPE_THETA ** (-jnp.arange(n_freq, dtype=F32) / n_freq)
    ar, ac = row[:, None] * inv, col[:, None] * inv
    c = jnp.concatenate([jnp.cos(ar), jnp.cos(ar), jnp.cos(ac), jnp.cos(ac)], axis=1)
    sg = jnp.concatenate([-jnp.sin(ar), jnp.sin(ar), -jnp.sin(ac), jnp.sin(ac)], axis=1)
    return c, sg


CHUNK_SHIFT = CHUNK.bit_length() - 1
assert 1 << CHUNK_SHIFT == CHUNK


def _chunk_masks(tb, reverse, strict):
    r = lax.broadcasted_iota(jnp.int32, (tb, tb), 0)
    c = lax.broadcasted_iota(jnp.int32, (tb, tb), 1)
    same = jnp.right_shift(r, CHUNK_SHIFT) == jnp.right_shift(c, CHUNK_SHIFT)
    if reverse:
        tri = (c > r) if strict else (c >= r)
    else:
        tri = (c < r) if strict else (c <= r)
    return same, jnp.logical_and(same, tri)


def _chunk_sums(ld, same, tri):
    tb = ld.shape[0]
    sel = jnp.concatenate([jnp.where(tri, 1.0, 0.0), jnp.where(same, 1.0, 0.0)], axis=0).astype(BF16)
    hi = ld.astype(BF16)
    rest = ld - hi.astype(F32)
    mid = rest.astype(BF16)
    lo = (rest - mid.astype(F32)).astype(BF16)
    dot = lambda p: jnp.dot(sel, p, preferred_element_type=F32)
    g = (dot(lo) + dot(mid)) + dot(hi)
    return g[:tb], g[tb:]


def _dot_tn(a, b):
    return lax.dot_general(a, b, (((0,), (0,)), ((), ())), preferred_element_type=F32)


def _dot_nt(a, b):
    return lax.dot_general(a, b, (((1,), (1,)), ((), ())), preferred_element_type=F32)


def _chunkrec_core(load_head, n_heads, tb, V, reverse, st_ref, dst_ref):
    @pl.when(pl.program_id(0) == 0)
    def _():
        st_ref[...] = jnp.zeros_like(st_ref)

    same, tri = _chunk_masks(tb, reverse, strict=False)
    loaded = [load_head(h) for h in range(n_heads)]
    K = loaded[0][3].shape[1]
    g_cum_all, g_tot_all = _chunk_sums(jnp.concatenate([ld for _, _, _, ld in loaded], axis=1), same, tri)
    per_head = []
    for h in range(n_heads):
        q, k, v, _ = loaded[h]
        g_cum, g_tot = g_cum_all[:, h * K:(h + 1) * K], g_tot_all[:, h * K:(h + 1) * K]
        q_dec = (q * jnp.exp(g_cum)).astype(BF16)
        k_inv = (k * jnp.exp(-g_cum)).astype(BF16)
        k_tail = k * jnp.exp(g_tot - g_cum)
        a = jnp.where(tri, _dot_nt(q_dec, k_inv), 0.0)
        o_intra = jnp.dot(a.astype(BF16), v.astype(BF16), preferred_element_type=F32)
        per_head.append((q_dec, k_tail, v, g_tot, o_intra))
    n_chunks = tb // CHUNK
    order = range(n_chunks - 1, -1, -1) if reverse else range(n_chunks)
    sts = [st_ref[h] for h in range(n_heads)]
    for c in order:
        sl = slice(c * CHUNK, (c + 1) * CHUNK)
        for h in range(n_heads):
            q_dec, k_tail, v, g_tot, o_intra = per_head[h]
            st = sts[h]
            dst_ref[sl, h * V:(h + 1) * V] = o_intra[sl] + _dot_nt(q_dec[sl], st.astype(BF16))
            d = jnp.exp(g_tot[c * CHUNK:c * CHUNK + 1, :])
            sts[h] = st * d + _dot_tn(v[sl], k_tail[sl])
    for h in range(n_heads):
        st_ref[h] = sts[h]


def _gated_head_norm(o, gain, gate_raw):
    y = o * lax.rsqrt(jnp.mean(o * o, axis=-1, keepdims=True) + NORM_EPS) * gain
    return y * (gate_raw * jax.nn.sigmoid(gate_raw))


def _chunk_mixer_kernel(*refs, reverse, n_heads, K, V, load_heads):
    if reverse:
        *in_refs, o_ref, st_ref = refs
        dst_ref = o_ref
    else:
        *in_refs, gate_ref, orev_ref, gain_ref, o_ref, st_ref, dst_ref = refs
    tb = o_ref.shape[0]
    _chunkrec_core(load_heads(*in_refs), n_heads, tb, V, reverse, st_ref, dst_ref)
    if not reverse:
        for h in range(n_heads):
            vs = slice(h * V, (h + 1) * V)
            o_ref[:, vs] = _gated_head_norm(dst_ref[:, vs] + orev_ref[:, vs], gain_ref[...],
                                            gate_ref[:, vs]).astype(o_ref.dtype)


def _hgrn_heads(qh_ref, ih_ref, fr_ref, lb_ref):
    def load(h):
        cs = slice(h * 128, (h + 1) * 128)
        lb = lb_ref[:, cs]
        f = lb + (1.0 - lb) * jax.nn.sigmoid(fr_ref[:, cs])
        qh = qh_ref[:, cs]
        return qh * jax.nn.sigmoid(qh), 1.0 - f, ih_ref[:, cs], jnp.log(f)
    return load


def _gla_heads(gq_ref, gk_ref, gv_ref, gd_ref, up_ref, bias_ref):
    x = jnp.dot(gd_ref[...].astype(BF16), up_ref[...].astype(BF16), preferred_element_type=F32) + bias_ref[...]
    lg = (jnp.minimum(x, 0.0) - jnp.log(1.0 + jnp.exp(-jnp.abs(x)))) * (1.0 / GLA_GATE_NORMALIZER)

    def load(h):
        ks, vs = slice(h * 128, (h + 1) * 128), slice(h * 256, (h + 1) * 256)
        return gq_ref[:, ks] * (128 ** -0.5), gk_ref[:, ks], gv_ref[:, vs], lg[:, ks]
    return load


def _seq_block_index(nblk, nctx, reverse):
    if not reverse:
        return lambda t: t
    return lambda t: jnp.where(t < nctx, nctx - 1 - t, nblk - 1 - (t - nctx))


def chunk_mixer(load_heads, inputs, n_heads, K, V, n_ctx_rows, reverse, final=None):
    T = inputs[0][0].shape[0]
    tb = TOK_BLOCK
    assert T % tb == 0 and n_ctx_rows % tb == 0
    nblk, nctx = T // tb, n_ctx_rows // tb
    blk = _seq_block_index(nblk, nctx, reverse)

    def spec(item):
        if len(item) == 1:
            return pl.BlockSpec(item[0].shape, lambda t: (0, 0))
        _, width, cb = item
        return pl.BlockSpec((tb, width), lambda t: (blk(t), cb))

    HV = n_heads * V
    ospec = pl.BlockSpec((tb, HV), lambda t: (blk(t), 0))
    scratch = [pltpu.VMEM((n_heads, V, K), F32)]
    if not reverse:
        gate, o_rev, gain = final
        inputs = list(inputs) + [gate, (o_rev, HV, 0), (gain.reshape(1, V),)]
        scratch.append(pltpu.VMEM((tb, HV), F32))
    return pl.pallas_call(
        functools.partial(_chunk_mixer_kernel, reverse=reverse, n_heads=n_heads, K=K, V=V, load_heads=load_heads),
        grid=(nblk,),
        in_specs=[spec(it) for it in inputs],
        out_specs=ospec,
        out_shape=jax.ShapeDtypeStruct((T, HV), F32 if reverse else BF16),
        scratch_shapes=scratch,
        compiler_params=_params("arbitrary"),
        name="chunk_mixer",
    )(*[it[0] for it in inputs])


def _mm_bf(a, b):
    return jnp.dot(a.astype(BF16), b.astype(BF16), preferred_element_type=F32)


def _softplus(x):
    return jnp.maximum(x, 0.0) + jnp.log(1.0 + jnp.exp(-jnp.abs(x)))


def _head_sums(x, bd_ref):
    hi = x.astype(BF16)
    lo = (x - hi.astype(F32)).astype(BF16)
    bd = bd_ref[...]
    W = bd.shape[0]
    groups = []
    for g in range(x.shape[1] // W):
        gs = slice(g * W, (g + 1) * W)
        groups.append(jnp.dot(lo[:, gs], bd, preferred_element_type=F32)
                      + jnp.dot(hi[:, gs], bd, preferred_element_type=F32))
    return jnp.concatenate(groups, axis=1)


def _rwkv_prep_kernel(p_ref, pp_ref, pn_ref, g_ref, gp_ref, gn_ref, s_ref, sp_ref, sn_ref,
                      mup_ref, mun_ref, mugp_ref, mugn_ref, musp_ref, musn_ref,
                      w2f_ref, w2b_ref, a2_ref, g2_ref, w0f_ref, w0b_ref, a0_ref, kk_ref, ka_ref, rk_ref, bd_ref,
                      r_out, k_out, v_out, kk_out, b_out, lwf_out, lwb_out, go_out, bon_out,
                      gcf_out, gtf_out, gcb_out, gtb_out, *, n_ctx_blocks):
    t, nblk = pl.program_id(0), pl.num_programs(0)
    tb = p_ref.shape[0]
    C = RWKV_HEADS * RWKV_N
    keep_prev = jnp.where(jnp.logical_or(t == 0, t == n_ctx_blocks), 0.0, 1.0)
    keep_next = jnp.where(jnp.logical_or(t == n_ctx_blocks - 1, t == nblk - 1), 0.0, 1.0)

    def shifted(x_ref, xp_ref, xn_ref, mu_p_ref, mu_n_ref):
        x = x_ref[...]
        rows = lax.broadcasted_iota(jnp.int32, x.shape, 0)
        prev = jnp.where(rows == 0, xp_ref[7:8, :] * keep_prev, pltpu.roll(x, 1, 0))
        nxt = jnp.where(rows == tb - 1, xn_ref[0:1, :] * keep_next, pltpu.roll(x, tb - 1, 0))
        return x + mu_p_ref[...] * (prev - x) + mu_n_ref[...] * (nxt - x)

    rkv = shifted(p_ref, pp_ref, pn_ref, mup_ref, mun_ref)
    rr, rk, rv = rkv[:, :C], rkv[:, C:2 * C], rkv[:, 2 * C:]
    low = shifted(s_ref, sp_ref, sn_ref, musp_ref, musn_ref)
    gd = shifted(g_ref, gp_ref, gn_ref, mugp_ref, mugn_ref)
    tl = jnp.tanh(low).astype(BF16)
    for w0_ref, w2_ref, lw_out, gc_out, gt_out, reverse in ((w0f_ref, w2f_ref, lwf_out, gcf_out, gtf_out, False),
                                                            (w0b_ref, w2b_ref, lwb_out, gcb_out, gtb_out, True)):
        lw = -jnp.exp(-_softplus(-(w0_ref[...] + jnp.dot(tl, w2_ref[...], preferred_element_type=F32))) - 0.5)
        lw_out[...] = lw
        same, tri = _chunk_masks(tb, reverse, strict=False)
        gc_out[...], gt_out[...] = _chunk_sums(lw, same, tri)
    a_sig = jax.nn.sigmoid(a0_ref[...] + jnp.dot(low.astype(BF16), a2_ref[...], preferred_element_type=F32))
    go_out[...] = jnp.dot(jax.nn.sigmoid(gd).astype(BF16), g2_ref[...], preferred_element_type=F32)
    kk = rk * kk_ref[...]
    kk = kk * lax.rsqrt(_head_sums(kk * kk, bd_ref) + 1e-12)
    k_mod = rk * (1.0 + (a_sig - 1.0) * ka_ref[...])
    r_out[...] = rr
    k_out[...] = k_mod
    v_out[...] = rv
    kk_out[...] = kk
    b_out[...] = kk * a_sig
    bon_out[...] = _head_sums(rr * k_mod * rk_ref[...], bd_ref) * rv


def _rwkv_pre_kernel(r_ref, k_ref, v_ref, kk_ref, b_ref, lw_ref, gc_ref, gt_ref, *rest, reverse):
    n_cast = (len(rest) - 4) // 2
    w32_refs, (qp_ref, ol_ref, plt_ref, zt_ref), w16_refs = rest[:n_cast], rest[n_cast:n_cast + 4], rest[n_cast + 4:]
    for w32_ref, w16_ref in zip(w32_refs, w16_refs):
        w16_ref[...] = w32_ref[...].astype(BF16)
    N = RWKV_N
    tb = r_ref.shape[0]
    hb = r_ref.shape[1] // N
    _, tri_incl = _chunk_masks(tb, reverse, strict=False)
    _, tri_strict = _chunk_masks(tb, reverse, strict=True)
    xs, lps, rest = [], [], []
    for h in range(hb):
        hs = slice(h * N, (h + 1) * N)
        r, k, v, b, lw, g_cum, g_tot = (ref[:, hs] for ref in (r_ref, k_ref, v_ref, b_ref, lw_ref, gc_ref, gt_ref))
        a = -kk_ref[:, hs]
        e_neg = jnp.exp(-g_cum)
        e_tail = jnp.exp(g_tot - g_cum)
        a_t = (a * jnp.exp(g_cum - lw)).astype(BF16)
        r_t = (r * jnp.exp(g_cum)).astype(BF16)
        b_t = (b * e_neg).astype(BF16)
        k_t = (k * e_neg).astype(BF16)
        a_ab = jnp.where(tri_strict, _dot_nt(a_t, b_t), 0.0)
        a_ak = jnp.where(tri_strict, _dot_nt(a_t, k_t), 0.0)
        a_rb = jnp.where(tri_incl, _dot_nt(r_t, b_t), 0.0)
        a_rk = jnp.where(tri_incl, _dot_nt(r_t, k_t), 0.0)
        xs.append(jnp.concatenate([a_t.astype(F32), _mm_bf(a_ak, v)], axis=1))
        lps.append(a_ab)
        rest.append((v, b * e_tail, k * e_tail, a_rb,
                     jnp.concatenate([r_t.astype(F32), _mm_bf(a_rk, v)], axis=1)))
    for j in range(CHUNK_SHIFT):
        xs = [x + _mm_bf(lp, x) for x, lp in zip(xs, lps)]
        if j < CHUNK_SHIFT - 1:
            lps = [_mm_bf(lp, lp) for lp in lps]
    for h in range(hb):
        v, b_h, k_h, a_rb, qo0 = rest[h]
        x = xs[h]
        qo = qo0 + _mm_bf(a_rb, x)
        qp_ref[:, h * N:(h + 1) * N] = qo[:, :N]
        ol_ref[:, h * N:(h + 1) * N] = qo[:, N:]
        w, uloc = x[:, :N], x[:, N:]
        for c in range(tb // CHUNK):
            sl = slice(c * CHUNK, (c + 1) * CHUNK)
            plt_ref[h, c] = _dot_tn(w[sl], b_h[sl])
            zt_ref[h, c] = _dot_tn(uloc[sl], b_h[sl]) + _dot_tn(v[sl], k_h[sl])


def _rwkv_scan_kernel(qp_ref, ol_ref, plt_ref, zt_ref, gt_ref, *rest, reverse):
    if reverse:
        o_ref, st_ref = rest
        dst_ref = o_ref
    else:
        orev_ref, go_ref, bon_ref, lnw_ref, lnb_ref, bd_ref, o_ref, st_ref, dst_ref = rest

    @pl.when(pl.program_id(0) == 0)
    def _():
        st_ref[...] = jnp.zeros_like(st_ref)

    N = RWKV_N
    tb = qp_ref.shape[0]
    n_heads = qp_ref.shape[1] // N
    n_chunks = tb // CHUNK
    order = range(n_chunks - 1, -1, -1) if reverse else range(n_chunks)
    sts = [st_ref[h] for h in range(n_heads)]
    for c in order:
        sl = slice(c * CHUNK, (c + 1) * CHUNK)
        for h in range(n_heads):
            hs = slice(h * N, (h + 1) * N)
            st = sts[h]
            dst_ref[sl, hs] = _dot_nt(qp_ref[sl, hs].astype(BF16), st.astype(BF16)) + ol_ref[sl, hs]
            d = jnp.exp(gt_ref[c * CHUNK:c * CHUNK + 1, hs])
            sts[h] = st * d + _mm_bf(st, plt_ref[h, c]) + zt_ref[h, c]
    for h in range(n_heads):
        st_ref[h] = sts[h]
    if not reverse:
        o = dst_ref[...] + orev_ref[...]
        dev = o - _head_sums(o, bd_ref) * (1.0 / N)
        var = _head_sums(dev * dev, bd_ref) * (1.0 / N)
        on = dev * lax.rsqrt(var + RWKV_LN_EPS) * lnw_ref[...] + lnb_ref[...]
        o_ref[...] = ((on + bon_ref[...]) * go_ref[...]).astype(o_ref.dtype)


def rwkv7_branch(pb, ps, rkv_col, gd_col, n_ctx_rows, mu_prev, mu_next, low_cols, w0_f, w0_b, w2_f, w2_b,
                 a0, a2, g2, k_k, k_a, r_k, ln_w, ln_b, to_bf16):
    T = pb.shape[0]
    H, N = RWKV_HEADS, RWKV_N
    C = H * N
    tb = TOK_BLOCK
    assert T % tb == 0 and n_ctx_rows % tb == 0
    nblk, nctx = T // tb, n_ctx_rows // tb
    ncb = tb // CHUNK
    GW, SW, R = g2.shape[0], ps.shape[1], w2_f.shape[0]
    wf_col, wb_col, ad_col = low_cols
    s0 = 3 * C
    row = lambda v: v.reshape(1, -1)

    def padded_rows(w, r0):
        return jnp.zeros((SW, C), F32).at[r0:r0 + w.shape[0]].set(w).astype(BF16)

    def mu_low(mu):
        return jnp.zeros((1, SW), F32).at[0, wf_col:wf_col + 3 * R].set(mu[s0:s0 + 3 * R])

    bdw = 2 * LANES
    bd = (jnp.arange(bdw)[:, None] // N == jnp.arange(bdw)[None, :] // N).astype(BF16)
    r8 = tb // 8

    def halo(width, col):
        return (pl.BlockSpec((tb, width), lambda t: (t, col)),
                pl.BlockSpec((8, width), lambda t: (jnp.maximum(t * r8 - 1, 0), col)),
                pl.BlockSpec((8, width), lambda t: (jnp.minimum((t + 1) * r8, T // 8 - 1), col)))

    whole = lambda a: pl.BlockSpec(a.shape, lambda t: (0, 0))
    consts = [row(mu_prev[:s0]), row(mu_next[:s0]), row(mu_prev[s0 + 3 * R:]), row(mu_next[s0 + 3 * R:]),
              mu_low(mu_prev), mu_low(mu_next),
              padded_rows(w2_f, wf_col), padded_rows(w2_b, wb_col), padded_rows(a2, ad_col), g2.astype(BF16),
              row(w0_f), row(w0_b), row(a0), row(k_k), row(k_a), row(r_k), bd]
    tokC = pl.BlockSpec((tb, C), lambda t: (t, 0))
    shapeC = jax.ShapeDtypeStruct((T, C), F32)
    r, k, v, kk, b, lw_f, lw_b, g_out, bonus, gc_f, gt_f, gc_b, gt_b = pl.pallas_call(
        functools.partial(_rwkv_prep_kernel, n_ctx_blocks=nctx),
        grid=(nblk,),
        in_specs=[*halo(3 * C, rkv_col), *halo(GW, gd_col), *halo(SW, 0), *[whole(a) for a in consts]],
        out_specs=[tokC] * 13,
        out_shape=[shapeC] * 13,
        compiler_params=_params("arbitrary"),
        name="rwkv_prep",
    )(pb, pb, pb, pb, pb, pb, ps, ps, ps, *consts)

    hb = RWKV_HEAD_BLOCK
    tok = pl.BlockSpec((tb, hb * N), lambda h, t: (t, h))
    mat = pl.BlockSpec((hb, ncb, N, N), lambda h, t: (h, t, 0, 0))
    mat_shape = jax.ShapeDtypeStruct((H, T // CHUNK, N, N), F32)
    o_rev = None
    cast_done = []
    def cast_specs(job):
        w32, row0, wr = job
        rows = BF16_SUBLANES * pl.cdiv(wr, BF16_SUBLANES * (H // hb) * nblk)
        assert wr % rows == 0 and row0 % rows == 0
        block = lambda h, t: jnp.minimum(h * nblk + t, wr // rows - 1)
        return (pl.BlockSpec((rows, w32.shape[1]), lambda h, t: (row0 // rows + block(h, t), 0)),
                pl.BlockSpec((rows, w32.shape[1]), lambda h, t: (block(h, t), 0)),
                jax.ShapeDtypeStruct((wr, w32.shape[1]), BF16))

    for lw, gc, gt, jobs, reverse in ((lw_b, gc_b, gt_b, to_bf16[0], True), (lw_f, gc_f, gt_f, to_bf16[1], False)):
        cin, cout, cshape = zip(*[cast_specs(job) for job in jobs])
        qp, ol, plt, zt, *w16s = pl.pallas_call(
            functools.partial(_rwkv_pre_kernel, reverse=reverse),
            grid=(H // hb, nblk),
            in_specs=[tok] * 8 + list(cin),
            out_specs=[tok, tok, mat, mat] + list(cout),
            out_shape=[shapeC, shapeC, mat_shape, mat_shape] + list(cshape),
            compiler_params=_params("arbitrary", "arbitrary"),
            name="rwkv_pre",
        )(r, k, v, kk, b, lw, gc, gt, *[job[0] for job in jobs])
        cast_done.append(w16s)
        blk = _seq_block_index(nblk, nctx, reverse)
        tok_all = pl.BlockSpec((tb, C), lambda t: (blk(t), 0))
        mat_all = pl.BlockSpec((H, ncb, N, N), lambda t: (0, blk(t), 0, 0))
        in_specs = [tok_all, tok_all, mat_all, mat_all, tok_all]
        args = [qp, ol, plt, zt, gt]
        scratch = [pltpu.VMEM((H, N, N), F32)]
        if not reverse:
            in_specs += [tok_all, tok_all, tok_all, whole(row(ln_w)), whole(row(ln_b)), whole(bd)]
            args += [o_rev, g_out, bonus, row(ln_w), row(ln_b), bd]
            scratch.append(pltpu.VMEM((tb, C), F32))
        out = pl.pallas_call(
            functools.partial(_rwkv_scan_kernel, reverse=reverse),
            grid=(nblk,),
            in_specs=in_specs,
            out_specs=tok_all,
            out_shape=jax.ShapeDtypeStruct((T, C), F32 if reverse else BF16),
            scratch_shapes=scratch,
            compiler_params=_params("arbitrary"),
            name="rwkv_scan",
        )(*args)
        if reverse:
            o_rev = out
    return out, cast_done


def _row_copy(src_hbm, src_row, dst_ref, dst_row, sem):
    return pltpu.make_async_copy(src_hbm.at[pl.ds(src_row, 1)], dst_ref.at[pl.ds(dst_row, 1)], sem)


def _gather_kernel(src_ref, nxt_ref, x_hbm, o_ref, buf, sems):
    i, n_steps = pl.program_id(0), pl.num_programs(0)
    n = o_ref.shape[0]

    def issue(idx_ref, slot):
        def body(r, carry):
            for u in range(2):
                row = 2 * r + u
                _row_copy(x_hbm, idx_ref[0, 0, row], buf.at[slot], row, sems.at[slot]).start(priority=u)
            return carry
        lax.fori_loop(0, n // 2, body, 0, unroll=ROW_COPY_UNROLL)

    @pl.when(i == 0)
    def _():
        issue(src_ref, 0)

    @pl.when(i + 1 < n_steps)
    def _():
        issue(nxt_ref, (i + 1) % 2)

    slot = i % 2

    pltpu.make_async_copy(x_hbm.at[pl.ds(0, n)], buf.at[slot], sems.at[slot]).wait()
    o_ref[...] = buf[slot].astype(o_ref.dtype)


def gather_rows(x, src, tg):
    S, D = x.shape
    P = src.shape[0]
    sub = D // LANES
    assert P % tg == 0 and x.dtype == BF16 and sub == BF16_SUBLANES
    n_steps = P // tg
    src3 = src.reshape(n_steps, 1, tg)
    out = pl.pallas_call(
        _gather_kernel,
        grid=(n_steps,),
        in_specs=[pl.BlockSpec((1, 1, tg), lambda i: (i, 0, 0), memory_space=pltpu.SMEM),
                  pl.BlockSpec((1, 1, tg), lambda i: (jnp.minimum(i + 1, n_steps - 1), 0, 0),
                               memory_space=pltpu.SMEM),
                  pl.BlockSpec(memory_space=pl.ANY)],
        out_specs=pl.BlockSpec((tg, sub, LANES), lambda i: (i, 0, 0)),
        out_shape=jax.ShapeDtypeStruct((P, sub, LANES), BF16),
        scratch_shapes=[pltpu.VMEM((2, tg, sub, LANES), BF16), pltpu.SemaphoreType.DMA((2,))],
        compiler_params=_params("arbitrary"),
        name="gather_rows",
    )(src3, src3, x.reshape(S, sub, LANES))
    return out.reshape(P, D)


def _combine_kernel(pos_ref, nxt_ref, x_ref, w_ref, ys_hbm, gpost_ref, g_ref, o_ref, buf, sems):
    i, n_steps = pl.program_id(0), pl.num_programs(0)
    n = x_ref.shape[0]

    def issue(idx_ref, slot):
        def body(r, carry):
            for k in range(2):
                _row_copy(ys_hbm, idx_ref[0, 0, 2 * r + k], buf.at[slot, k], r, sems.at[slot, k]).start(priority=k)
            return carry
        lax.fori_loop(0, n, body, 0, unroll=ROW_COPY_UNROLL)

    @pl.when(i == 0)
    def _():
        issue(pos_ref, 0)

    @pl.when(i + 1 < n_steps)
    def _():
        issue(nxt_ref, (i + 1) % 2)

    slot = i % 2

    for k in range(2):
        pltpu.make_async_copy(ys_hbm.at[pl.ds(0, n)], buf.at[slot, k], sems.at[slot, k]).wait()
    w = w_ref[...]
    f = w[:, 0:1] * buf[slot, 0] + w[:, 1:2] * buf[slot, 1]
    o_ref[...] = x_ref[...] + g_ref[0] * _rms(f, gpost_ref[...])


def combine_resid(x, ys, pos, weights, gain_post, mods, g_slot):
    S, D = x.shape
    tc = TOK_BLOCK
    assert S % tc == 0
    n_steps = S // tc
    row = pl.BlockSpec((tc, D), lambda i: (i, 0))
    pos3 = pos.reshape(n_steps, 1, 2 * tc)
    return pl.pallas_call(
        _combine_kernel,
        grid=(n_steps,),
        in_specs=[pl.BlockSpec((1, 1, 2 * tc), lambda i: (i, 0, 0), memory_space=pltpu.SMEM),
                  pl.BlockSpec((1, 1, 2 * tc), lambda i: (jnp.minimum(i + 1, n_steps - 1), 0, 0),
                               memory_space=pltpu.SMEM),
                  row, pl.BlockSpec((tc, 2), lambda i: (i, 0)), pl.BlockSpec(memory_space=pl.ANY),
                  pl.BlockSpec((1, D), lambda i: (0, 0)), _mod_spec(D, g_slot, 0)],
        out_specs=row,
        out_shape=jax.ShapeDtypeStruct((S, D), F32),
        scratch_shapes=[pltpu.VMEM((2, 2, tc, D), F32), pltpu.SemaphoreType.DMA((2, 2))],
        compiler_params=_params("arbitrary"),
        name="combine_resid",
    )(pos3, pos3, x, weights, ys, gain_post.reshape(1, D), mods)


def _route(logits, tm):
    S = logits.shape[0]
    top_val, top_idx = lax.top_k(logits, 2)
    weights = jax.nn.softmax(top_val, axis=-1)
    e_flat = top_idx.reshape(-1)
    onehot = (e_flat[:, None] == jnp.arange(N_EXPERTS)[None, :]).astype(jnp.int32)
    rank = jnp.take_along_axis(jnp.cumsum(onehot, axis=0), e_flat[:, None], axis=1)[:, 0] - 1
    counts = jnp.sum(onehot, axis=0)
    tiles_per = (counts + tm - 1) // tm
    tile_end = jnp.cumsum(tiles_per)
    start = (tile_end - tiles_per) * tm
    dest = start[e_flat] + rank
    n_rows = 2 * S + N_EXPERTS * tm
    n_tiles = n_rows // tm
    token = jnp.arange(2 * S, dtype=jnp.int32) // 2
    src = jnp.zeros((n_rows,), jnp.int32).at[dest].set(token)
    n_used = tile_end[-1].astype(jnp.int32)
    tile_id = jnp.minimum(jnp.arange(n_tiles, dtype=jnp.int32), n_used - 1)
    tile_expert = jnp.sum((tile_end[None, :] <= tile_id[:, None]).astype(jnp.int32), axis=1)
    tile_expert = jnp.minimum(tile_expert, N_EXPERTS - 1)
    return src, weights, dest.astype(jnp.int32), tile_expert, n_used.reshape(1)


def _adaln(c, c_ctx, w, b):
    D = c.shape[-1]
    rows = jnp.zeros((8, D), F32).at[0].set(c[0]).at[1].set(c_ctx)
    m = matmul(jax.nn.silu(rows), w, 8, 2048)[:2] + b[None, :]
    return m.reshape(12, 1, D)


def _even_mixer(h, L, w_in, w_out, attn_sink, hgrn_norm, hgrn_lb):
    T = h.shape[0]
    S = T - L
    proj = matmul(h, w_in, STREAM_ROWS, 512, col_roll=1536 // 512)
    att = sink_attention(proj, 5, 24, 25, L, attn_sink, _rope_tables(S))
    att_c = sink_attention(proj, 5, 24, 25, L, attn_sink, None)
    lb = (hgrn_lb.reshape(1, -1),)
    qh, ih = (proj, 1024, 0), (proj, 1024, 1)
    o_rev = chunk_mixer(_hgrn_heads, [qh, ih, (proj, 1024, 3), lb], 8, 128, 128, L, True)
    hg = chunk_mixer(_hgrn_heads, [qh, ih, (proj, 1024, 2), lb], 8, 128, 128, L, False,
                     final=((proj, 1024, 4), o_rev, hgrn_norm))
    ycat = jnp.concatenate([jnp.concatenate([att_c, att], axis=0), hg], axis=-1)
    return matmul(ycat, w_out, STREAM_ROWS, 1024)


def _odd_mixer(h, L, w_in, w_out, gla_gate_up_f, gla_gate_up_b, gla_gate_bias_f, gla_gate_bias_b, gla_norm,
               mu_prev, mu_next, w0_f, w0_b, w2_f, w2_b, a0, a2, g2, k_k, k_a, r_k, ln_w, ln_b, to_bf16):
    T = h.shape[0]
    GO = 3104
    cols = lambda a, b: w_in[:, a:b]
    w_big = jnp.concatenate([cols(0, 2048), cols(2080, 3104), cols(GO, GO + 3072), cols(GO + 3360, GO + 3616)], axis=1)
    w_small = jnp.concatenate([cols(2048, 2080), cols(GO + 3072, GO + 3360),
                               jnp.zeros((w_in.shape[0], 64), F32)], axis=1)
    pb = matmul(h, w_big, STREAM_ROWS, 640)
    ps = matmul(h, w_small, STREAM_ROWS, 384)
    def gla_inputs(up, row0, bias):
        up_rows = jnp.zeros((ps.shape[1], up.shape[1]), F32).at[row0:row0 + up.shape[0]].set(up)
        return [(pb, 512, 0), (pb, 512, 1), (pb, 1024, 1), (ps, ps.shape[1], 0), (up_rows,), (bias.reshape(1, -1),)]
    o_rev = chunk_mixer(_gla_heads, gla_inputs(gla_gate_up_b, 16, gla_gate_bias_b), 4, 128, 256, L, True)
    gla = chunk_mixer(_gla_heads, gla_inputs(gla_gate_up_f, 0, gla_gate_bias_f), 4, 128, 256, L, False,
                      final=((pb, 1024, 2), o_rev, gla_norm))
    rw, cast_done = rwkv7_branch(pb, ps, 1, 24, L, mu_prev, mu_next, (32, 128, 224), w0_f, w0_b, w2_f, w2_b,
                                 a0, a2, g2, k_k, k_a, r_k, ln_w, ln_b, to_bf16)
    ycat = jnp.concatenate([gla[L:], rw[L:]], axis=-1)
    return matmul(ycat, w_out, LATENT_ROWS, 1024), cast_done


def kernel(x, c, ctx, c_ctx, hgrn_lb_logits, l0_ada_w, l0_ada_b, l0_norm_mix_pre, l0_norm_mix_post, l0_norm_ffn_pre, l0_norm_ffn_post, l0_w_in, l0_w_out, l0_attn_sink, l0_hgrn_norm, l0_ffn_w_gate, l0_ffn_w_up, l0_ffn_w_down, l1_ada_w, l1_ada_b, l1_norm_mix_pre, l1_norm_mix_post, l1_norm_ffn_pre, l1_norm_ffn_post, l1_w_in, l1_w_out, l1_gla_gate_up_f, l1_gla_gate_up_b, l1_gla_gate_bias_f, l1_gla_gate_bias_b, l1_gla_norm, l1_rwkv_mu_prev, l1_rwkv_mu_next, l1_rwkv_w0_f, l1_rwkv_w0_b, l1_rwkv_w2_f, l1_rwkv_w2_b, l1_rwkv_a0, l1_rwkv_a2, l1_rwkv_g2, l1_rwkv_k_k, l1_rwkv_k_a, l1_rwkv_r_k, l1_rwkv_ln_w, l1_rwkv_ln_b, l1_moe_router, l1_moe_w_gate, l1_moe_w_up, l1_moe_w_down):
    B, S, D = x.shape
    L = ctx.shape[1]
    assert B == 1
    T = L + S
    SH1, SC1, G1, SH2, SC2, G2 = range(6)
    xa = jnp.concatenate([ctx[0], x[0]], axis=0)
    hgrn_lb = jnp.cumsum(jax.nn.softmax(hgrn_lb_logits.astype(F32), axis=0), axis=0)
    m0 = _adaln(c, c_ctx, l0_ada_w, l0_ada_b)
    m1 = _adaln(c, c_ctx, l1_ada_w, l1_ada_b)

    h = normmod(xa, l0_norm_mix_pre, m0, SH1, SC1, L)
    y = _even_mixer(h, L, l0_w_in, l0_w_out, l0_attn_sink, l0_hgrn_norm, hgrn_lb[0])
    xa, h = resid_norm(xa, y, l0_norm_mix_post, m0, G1, L, nxt=(l0_norm_ffn_pre, m0, SH2, SC2))
    n_t = T // DENSE_TILE
    f = swiglu_ffn(h, l0_ffn_w_gate[None].astype(BF16), l0_ffn_w_up[None].astype(BF16),
                   [l0_ffn_w_down[None].astype(BF16)],
                   jnp.zeros((n_t,), jnp.int32), jnp.full((1,), n_t, jnp.int32), DENSE_TILE, DENSE_FF_BLOCK)
    xa, h = resid_norm(xa, f, l0_norm_ffn_post, m0, G2, L, nxt=(l1_norm_mix_pre, m1, SH1, SC1))

    E, _, FE = l1_moe_w_gate.shape
    wg2, wu2, wd2 = (l1_moe_w_gate.reshape(E * D, FE), l1_moe_w_up.reshape(E * D, FE),
                     l1_moe_w_down.reshape(E * FE, D))
    half = E * FE // 2
    y, ((wg16, wd16_lo), (wu16, wd16_hi)) = _odd_mixer(
        h, L, l1_w_in, l1_w_out, l1_gla_gate_up_f, l1_gla_gate_up_b, l1_gla_gate_bias_f,
        l1_gla_gate_bias_b, l1_gla_norm, l1_rwkv_mu_prev, l1_rwkv_mu_next, l1_rwkv_w0_f, l1_rwkv_w0_b,
        l1_rwkv_w2_f, l1_rwkv_w2_b, l1_rwkv_a0, l1_rwkv_a2, l1_rwkv_g2, l1_rwkv_k_k, l1_rwkv_k_a,
        l1_rwkv_r_k, l1_rwkv_ln_w, l1_rwkv_ln_b,
        to_bf16=([(wg2, 0, E * D), (wd2, 0, half)], [(wu2, 0, E * D), (wd2, half, half)]))
    xl, h = resid_norm(xa, y, l1_norm_mix_post, m1, G1, 0, nxt=(l1_norm_ffn_pre, m1, SH2, SC2), x_row0=L)
    router = jnp.concatenate([l1_moe_router, jnp.zeros((D, LANES - N_EXPERTS), F32)], axis=1)
    logits = matmul(h, router, LATENT_ROWS, LANES)[:, :N_EXPERTS]
    src, gate_w, dest, tile_expert, n_used = _route(logits, MOE_TILE)
    hs = gather_rows(h, src, TOK_BLOCK)
    ys = swiglu_ffn(hs, wg16.reshape(E, D, FE), wu16.reshape(E, D, FE),
                    [wd16_lo.reshape(E // 2, FE, D), wd16_hi.reshape(E // 2, FE, D)],
                    tile_expert, n_used, MOE_TILE, MOE_FF_BLOCK)
    out = combine_resid(xl, ys, dest, gate_w, l1_norm_ffn_post, m1, G2)
    return out[None]
```

```python
import functools

import jax
import jax.numpy as jnp
from jax import lax
from jax.experimental import pallas as pl
from jax.experimental.pallas import tpu as pltpu

F32 = jnp.float32
BF16 = jnp.bfloat16

NORM_EPS = 1e-6
CHUNK = 32
TOK_BLOCK = 256
ATT_BLOCK = 128
WINDOW = 128
HEAD_DIM = 128
ATT_HEADS = 8
ATT_KV_HEADS = 2
ROPE_THETA = 10000.0
GRID_W = 64
RWKV_N = 64
RWKV_HEADS = 16
RWKV_LN_EPS = 64e-5
RWKV_HEAD_BLOCK = 8
GLA_GATE_NORMALIZER = 16.0
N_EXPERTS = 8
MOE_TILE = 512
DENSE_TILE = 416
DENSE_FF_BLOCK = 1408
MOE_FF_BLOCK = 1024
STREAM_ROWS = 1280
LATENT_ROWS = 1024
VMEM_LIMIT_BYTES = 56 * 1024 * 1024
LANES = 128
BF16_SUBLANES = 16
ROW_COPY_UNROLL = 8


def _params(*sem):
    return pltpu.CompilerParams(dimension_semantics=sem, vmem_limit_bytes=VMEM_LIMIT_BYTES)


def _mm_kernel(*refs):
    *x_refs, w_ref, o_ref, wbf_ref = refs

    @pl.when(pl.program_id(1) == 0)
    def _():
        wbf_ref[...] = w_ref[...].astype(BF16)

    acc, k0 = None, 0
    for x_ref in x_refs:
        k1 = k0 + x_ref.shape[1]
        part = jnp.dot(x_ref[...].astype(BF16), wbf_ref[k0:k1, :], preferred_element_type=F32)
        acc = part if acc is None else acc + part
        k0 = k1
    o_ref[...] = acc.astype(o_ref.dtype)


def matmul(x, w, tm, tn, out_dtype=F32, col_roll=0):
    xs = x if isinstance(x, tuple) else (x,)
    M = xs[0].shape[0]
    K, N = w.shape
    assert M % tm == 0 and N % tn == 0 and sum(a.shape[1] for a in xs) == K, (M, tm, N, tn)
    nj = N // tn
    return pl.pallas_call(
        _mm_kernel,
        grid=(nj, M // tm),
        in_specs=[*[pl.BlockSpec((tm, a.shape[1]), lambda j, i: (i, 0)) for a in xs],
                  pl.BlockSpec((K, tn), lambda j, i: (0, jnp.where(j + col_roll >= nj, j + col_roll - nj,
                                                                    j + col_roll)))],
        out_specs=pl.BlockSpec((tm, tn), lambda j, i: (i, j)),
        out_shape=jax.ShapeDtypeStruct((M, N), out_dtype),
        scratch_shapes=[pltpu.VMEM((K, tn), BF16)],
        compiler_params=_params("arbitrary", "arbitrary"),
        name="matmul",
    )(*xs, w)


def _rms(x, gain):
    ms = jnp.mean(x * x, axis=-1, keepdims=True)
    return x * lax.rsqrt(ms + NORM_EPS) * gain


def _stream_specs(x, tr, n_ctx_tiles, row0_tiles=0):
    if not isinstance(x, tuple):
        return [x], [pl.BlockSpec((tr, x.shape[1]), lambda i: (i + row0_tiles, 0))]
    top, rest = x
    D = top.shape[1]
    assert top.shape[0] == n_ctx_tiles * tr and rest.shape[0] % tr == 0 and row0_tiles == 0
    return [top, rest], [pl.BlockSpec((tr, D), lambda i: (jnp.minimum(i, n_ctx_tiles - 1), 0)),
                         pl.BlockSpec((tr, D), lambda i: (jnp.maximum(i - n_ctx_tiles, 0), 0))]


def _stream_block(src_refs, n_ctx_tiles):
    if len(src_refs) == 1:
        return src_refs[0][...]
    return jnp.where(pl.program_id(0) < n_ctx_tiles, src_refs[0][...], src_refs[1][...])


def _normmod_kernel(*refs, n_src, n_ctx_tiles):
    gain_ref, sc_ref, sh_ref, h_ref = refs[n_src:]
    y = _rms(_stream_block(refs[:n_src], n_ctx_tiles), gain_ref[...])
    h_ref[...] = (y * (1.0 + sc_ref[0]) + sh_ref[0]).astype(h_ref.dtype)


def _mod_spec(D, slot, n_ctx_tiles):
    return pl.BlockSpec((1, 1, D), lambda i: (jnp.where(i < n_ctx_tiles, 6, 0) + slot, 0, 0))


def normmod(x, gain, mods, sh_slot, sc_slot, n_ctx_rows):
    tr = TOK_BLOCK
    nct = n_ctx_rows // tr
    srcs, src_specs = _stream_specs(x, tr, nct)
    R, D = sum(a.shape[0] for a in srcs), srcs[0].shape[1]
    assert R % tr == 0 and n_ctx_rows % tr == 0
    return pl.pallas_call(
        functools.partial(_normmod_kernel, n_src=len(srcs), n_ctx_tiles=nct),
        grid=(R // tr,),
        in_specs=[*src_specs, pl.BlockSpec((1, D), lambda i: (0, 0)),
                  _mod_spec(D, sc_slot, nct), _mod_spec(D, sh_slot, nct)],
        out_specs=pl.BlockSpec((tr, D), lambda i: (i, 0)),
        out_shape=jax.ShapeDtypeStruct((R, D), BF16),
        compiler_params=_params("arbitrary"),
        name="normmod",
    )(*srcs, gain.reshape(1, D), mods, mods)


def _resid_kernel(*refs, n_src, n_ctx_tiles, with_next):
    y_ref, gpost_ref, g_ref, *rest = refs[n_src:]
    xn = _stream_block(refs[:n_src], n_ctx_tiles) + g_ref[0] * _rms(y_ref[...], gpost_ref[...])
    if with_next:
        gpre_ref, sc_ref, sh_ref, xo_ref, h_ref = rest
        xo_ref[...] = xn
        h_ref[...] = (_rms(xn, gpre_ref[...]) * (1.0 + sc_ref[0]) + sh_ref[0]).astype(h_ref.dtype)
    else:
        (xo_ref,) = rest
        xo_ref[...] = xn


def resid_norm(x, y, gain_post, mods, g_slot, n_ctx_rows, nxt=None, x_row0=0):
    R, D = y.shape
    tr = TOK_BLOCK
    assert R % tr == 0 and n_ctx_rows % tr == 0 and x_row0 % tr == 0
    nct = n_ctx_rows // tr
    srcs, src_specs = _stream_specs(x, tr, nct, x_row0 // tr)
    assert sum(a.shape[0] for a in srcs) == R + x_row0
    row = pl.BlockSpec((tr, D), lambda i: (i, 0))
    vec = pl.BlockSpec((1, D), lambda i: (0, 0))
    in_specs = [*src_specs, row, vec, _mod_spec(D, g_slot, nct)]
    args = [*srcs, y, gain_post.reshape(1, D), mods]
    out_specs = [row]
    out_shape = [jax.ShapeDtypeStruct((R, D), F32)]
    if nxt is not None:
        gain_pre, mods_n, sh_slot, sc_slot = nxt
        in_specs += [vec, _mod_spec(D, sc_slot, nct), _mod_spec(D, sh_slot, nct)]
        args += [gain_pre.reshape(1, D), mods_n, mods_n]
        out_specs.append(row)
        out_shape.append(jax.ShapeDtypeStruct((R, D), BF16))
    out = pl.pallas_call(
        functools.partial(_resid_kernel, n_src=len(srcs), n_ctx_tiles=nct, with_next=nxt is not None),
        grid=(R // tr,),
        in_specs=in_specs, out_specs=out_specs, out_shape=out_shape,
        compiler_params=_params("arbitrary"),
        name="resid_norm",
    )(*args)
    return out if nxt is not None else out[0]


def _ffn_kernel(te_ref, nu_ref, x_ref, wg_ref, wu_ref, *rest, experts_per_part):
    *wd_refs, o_ref = rest
    i, j = pl.program_id(0), pl.program_id(1)

    @pl.when(i < nu_ref[0])
    def _():
        x = x_ref[...]
        g = jnp.dot(x, wg_ref[0], preferred_element_type=F32)
        u = jnp.dot(x, wu_ref[0], preferred_element_type=F32)
        hid = (g * jax.nn.sigmoid(g) * u).astype(BF16)

        wd = wd_refs[0][0]
        part_id = te_ref[i] // experts_per_part
        for p in range(1, len(wd_refs)):
            wd = jnp.where(part_id == p, wd_refs[p][0], wd)
        part = jnp.dot(hid, wd, preferred_element_type=F32)

        @pl.when(j == 0)
        def _():
            o_ref[...] = part

        @pl.when(j > 0)
        def _():
            o_ref[...] += part

    @pl.when(jnp.logical_and(i >= nu_ref[0], j == 0))
    def _():
        o_ref[...] = jnp.zeros_like(o_ref)


def swiglu_ffn(x, w_gate, w_up, w_down_parts, tile_expert, n_used, tm, tf):
    R, D = x.shape
    E, _, F = w_gate.shape
    n_parts = len(w_down_parts)
    epp = E // n_parts
    assert R % tm == 0 and F % tf == 0 and w_gate.dtype == BF16 and x.dtype == BF16 and E % n_parts == 0
    nf = F // tf

    def fblk(i, j, te, nu):
        return jnp.where(i < nu[0], j, nf - 1)

    def down_spec(p):
        def index(i, j, te, nu):
            mine = te[i] // epp == p
            return (jnp.where(mine, te[i] - p * epp, 0), jnp.where(mine, fblk(i, j, te, nu), 0), 0)
        return pl.BlockSpec((1, tf, D), index)

    in_specs = [pl.BlockSpec((tm, D), lambda i, j, te, nu: (i, 0)),
                pl.BlockSpec((1, D, tf), lambda i, j, te, nu: (te[i], 0, fblk(i, j, te, nu))),
                pl.BlockSpec((1, D, tf), lambda i, j, te, nu: (te[i], 0, fblk(i, j, te, nu))),
                *[down_spec(p) for p in range(n_parts)]]
    args = [x, w_gate, w_up, *w_down_parts]
    return pl.pallas_call(
        functools.partial(_ffn_kernel, experts_per_part=epp),
        grid_spec=pltpu.PrefetchScalarGridSpec(
            num_scalar_prefetch=2,
            grid=(R // tm, nf),
            in_specs=in_specs,
            out_specs=pl.BlockSpec((tm, D), lambda i, j, te, nu: (i, 0)),
        ),
        out_shape=jax.ShapeDtypeStruct((R, D), F32),
        compiler_params=_params("arbitrary", "arbitrary"),
        name="swiglu_ffn",
    )(tile_expert, n_used, *args)


def _rope(x, c, sg):
    lane = lax.broadcasted_iota(jnp.int32, x.shape, 1)
    first_half = jnp.bitwise_and(lane, HEAD_DIM // 2 - 1) < HEAD_DIM // 4
    partner = jnp.where(first_half, pltpu.roll(x, HEAD_DIM - HEAD_DIM // 4, 1), pltpu.roll(x, HEAD_DIM // 4, 1))
    return x * c + partner * sg


def _attn_kernel(sink_ref, q_ref, *rest, has_window, seq_len):
    if has_window:
        (kp_ref, ko_ref, kn_ref, vp_ref, vo_ref, vn_ref, cp_ref, co_ref, cn_ref, sp_ref, so_ref, sn_ref,
         kc_ref, vc_ref, o_ref) = rest
    else:
        kc_ref, vc_ref, o_ref = rest
    i = pl.program_id(0)
    BQ = q_ref.shape[0]
    L = kc_ref.shape[0]
    G = ATT_KV_HEADS
    R = ATT_HEADS // G
    n_win = 3 * BQ if has_window else 0
    nk = n_win + L
    if has_window:
        row = lax.broadcasted_iota(jnp.int32, (R * BQ, nk), 0)
        col = lax.broadcasted_iota(jnp.int32, (R * BQ, nk), 1)
        qi = jnp.bitwise_and(row, BQ - 1)
        kpos = i * BQ + col - BQ
        in_band = jnp.abs(col - BQ - qi) <= WINDOW
        in_seq = jnp.logical_and(kpos >= 0, kpos < seq_len)
        valid = jnp.logical_or(col >= n_win, jnp.logical_and(in_band, in_seq))
    for g in range(G):
        cs = slice(g * HEAD_DIM, (g + 1) * HEAD_DIM)
        q_heads = [q_ref[:, (g * R + r) * HEAD_DIM:(g * R + r + 1) * HEAD_DIM] for r in range(R)]
        if has_window:
            q_heads = [_rope(q, co_ref[...], so_ref[...]) for q in q_heads]
            k_all = jnp.concatenate([_rope(kp_ref[:, cs], cp_ref[...], sp_ref[...]),
                                     _rope(ko_ref[:, cs], co_ref[...], so_ref[...]),
                                     _rope(kn_ref[:, cs], cn_ref[...], sn_ref[...]), kc_ref[:, cs]], axis=0)
            v_all = jnp.concatenate([vp_ref[:, cs], vo_ref[:, cs], vn_ref[:, cs], vc_ref[:, cs]], axis=0)
        else:
            k_all, v_all = kc_ref[:, cs], vc_ref[:, cs]
        q_g = (jnp.concatenate(q_heads, axis=0) * (HEAD_DIM ** -0.5)).astype(BF16)
        s = lax.dot_general(q_g, k_all.astype(BF16), (((1,), (1,)), ((), ())),
                            preferred_element_type=F32)
        if has_window:
            s = jnp.where(valid, s, -jnp.inf)
        sink = jnp.concatenate([jnp.full((BQ, 1), sink_ref[g * R + r], F32) for r in range(R)], axis=0)
        m = jnp.maximum(sink, jnp.max(s, axis=-1, keepdims=True))
        p = jnp.exp(s - m)
        denom = jnp.exp(sink - m) + jnp.sum(p, axis=-1, keepdims=True)
        o = jnp.dot(p.astype(BF16), v_all.astype(BF16), preferred_element_type=F32) / denom
        for r in range(R):
            h = g * R + r
            o_ref[:, h * HEAD_DIM:(h + 1) * HEAD_DIM] = o[r * BQ:(r + 1) * BQ].astype(o_ref.dtype)


def sink_attention(proj, q_col, k_col, v_col, L, sink, rope):
    QW, KW = ATT_HEADS * HEAD_DIM, ATT_KV_HEADS * HEAD_DIM
    has_window = rope is not None
    S = proj.shape[0] - L if has_window else L
    BQ = ATT_BLOCK if has_window else L
    nb = S // BQ
    assert S % BQ == 0 and L % BQ == 0
    r0 = L // BQ if has_window else 0
    in_specs = [pl.BlockSpec(memory_space=pltpu.SMEM), pl.BlockSpec((BQ, QW), lambda i: (r0 + i, q_col))]
    args = [sink, proj]
    if has_window:
        prev = lambda i: jnp.maximum(i - 1, 0)
        nxt = lambda i: jnp.minimum(i + 1, nb - 1)
        for col in (k_col, v_col):
            in_specs += [pl.BlockSpec((BQ, KW), lambda i, col=col: (r0 + prev(i), col)),
                         pl.BlockSpec((BQ, KW), lambda i, col=col: (r0 + i, col)),
                         pl.BlockSpec((BQ, KW), lambda i, col=col: (r0 + nxt(i), col))]
            args += [proj] * 3
        for tab in rope:
            in_specs += [pl.BlockSpec((BQ, HEAD_DIM), lambda i: (prev(i), 0)),
                         pl.BlockSpec((BQ, HEAD_DIM), lambda i: (i, 0)),
                         pl.BlockSpec((BQ, HEAD_DIM), lambda i: (nxt(i), 0))]
            args += [tab] * 3
    in_specs += [pl.BlockSpec((L, KW), lambda i: (0, k_col)), pl.BlockSpec((L, KW), lambda i: (0, v_col))]
    args += [proj, proj]
    return pl.pallas_call(
        functools.partial(_attn_kernel, has_window=has_window, seq_len=S),
        grid=(nb,),
        in_specs=in_specs,
        out_specs=pl.BlockSpec((BQ, QW), lambda i: (i, 0)),
        out_shape=jax.ShapeDtypeStruct((S, QW), BF16),
        compiler_params=_params("arbitrary"),
        name="sink_attention",
    )(*args)


def _rope_tables(n_tokens):
    n_freq = HEAD_DIM // 4
    t = jnp.arange(n_tokens)
    row = (t // GRID_W).astype(F32)
    col = (t % GRID_W).astype(F32)
    inv = ROPE_THETA ** (-jnp.arange(n_freq, dtype=F32) / n_freq)
    ar, ac = row[:, None] * inv, col[:, None] * inv
    c = jnp.concatenate([jnp.cos(ar), jnp.cos(ar), jnp.cos(ac), jnp.cos(ac)], axis=1)
    sg = jnp.concatenate([-jnp.sin(ar), jnp.sin(ar), -jnp.sin(ac), jnp.sin(ac)], axis=1)
    return c, sg


CHUNK_SHIFT = CHUNK.bit_length() - 1
assert 1 << CHUNK_SHIFT == CHUNK


def _chunk_masks(tb, reverse, strict):
    r = lax.broadcasted_iota(jnp.int32, (tb, tb), 0)
    c = lax.broadcasted_iota(jnp.int32, (tb, tb), 1)
    same = jnp.right_shift(r, CHUNK_SHIFT) == jnp.right_shift(c, CHUNK_SHIFT)
    if reverse:
        tri = (c > r) if strict else (c >= r)
    else:
        tri = (c < r) if strict else (c <= r)
    return same, jnp.logical_and(same, tri)


def _chunk_sums(ld, same, tri):
    tb = ld.shape[0]
    sel = jnp.concatenate([jnp.where(tri, 1.0, 0.0), jnp.where(same, 1.0, 0.0)], axis=0).astype(BF16)
    hi = ld.astype(BF16)
    rest = ld - hi.astype(F32)
    mid = rest.astype(BF16)
    lo = (rest - mid.astype(F32)).astype(BF16)
    dot = lambda p: jnp.dot(sel, p, preferred_element_type=F32)
    g = (dot(lo) + dot(mid)) + dot(hi)
    return g[:tb], g[tb:]


def _dot_tn(a, b):
    return lax.dot_general(a, b, (((0,), (0,)), ((), ())), preferred_element_type=F32)


def _dot_nt(a, b):
    return lax.dot_general(a, b, (((1,), (1,)), ((), ())), preferred_element_type=F32)


def _chunkrec_core(load_head, n_heads, tb, V, reverse, st_ref, dst_ref):
    @pl.when(pl.program_id(0) == 0)
    def _():
        st_ref[...] = jnp.zeros_like(st_ref)

    same, tri = _chunk_masks(tb, reverse, strict=False)
    loaded = [load_head(h) for h in range(n_heads)]
    K = loaded[0][3].shape[1]
    g_cum_all, g_tot_all = _chunk_sums(jnp.concatenate([ld for _, _, _, ld in loaded], axis=1), same, tri)
    per_head = []
    for h in range(n_heads):
        q, k, v, _ = loaded[h]
        g_cum, g_tot = g_cum_all[:, h * K:(h + 1) * K], g_tot_all[:, h * K:(h + 1) * K]
        q_dec = (q * jnp.exp(g_cum)).astype(BF16)
        k_inv = (k * jnp.exp(-g_cum)).astype(BF16)
        k_tail = k * jnp.exp(g_tot - g_cum)
        a = jnp.where(tri, _dot_nt(q_dec, k_inv), 0.0)
        o_intra = jnp.dot(a.astype(BF16), v.astype(BF16), preferred_element_type=F32)
        per_head.append((q_dec, k_tail, v, g_tot, o_intra))
    n_chunks = tb // CHUNK
    order = range(n_chunks - 1, -1, -1) if reverse else range(n_chunks)
    sts = [st_ref[h] for h in range(n_heads)]
    for c in order:
        sl = slice(c * CHUNK, (c + 1) * CHUNK)
        for h in range(n_heads):
            q_dec, k_tail, v, g_tot, o_intra = per_head[h]
            st = sts[h]
            dst_ref[sl, h * V:(h + 1) * V] = o_intra[sl] + _dot_nt(q_dec[sl], st.astype(BF16))
            d = jnp.exp(g_tot[c * CHUNK:c * CHUNK + 1, :])
            sts[h] = st * d + _dot_tn(v[sl], k_tail[sl])
    for h in range(n_heads):
        st_ref[h] = sts[h]


def _gated_head_norm(o, gain, gate_raw):
    y = o * lax.rsqrt(jnp.mean(o * o, axis=-1, keepdims=True) + NORM_EPS) * gain
    return y * (gate_raw * jax.nn.sigmoid(gate_raw))


def _chunk_mixer_kernel(*refs, reverse, n_heads, K, V, load_heads):
    if reverse:
        *in_refs, o_ref, st_ref = refs
        dst_ref = o_ref
    else:
        *in_refs, gate_ref, orev_ref, gain_ref, o_ref, st_ref, dst_ref = refs
    tb = o_ref.shape[0]
    _chunkrec_core(load_heads(*in_refs), n_heads, tb, V, reverse, st_ref, dst_ref)
    if not reverse:
        for h in range(n_heads):
            vs = slice(h * V, (h + 1) * V)
            o_ref[:, vs] = _gated_head_norm(dst_ref[:, vs] + orev_ref[:, vs], gain_ref[...],
                                            gate_ref[:, vs]).astype(o_ref.dtype)


def _hgrn_heads(qh_ref, ih_ref, fr_ref, lb_ref):
    def load(h):
        cs = slice(h * 128, (h + 1) * 128)
        lb = lb_ref[:, cs]
        f = lb + (1.0 - lb) * jax.nn.sigmoid(fr_ref[:, cs])
        qh = qh_ref[:, cs]
        return qh * jax.nn.sigmoid(qh), 1.0 - f, ih_ref[:, cs], jnp.log(f)
    return load


def _gla_heads(gq_ref, gk_ref, gv_ref, gd_ref, up_ref, bias_ref):
    x = jnp.dot(gd_ref[...].astype(BF16), up_ref[...].astype(BF16), preferred_element_type=F32) + bias_ref[...]
    lg = (jnp.minimum(x, 0.0) - jnp.log(1.0 + jnp.exp(-jnp.abs(x)))) * (1.0 / GLA_GATE_NORMALIZER)

    def load(h):
        ks, vs = slice(h * 128, (h + 1) * 128), slice(h * 256, (h + 1) * 256)
        return gq_ref[:, ks] * (128 ** -0.5), gk_ref[:, ks], gv_ref[:, vs], lg[:, ks]
    return load


def _seq_block_index(nblk, nctx, reverse):
    if not reverse:
        return lambda t: t
    return lambda t: jnp.where(t < nctx, nctx - 1 - t, nblk - 1 - (t - nctx))


def chunk_mixer(load_heads, inputs, n_heads, K, V, n_ctx_rows, reverse, final=None):
    T = inputs[0][0].shape[0]
    tb = TOK_BLOCK
    assert T % tb == 0 and n_ctx_rows % tb == 0
    nblk, nctx = T // tb, n_ctx_rows // tb
    blk = _seq_block_index(nblk, nctx, reverse)

    def spec(item):
        if len(item) == 1:
            return pl.BlockSpec(item[0].shape, lambda t: (0, 0))
        _, width, cb = item
        return pl.BlockSpec((tb, width), lambda t: (blk(t), cb))

    HV = n_heads * V
    ospec = pl.BlockSpec((tb, HV), lambda t: (blk(t), 0))
    scratch = [pltpu.VMEM((n_heads, V, K), F32)]
    if not reverse:
        gate, o_rev, gain = final
        inputs = list(inputs) + [gate, (o_rev, HV, 0), (gain.reshape(1, V),)]
        scratch.append(pltpu.VMEM((tb, HV), F32))
    return pl.pallas_call(
        functools.partial(_chunk_mixer_kernel, reverse=reverse, n_heads=n_heads, K=K, V=V, load_heads=load_heads),
        grid=(nblk,),
        in_specs=[spec(it) for it in inputs],
        out_specs=ospec,
        out_shape=jax.ShapeDtypeStruct((T, HV), F32 if reverse else BF16),
        scratch_shapes=scratch,
        compiler_params=_params("arbitrary"),
        name="chunk_mixer",
    )(*[it[0] for it in inputs])


def _mm_bf(a, b):
    return jnp.dot(a.astype(BF16), b.astype(BF16), preferred_element_type=F32)


def _softplus(x):
    return jnp.maximum(x, 0.0) + jnp.log(1.0 + jnp.exp(-jnp.abs(x)))


def _head_sums(x, bd_ref):
    hi = x.astype(BF16)
    lo = (x - hi.astype(F32)).astype(BF16)
    bd = bd_ref[...]
    W = bd.shape[0]
    groups = []
    for g in range(x.shape[1] // W):
        gs = slice(g * W, (g + 1) * W)
        groups.append(jnp.dot(lo[:, gs], bd, preferred_element_type=F32)
                      + jnp.dot(hi[:, gs], bd, preferred_element_type=F32))
    return jnp.concatenate(groups, axis=1)


def _rwkv_prep_kernel(p_ref, pp_ref, pn_ref, g_ref, gp_ref, gn_ref, s_ref, sp_ref, sn_ref,
                      mup_ref, mun_ref, mugp_ref, mugn_ref, musp_ref, musn_ref,
                      w2f_ref, w2b_ref, a2_ref, g2_ref, w0f_ref, w0b_ref, a0_ref, kk_ref, ka_ref, rk_ref, bd_ref,
                      r_out, k_out, v_out, kk_out, b_out, lwf_out, lwb_out, go_out, bon_out,
                      gcf_out, gtf_out, gcb_out, gtb_out, *, n_ctx_blocks):
    t, nblk = pl.program_id(0), pl.num_programs(0)
    tb = p_ref.shape[0]
    C = RWKV_HEADS * RWKV_N
    keep_prev = jnp.where(jnp.logical_or(t == 0, t == n_ctx_blocks), 0.0, 1.0)
    keep_next = jnp.where(jnp.logical_or(t == n_ctx_blocks - 1, t == nblk - 1), 0.0, 1.0)

    def shifted(x_ref, xp_ref, xn_ref, mu_p_ref, mu_n_ref):
        x = x_ref[...]
        rows = lax.broadcasted_iota(jnp.int32, x.shape, 0)
        prev = jnp.where(rows == 0, xp_ref[7:8, :] * keep_prev, pltpu.roll(x, 1, 0))
        nxt = jnp.where(rows == tb - 1, xn_ref[0:1, :] * keep_next, pltpu.roll(x, tb - 1, 0))
        return x + mu_p_ref[...] * (prev - x) + mu_n_ref[...] * (nxt - x)

    rkv = shifted(p_ref, pp_ref, pn_ref, mup_ref, mun_ref)
    rr, rk, rv = rkv[:, :C], rkv[:, C:2 * C], rkv[:, 2 * C:]
    low = shifted(s_ref, sp_ref, sn_ref, musp_ref, musn_ref)
    gd = shifted(g_ref, gp_ref, gn_ref, mugp_ref, mugn_ref)
    tl = jnp.tanh(low).astype(BF16)
    for w0_ref, w2_ref, lw_out, gc_out, gt_out, reverse in ((w0f_ref, w2f_ref, lwf_out, gcf_out, gtf_out, False),
                                                            (w0b_ref, w2b_ref, lwb_out, gcb_out, gtb_out, True)):
        lw = -jnp.exp(-_softplus(-(w0_ref[...] + jnp.dot(tl, w2_ref[...], preferred_element_type=F32))) - 0.5)
        lw_out[...] = lw
        same, tri = _chunk_masks(tb, reverse, strict=False)
        gc_out[...], gt_out[...] = _chunk_sums(lw, same, tri)
    a_sig = jax.nn.sigmoid(a0_ref[...] + jnp.dot(low.astype(BF16), a2_ref[...], preferred_element_type=F32))
    go_out[...] = jnp.dot(jax.nn.sigmoid(gd).astype(BF16), g2_ref[...], preferred_element_type=F32)
    kk = rk * kk_ref[...]
    kk = kk * lax.rsqrt(_head_sums(kk * kk, bd_ref) + 1e-12)
    k_mod = rk * (1.0 + (a_sig - 1.0) * ka_ref[...])
    r_out[...] = rr
    k_out[...] = k_mod
    v_out[...] = rv
    kk_out[...] = kk
    b_out[...] = kk * a_sig
    bon_out[...] = _head_sums(rr * k_mod * rk_ref[...], bd_ref) * rv


def _rwkv_pre_kernel(r_ref, k_ref, v_ref, kk_ref, b_ref, lw_ref, gc_ref, gt_ref, *rest, reverse):
    n_cast = (len(rest) - 4) // 2
    w32_refs, (qp_ref, ol_ref, plt_ref, zt_ref), w16_refs = rest[:n_cast], rest[n_cast:n_cast + 4], rest[n_cast + 4:]
    for w32_ref, w16_ref in zip(w32_refs, w16_refs):
        w16_ref[...] = w32_ref[...].astype(BF16)
    N = RWKV_N
    tb = r_ref.shape[0]
    hb = r_ref.shape[1] // N
    _, tri_incl = _chunk_masks(tb, reverse, strict=False)
    _, tri_strict = _chunk_masks(tb, reverse, strict=True)
    xs, lps, rest = [], [], []
    for h in range(hb):
        hs = slice(h * N, (h + 1) * N)
        r, k, v, b, lw, g_cum, g_tot = (ref[:, hs] for ref in (r_ref, k_ref, v_ref, b_ref, lw_ref, gc_ref, gt_ref))
        a = -kk_ref[:, hs]
        e_neg = jnp.exp(-g_cum)
        e_tail = jnp.exp(g_tot - g_cum)
        a_t = (a * jnp.exp(g_cum - lw)).astype(BF16)
        r_t = (r * jnp.exp(g_cum)).astype(BF16)
        b_t = (b * e_neg).astype(BF16)
        k_t = (k * e_neg).astype(BF16)
        a_ab = jnp.where(tri_strict, _dot_nt(a_t, b_t), 0.0)
        a_ak = jnp.where(tri_strict, _dot_nt(a_t, k_t), 0.0)
        a_rb = jnp.where(tri_incl, _dot_nt(r_t, b_t), 0.0)
        a_rk = jnp.where(tri_incl, _dot_nt(r_t, k_t), 0.0)
        xs.append(jnp.concatenate([a_t.astype(F32), _mm_bf(a_ak, v)], axis=1))
        lps.append(a_ab)
        rest.append((v, b * e_tail, k * e_tail, a_rb,
                     jnp.concatenate([r_t.astype(F32), _mm_bf(a_rk, v)], axis=1)))
    for j in range(CHUNK_SHIFT):
        xs = [x + _mm_bf(lp, x) for x, lp in zip(xs, lps)]
        if j < CHUNK_SHIFT - 1:
            lps = [_mm_bf(lp, lp) for lp in lps]
    for h in range(hb):
        v, b_h, k_h, a_rb, qo0 = rest[h]
        x = xs[h]
        qo = qo0 + _mm_bf(a_rb, x)
        qp_ref[:, h * N:(h + 1) * N] = qo[:, :N]
        ol_ref[:, h * N:(h + 1) * N] = qo[:, N:]
        w, uloc = x[:, :N], x[:, N:]
        for c in range(tb // CHUNK):
            sl = slice(c * CHUNK, (c + 1) * CHUNK)
            plt_ref[h, c] = _dot_tn(w[sl], b_h[sl])
            zt_ref[h, c] = _dot_tn(uloc[sl], b_h[sl]) + _dot_tn(v[sl], k_h[sl])


def _rwkv_scan_kernel(qp_ref, ol_ref, plt_ref, zt_ref, gt_ref, *rest, reverse):
    if reverse:
        o_ref, st_ref = rest
        dst_ref = o_ref
    else:
        orev_ref, go_ref, bon_ref, lnw_ref, lnb_ref, bd_ref, o_ref, st_ref, dst_ref = rest

    @pl.when(pl.program_id(0) == 0)
    def _():
        st_ref[...] = jnp.zeros_like(st_ref)

    N = RWKV_N
    tb = qp_ref.shape[0]
    n_heads = qp_ref.shape[1] // N
    n_chunks = tb // CHUNK
    order = range(n_chunks - 1, -1, -1) if reverse else range(n_chunks)
    sts = [st_ref[h] for h in range(n_heads)]
    for c in order:
        sl = slice(c * CHUNK, (c + 1) * CHUNK)
        for h in range(n_heads):
            hs = slice(h * N, (h + 1) * N)
            st = sts[h]
            dst_ref[sl, hs] = _dot_nt(qp_ref[sl, hs].astype(BF16), st.astype(BF16)) + ol_ref[sl, hs]
            d = jnp.exp(gt_ref[c * CHUNK:c * CHUNK + 1, hs])
            sts[h] = st * d + _mm_bf(st, plt_ref[h, c]) + zt_ref[h, c]
    for h in range(n_heads):
        st_ref[h] = sts[h]
    if not reverse:
        o = dst_ref[...] + orev_ref[...]
        dev = o - _head_sums(o, bd_ref) * (1.0 / N)
        var = _head_sums(dev * dev, bd_ref) * (1.0 / N)
        on = dev * lax.rsqrt(var + RWKV_LN_EPS) * lnw_ref[...] + lnb_ref[...]
        o_ref[...] = ((on + bon_ref[...]) * go_ref[...]).astype(o_ref.dtype)


def rwkv7_branch(pb, ps, rkv_col, gd_col, n_ctx_rows, mu_prev, mu_next, low_cols, w0_f, w0_b, w2_f, w2_b,
                 a0, a2, g2, k_k, k_a, r_k, ln_w, ln_b, to_bf16):
    T = pb.shape[0]
    H, N = RWKV_HEADS, RWKV_N
    C = H * N
    tb = TOK_BLOCK
    assert T % tb == 0 and n_ctx_rows % tb == 0
    nblk, nctx = T // tb, n_ctx_rows // tb
    ncb = tb // CHUNK
    GW, SW, R = g2.shape[0], ps.shape[1], w2_f.shape[0]
    wf_col, wb_col, ad_col = low_cols
    s0 = 3 * C
    row = lambda v: v.reshape(1, -1)

    def padded_rows(w, r0):
        return jnp.zeros((SW, C), F32).at[r0:r0 + w.shape[0]].set(w).astype(BF16)

    def mu_low(mu):
        return jnp.zeros((1, SW), F32).at[0, wf_col:wf_col + 3 * R].set(mu[s0:s0 + 3 * R])

    bdw = 2 * LANES
    bd = (jnp.arange(bdw)[:, None] // N == jnp.arange(bdw)[None, :] // N).astype(BF16)
    r8 = tb // 8

    def halo(width, col):
        return (pl.BlockSpec((tb, width), lambda t: (t, col)),
                pl.BlockSpec((8, width), lambda t: (jnp.maximum(t * r8 - 1, 0), col)),
                pl.BlockSpec((8, width), lambda t: (jnp.minimum((t + 1) * r8, T // 8 - 1), col)))

    whole = lambda a: pl.BlockSpec(a.shape, lambda t: (0, 0))
    consts = [row(mu_prev[:s0]), row(mu_next[:s0]), row(mu_prev[s0 + 3 * R:]), row(mu_next[s0 + 3 * R:]),
              mu_low(mu_prev), mu_low(mu_next),
              padded_rows(w2_f, wf_col), padded_rows(w2_b, wb_col), padded_rows(a2, ad_col), g2.astype(BF16),
              row(w0_f), row(w0_b), row(a0), row(k_k), row(k_a), row(r_k), bd]
    tokC = pl.BlockSpec((tb, C), lambda t: (t, 0))
    shapeC = jax.ShapeDtypeStruct((T, C), F32)
    r, k, v, kk, b, lw_f, lw_b, g_out, bonus, gc_f, gt_f, gc_b, gt_b = pl.pallas_call(
        functools.partial(_rwkv_prep_kernel, n_ctx_blocks=nctx),
        grid=(nblk,),
        in_specs=[*halo(3 * C, rkv_col), *halo(GW, gd_col), *halo(SW, 0), *[whole(a) for a in consts]],
        out_specs=[tokC] * 13,
        out_shape=[shapeC] * 13,
        compiler_params=_params("arbitrary"),
        name="rwkv_prep",
    )(pb, pb, pb, pb, pb, pb, ps, ps, ps, *consts)

    hb = RWKV_HEAD_BLOCK
    tok = pl.BlockSpec((tb, hb * N), lambda h, t: (t, h))
    mat = pl.BlockSpec((hb, ncb, N, N), lambda h, t: (h, t, 0, 0))
    mat_shape = jax.ShapeDtypeStruct((H, T // CHUNK, N, N), F32)
    o_rev = None
    cast_done = []
    def cast_specs(job):
        w32, row0, wr = job
        rows = BF16_SUBLANES * pl.cdiv(wr, BF16_SUBLANES * (H // hb) * nblk)
        assert wr % rows == 0 and row0 % rows == 0
        block = lambda h, t: jnp.minimum(h * nblk + t, wr // rows - 1)
        return (pl.BlockSpec((rows, w32.shape[1]), lambda h, t: (row0 // rows + block(h, t), 0)),
                pl.BlockSpec((rows, w32.shape[1]), lambda h, t: (block(h, t), 0)),
                jax.ShapeDtypeStruct((wr, w32.shape[1]), BF16))

    for lw, gc, gt, jobs, reverse in ((lw_b, gc_b, gt_b, to_bf16[0], True), (lw_f, gc_f, gt_f, to_bf16[1], False)):
        cin, cout, cshape = zip(*[cast_specs(job) for job in jobs])
        qp, ol, plt, zt, *w16s = pl.pallas_call(
            functools.partial(_rwkv_pre_kernel, reverse=reverse),
            grid=(H // hb, nblk),
            in_specs=[tok] * 8 + list(cin),
            out_specs=[tok, tok, mat, mat] + list(cout),
            out_shape=[shapeC, shapeC, mat_shape, mat_shape] + list(cshape),
            compiler_params=_params("arbitrary", "arbitrary"),
            name="rwkv_pre",
        )(r, k, v, kk, b, lw, gc, gt, *[job[0] for job in jobs])
        cast_done.append(w16s)
        blk = _seq_block_index(nblk, nctx, reverse)
        tok_all = pl.BlockSpec((tb, C), lambda t: (blk(t), 0))
        mat_all = pl.BlockSpec((H, ncb, N, N), lambda t: (0, blk(t), 0, 0))
        in_specs = [tok_all, tok_all, mat_all, mat_all, tok_all]
        args = [qp, ol, plt, zt, gt]
        scratch = [pltpu.VMEM((H, N, N), F32)]
        if not reverse:
            in_specs += [tok_all, tok_all, tok_all, whole(row(ln_w)), whole(row(ln_b)), whole(bd)]
            args += [o_rev, g_out, bonus, row(ln_w), row(ln_b), bd]
            scratch.append(pltpu.VMEM((tb, C), F32))
        out = pl.pallas_call(
            functools.partial(_rwkv_scan_kernel, reverse=reverse),
            grid=(nblk,),
            in_specs=in_specs,
            out_specs=tok_all,
            out_shape=jax.ShapeDtypeStruct((T, C), F32 if reverse else BF16),
            scratch_shapes=scratch,
            compiler_params=_params("arbitrary"),
            name="rwkv_scan",
        )(*args)
        if reverse:
            o_rev = out
    return out, cast_done


def _row_copy(src_hbm, src_row, dst_ref, dst_row, sem):
    return pltpu.make_async_copy(src_hbm.at[pl.ds(src_row, 1)], dst_ref.at[pl.ds(dst_row, 1)], sem)


def _gather_kernel(src_ref, nxt_ref, x_hbm, o_ref, buf, sems):
    i, n_steps = pl.program_id(0), pl.num_programs(0)
    n = o_ref.shape[0]

    def issue(idx_ref, slot):
        def body(r, carry):
            for u in range(2):
                row = 2 * r + u
                _row_copy(x_hbm, idx_ref[0, 0, row], buf.at[slot], row, sems.at[slot]).start(priority=u)
            return carry
        lax.fori_loop(0, n // 2, body, 0, unroll=ROW_COPY_UNROLL)

    @pl.when(i == 0)
    def _():
        issue(src_ref, 0)

    @pl.when(i + 1 < n_steps)
    def _():
        issue(nxt_ref, (i + 1) % 2)

    slot = i % 2

    pltpu.make_async_copy(x_hbm.at[pl.ds(0, n)], buf.at[slot], sems.at[slot]).wait()
    o_ref[...] = buf[slot].astype(o_ref.dtype)


def gather_rows(x, src, tg):
    S, D = x.shape
    P = src.shape[0]
    sub = D // LANES
    assert P % tg == 0 and x.dtype == BF16 and sub == BF16_SUBLANES
    n_steps = P // tg
    src3 = src.reshape(n_steps, 1, tg)
    out = pl.pallas_call(
        _gather_kernel,
        grid=(n_steps,),
        in_specs=[pl.BlockSpec((1, 1, tg), lambda i: (i, 0, 0), memory_space=pltpu.SMEM),
                  pl.BlockSpec((1, 1, tg), lambda i: (jnp.minimum(i + 1, n_steps - 1), 0, 0),
                               memory_space=pltpu.SMEM),
                  pl.BlockSpec(memory_space=pl.ANY)],
        out_specs=pl.BlockSpec((tg, sub, LANES), lambda i: (i, 0, 0)),
        out_shape=jax.ShapeDtypeStruct((P, sub, LANES), BF16),
        scratch_shapes=[pltpu.VMEM((2, tg, sub, LANES), BF16), pltpu.SemaphoreType.DMA((2,))],
        compiler_params=_params("arbitrary"),
        name="gather_rows",
    )(src3, src3, x.reshape(S, sub, LANES))
    return out.reshape(P, D)


def _combine_kernel(pos_ref, nxt_ref, x_ref, w_ref, ys_hbm, gpost_ref, g_ref, o_ref, buf, sems):
    i, n_steps = pl.program_id(0), pl.num_programs(0)
    n = x_ref.shape[0]

    def issue(idx_ref, slot):
        def body(r, carry):
            for k in range(2):
                _row_copy(ys_hbm, idx_ref[0, 0, 2 * r + k], buf.at[slot, k], r, sems.at[slot, k]).start(priority=k)
            return carry
        lax.fori_loop(0, n, body, 0, unroll=ROW_COPY_UNROLL)

    @pl.when(i == 0)
    def _():
        issue(pos_ref, 0)

    @pl.when(i + 1 < n_steps)
    def _():
        issue(nxt_ref, (i + 1) % 2)

    slot = i % 2

    for k in range(2):
        pltpu.make_async_copy(ys_hbm.at[pl.ds(0, n)], buf.at[slot, k], sems.at[slot, k]).wait()
    w = w_ref[...]
    f = w[:, 0:1] * buf[slot, 0] + w[:, 1:2] * buf[slot, 1]
    o_ref[...] = x_ref[...] + g_ref[0] * _rms(f, gpost_ref[...])


def combine_resid(x, ys, pos, weights, gain_post, mods, g_slot):
    S, D = x.shape
    tc = TOK_BLOCK
    assert S % tc == 0
    n_steps = S // tc
    row = pl.BlockSpec((tc, D), lambda i: (i, 0))
    pos3 = pos.reshape(n_steps, 1, 2 * tc)
    return pl.pallas_call(
        _combine_kernel,
        grid=(n_steps,),
        in_specs=[pl.BlockSpec((1, 1, 2 * tc), lambda i: (i, 0, 0), memory_space=pltpu.SMEM),
                  pl.BlockSpec((1, 1, 2 * tc), lambda i: (jnp.minimum(i + 1, n_steps - 1), 0, 0),
                               memory_space=pltpu.SMEM),
                  row, pl.BlockSpec((tc, 2), lambda i: (i, 0)), pl.BlockSpec(memory_space=pl.ANY),
                  pl.BlockSpec((1, D), lambda i: (0, 0)), _mod_spec(D, g_slot, 0)],
        out_specs=row,
        out_shape=jax.ShapeDtypeStruct((S, D), F32),
        scratch_shapes=[pltpu.VMEM((2, 2, tc, D), F32), pltpu.SemaphoreType.DMA((2, 2))],
        compiler_params=_params("arbitrary"),
        name="combine_resid",
    )(pos3, pos3, x, weights, ys, gain_post.reshape(1, D), mods)


def _route(logits, tm):
    S = logits.shape[0]
    top_val, top_idx = lax.top_k(logits, 2)
    weights = jax.nn.softmax(top_val, axis=-1)
    e_flat = top_idx.reshape(-1)
    onehot = (e_flat[:, None] == jnp.arange(N_EXPERTS)[None, :]).astype(jnp.int32)
    rank = jnp.take_along_axis(jnp.cumsum(onehot, axis=0), e_flat[:, None], axis=1)[:, 0] - 1
    counts = jnp.sum(onehot, axis=0)
    tiles_per = (counts + tm - 1) // tm
    tile_end = jnp.cumsum(tiles_per)
    start = (tile_end - tiles_per) * tm
    dest = start[e_flat] + rank
    n_rows = 2 * S + N_EXPERTS * tm
    n_tiles = n_rows // tm
    token = jnp.arange(2 * S, dtype=jnp.int32) // 2
    src = jnp.zeros((n_rows,), jnp.int32).at[dest].set(token)
    n_used = tile_end[-1].astype(jnp.int32)
    tile_id = jnp.minimum(jnp.arange(n_tiles, dtype=jnp.int32), n_used - 1)
    tile_expert = jnp.sum((tile_end[None, :] <= tile_id[:, None]).astype(jnp.int32), axis=1)
    tile_expert = jnp.minimum(tile_expert, N_EXPERTS - 1)
    return src, weights, dest.astype(jnp.int32), tile_expert, n_used.reshape(1)


def _adaln(c, c_ctx, w, b):
    D = c.shape[-1]
    rows = jnp.zeros((8, D), F32).at[0].set(c[0]).at[1].set(c_ctx)
    m = matmul(jax.nn.silu(rows), w, 8, 2048)[:2] + b[None, :]
    return m.reshape(12, 1, D)


def _even_mixer(h, L, w_in, w_out, attn_sink, hgrn_norm, hgrn_lb):
    T = h.shape[0]
    S = T - L
    proj = matmul(h, w_in, STREAM_ROWS, 512, col_roll=1536 // 512)
    att = sink_attention(proj, 5, 24, 25, L, attn_sink, _rope_tables(S))
    att_c = sink_attention(proj, 5, 24, 25, L, attn_sink, None)
    lb = (hgrn_lb.reshape(1, -1),)
    qh, ih = (proj, 1024, 0), (proj, 1024, 1)
    o_rev = chunk_mixer(_hgrn_heads, [qh, ih, (proj, 1024, 3), lb], 8, 128, 128, L, True)
    hg = chunk_mixer(_hgrn_heads, [qh, ih, (proj, 1024, 2), lb], 8, 128, 128, L, False,
                     final=((proj, 1024, 4), o_rev, hgrn_norm))
    return matmul((jnp.concatenate([att_c, att], axis=0), hg), w_out, STREAM_ROWS, 1024)


def _odd_mixer(h, L, w_in, w_out, gla_gate_up_f, gla_gate_up_b, gla_gate_bias_f, gla_gate_bias_b, gla_norm,
               mu_prev, mu_next, w0_f, w0_b, w2_f, w2_b, a0, a2, g2, k_k, k_a, r_k, ln_w, ln_b, to_bf16):
    T = h.shape[0]
    GO = 3104
    cols = lambda a, b: w_in[:, a:b]
    w_big = jnp.concatenate([cols(0, 2048), cols(2080, 3104), cols(GO, GO + 3072), cols(GO + 3360, GO + 3616)], axis=1)
    w_small = jnp.concatenate([cols(2048, 2080), cols(GO + 3072, GO + 3360),
                               jnp.zeros((w_in.shape[0], 64), F32)], axis=1)
    pb = matmul(h, w_big, STREAM_ROWS, 640)
    ps = matmul(h, w_small, STREAM_ROWS, 384)
    def gla_inputs(up, row0, bias):
        up_rows = jnp.zeros((ps.shape[1], up.shape[1]), F32).at[row0:row0 + up.shape[0]].set(up)
        return [(pb, 512, 0), (pb, 512, 1), (pb, 1024, 1), (ps, ps.shape[1], 0), (up_rows,), (bias.reshape(1, -1),)]
    o_rev = chunk_mixer(_gla_heads, gla_inputs(gla_gate_up_b, 16, gla_gate_bias_b), 4, 128, 256, L, True)
    gla = chunk_mixer(_gla_heads, gla_inputs(gla_gate_up_f, 0, gla_gate_bias_f), 4, 128, 256, L, False,
                      final=((pb, 1024, 2), o_rev, gla_norm))
    rw, cast_done = rwkv7_branch(pb, ps, 1, 24, L, mu_prev, mu_next, (32, 128, 224), w0_f, w0_b, w2_f, w2_b,
                                 a0, a2, g2, k_k, k_a, r_k, ln_w, ln_b, to_bf16)
    ycat = jnp.concatenate([gla[L:], rw[L:]], axis=-1)
    return matmul(ycat, w_out, LATENT_ROWS, 1024), cast_done


def kernel(x, c, ctx, c_ctx, hgrn_lb_logits, l0_ada_w, l0_ada_b, l0_norm_mix_pre, l0_norm_mix_post, l0_norm_ffn_pre, l0_norm_ffn_post, l0_w_in, l0_w_out, l0_attn_sink, l0_hgrn_norm, l0_ffn_w_gate, l0_ffn_w_up, l0_ffn_w_down, l1_ada_w, l1_ada_b, l1_norm_mix_pre, l1_norm_mix_post, l1_norm_ffn_pre, l1_norm_ffn_post, l1_w_in, l1_w_out, l1_gla_gate_up_f, l1_gla_gate_up_b, l1_gla_gate_bias_f, l1_gla_gate_bias_b, l1_gla_norm, l1_rwkv_mu_prev, l1_rwkv_mu_next, l1_rwkv_w0_f, l1_rwkv_w0_b, l1_rwkv_w2_f, l1_rwkv_w2_b, l1_rwkv_a0, l1_rwkv_a2, l1_rwkv_g2, l1_rwkv_k_k, l1_rwkv_k_a, l1_rwkv_r_k, l1_rwkv_ln_w, l1_rwkv_ln_b, l1_moe_router, l1_moe_w_gate, l1_moe_w_up, l1_moe_w_down):
    B, S, D = x.shape
    L = ctx.shape[1]
    assert B == 1
    T = L + S
    SH1, SC1, G1, SH2, SC2, G2 = range(6)
    xs = (ctx[0], x[0])
    hgrn_lb = jnp.cumsum(jax.nn.softmax(hgrn_lb_logits.astype(F32), axis=0), axis=0)
    m0 = _adaln(c, c_ctx, l0_ada_w, l0_ada_b)
    m1 = _adaln(c, c_ctx, l1_ada_w, l1_ada_b)

    h = normmod(xs, l0_norm_mix_pre, m0, SH1, SC1, L)
    y = _even_mixer(h, L, l0_w_in, l0_w_out, l0_attn_sink, l0_hgrn_norm, hgrn_lb[0])
    xa, h = resid_norm(xs, y, l0_norm_mix_post, m0, G1, L, nxt=(l0_norm_ffn_pre, m0, SH2, SC2))
    n_t = T // DENSE_TILE
    f = swiglu_ffn(h, l0_ffn_w_gate[None].astype(BF16), l0_ffn_w_up[None].astype(BF16),
                   [l0_ffn_w_down[None].astype(BF16)],
                   jnp.zeros((n_t,), jnp.int32), jnp.full((1,), n_t, jnp.int32), DENSE_TILE, DENSE_FF_BLOCK)
    xa, h = resid_norm(xa, f, l0_norm_ffn_post, m0, G2, L, nxt=(l1_norm_mix_pre, m1, SH1, SC1))

    E, _, FE = l1_moe_w_gate.shape
    wg2, wu2, wd2 = (l1_moe_w_gate.reshape(E * D, FE), l1_moe_w_up.reshape(E * D, FE),
                     l1_moe_w_down.reshape(E * FE, D))
    half = E * FE // 2
    y, ((wg16, wd16_lo), (wu16, wd16_hi)) = _odd_mixer(
        h, L, l1_w_in, l1_w_out, l1_gla_gate_up_f, l1_gla_gate_up_b, l1_gla_gate_bias_f,
        l1_gla_gate_bias_b, l1_gla_norm, l1_rwkv_mu_prev, l1_rwkv_mu_next, l1_rwkv_w0_f, l1_rwkv_w0_b,
        l1_rwkv_w2_f, l1_rwkv_w2_b, l1_rwkv_a0, l1_rwkv_a2, l1_rwkv_g2, l1_rwkv_k_k, l1_rwkv_k_a,
        l1_rwkv_r_k, l1_rwkv_ln_w, l1_rwkv_ln_b,
        to_bf16=([(wg2, 0, E * D), (wd2, 0, half)], [(wu2, 0, E * D), (wd2, half, half)]))
    xl, h = resid_norm(xa, y, l1_norm_mix_post, m1, G1, 0, nxt=(l1_norm_ffn_pre, m1, SH2, SC2), x_row0=L)
    router = jnp.concatenate([l1_moe_router, jnp.zeros((D, LANES - N_EXPERTS), F32)], axis=1)
    logits = matmul(h, router, LATENT_ROWS, LANES)[:, :N_EXPERTS]
    src, gate_w, dest, tile_expert, n_used = _route(logits, MOE_TILE)
    hs = gather_rows(h, src, TOK_BLOCK)
    ys = swiglu_ffn(hs, wg16.reshape(E, D, FE), wu16.reshape(E, D, FE),
                    [wd16_lo.reshape(E // 2, FE, D), wd16_hi.reshape(E // 2, FE, D)],
                    tile_expert, n_used, MOE_TILE, MOE_FF_BLOCK)
    out = combine_resid(xl, ys, dest, gate_w, l1_norm_ffn_post, m1, G2)
    return out[None]
```
